```python
import math
import jax, jax.numpy as jnp
from jax import lax
import numpy as np

D_MODEL = 2048
BATCH = 4
SEQ = 4096
DEPTH = 2

N_META = 16
BLOCK = 128
PAD = BLOCK - N_META

D_MIX = D_MODEL
ATTN_HEADS = 8
ATTN_HEAD_DIM = 64
D_ATTN = ATTN_HEADS * ATTN_HEAD_DIM
POOL_WINDOWS = (2, 4, 8, 16)
POOL_GROUPS = 4
D_POOL = 512
POOL_GROUP_DIM = D_POOL // POOL_GROUPS
D_SSD = D_MIX - D_ATTN - D_POOL
SSD_HEAD_DIM = 64
SSD_HEADS = D_SSD // SSD_HEAD_DIM
SSD_GROUPS = 2
SSD_HEADS_PER_GROUP = SSD_HEADS // SSD_GROUPS
SSD_STATE = 128
CONV_K = 4
D_CONV = D_SSD + 2 * SSD_GROUPS * SSD_STATE
SPLIT_SIZES = (D_ATTN, D_ATTN, D_ATTN, ATTN_HEADS, D_POOL, D_SSD, D_CONV, SSD_HEADS)
D_IN = sum(SPLIT_SIZES)
D_FF = 5632
ALPHA = (2 * DEPTH) ** 0.25
BETA = (8 * DEPTH) ** -0.25
LN_EPS = 1e-5
RMS_EPS = 1e-5
NEG_INF = -1e30

kernel_name = "hybrid_fox_pool_ssd_macaron_deepnorm"


def layer_norm(x, g, b):
    xf = x.astype(jnp.float32)
    mu = jnp.mean(xf, axis=-1, keepdims=True)
    var = jnp.mean(jnp.square(xf - mu), axis=-1, keepdims=True)
    return ((xf - mu) * lax.rsqrt(var + LN_EPS) * g + b).astype(x.dtype)


def swiglu(x, w_gate, w_up, w_down):
    return (jax.nn.silu(x @ w_gate) * (x @ w_up)) @ w_down


def pad_front(a):
    return jnp.pad(a, [(0, 0), (PAD, 0)] + [(0, 0)] * (a.ndim - 2))


def forgetting_attention(q, k, v, f_logit):
    b, l = q.shape[:2]
    lp = l + PAD
    nb = lp // BLOCK
    log_f = jax.nn.log_sigmoid(f_logit.astype(jnp.float32))
    c = jnp.cumsum(pad_front(log_f), axis=1).transpose(0, 2, 1)
    kf = pad_front(k).astype(jnp.float32)
    vf = pad_front(v).astype(jnp.float32)
    q_blocks = pad_front(q).astype(jnp.float32).reshape(
        b, nb, BLOCK, ATTN_HEADS, ATTN_HEAD_DIM).transpose(1, 0, 2, 3, 4)
    c_blocks = c.reshape(b, ATTN_HEADS, nb, BLOCK).transpose(2, 0, 1, 3)
    key_pos = jnp.arange(lp)
    scale = ATTN_HEAD_DIM ** -0.5

    def one_block(args):
        qb, cb, start = args
        s = jnp.einsum('bqhd,bkhd->bhqk', qb, kf) * scale
        s = s + cb[..., :, None] - c[:, :, None, :]
        q_pos = start + jnp.arange(BLOCK)
        mask = (key_pos[None, :] <= q_pos[:, None]) & (key_pos[None, :] >= PAD)
        p = jax.nn.softmax(jnp.where(mask, s, NEG_INF), axis=-1)
        return jnp.einsum('bhqk,bkhd->bqhd', p, vf)

    out = lax.map(one_block, (q_blocks, c_blocks, jnp.arange(nb) * BLOCK))
    out = out.transpose(1, 0, 2, 3, 4).reshape(b, lp, D_ATTN)[:, PAD:]
    return out.astype(q.dtype)


def multiscale_pool(u, pool_w, pool_scale):
    b, l, _ = u.shape
    uf = u.astype(jnp.float32).reshape(b, l, POOL_GROUPS, POOL_GROUP_DIM)
    cs0 = jnp.concatenate([jnp.zeros((b, 1, POOL_GROUPS, POOL_GROUP_DIM), jnp.float32),
                           jnp.cumsum(uf, axis=1)], axis=1)
    t = jnp.arange(l)
    means = []
    for g, w in enumerate(POOL_WINDOWS):
        upper = cs0[:, 1:, g]
        lower = jnp.concatenate([jnp.zeros((b, w - 1, POOL_GROUP_DIM), jnp.float32),
                                 cs0[:, :l + 1 - w, g]], axis=1)
        cnt = jnp.minimum(t + 1, w).astype(jnp.float32)
        means.append((upper - lower) / cnt[None, :, None])
    pooled = jnp.stack(means, axis=2)
    mixed = jnp.einsum('blgc,gcd->blgd', pooled - uf, pool_w)
    out = mixed.reshape(b, l, D_POOL) * pool_scale
    return out.astype(u.dtype)


def ssd_mixer(z, xbc, dt_raw, conv_w, conv_b, dt_bias, a_log, d_skip, norm_w):
    b, l, _ = z.shape
    lp = l + PAD
    nc = lp // BLOCK
    G, R, P, N = SSD_GROUPS, SSD_HEADS_PER_GROUP, SSD_HEAD_DIM, SSD_STATE
    xbc = lax.conv_general_dilated(xbc, conv_w[:, None, :], window_strides=(1,),
                                   padding=[(CONV_K - 1, 0)],
                                   dimension_numbers=('NWC', 'WIO', 'NWC'),
                                   feature_group_count=D_CONV) + conv_b
    xbc = jax.nn.silu(xbc.astype(jnp.float32))
    xs, bm, cm = jnp.split(xbc, [D_SSD, D_SSD + G * N], axis=-1)
    dt = jax.nn.softplus(dt_raw.astype(jnp.float32) + dt_bias)
    a = -jnp.exp(a_log.astype(jnp.float32)).reshape(G, R)
    xs = pad_front(xs).reshape(b, nc, BLOCK, G, R, P)
    bm = pad_front(bm).reshape(b, nc, BLOCK, G, N)
    cm = pad_front(cm).reshape(b, nc, BLOCK, G, N)
    dt_p = pad_front(dt).reshape(b, nc, BLOCK, G, R)
    x_dt = xs * dt_p[..., None]
    a_blk = (dt_p * a).transpose(0, 3, 4, 1, 2)
    a_cs = jnp.cumsum(a_blk, axis=-1)
    seg = a_cs[..., :, None] - a_cs[..., None, :]
    causal = jnp.tril(jnp.ones((BLOCK, BLOCK), dtype=bool))
    decay = jnp.exp(jnp.where(causal, seg, -jnp.inf))
    cb = jnp.einsum('bclgn,bcsgn->bgcls', cm, bm)
    y_diag = jnp.einsum('bgcls,bgrcls,bcsgrp->bclgrp', cb, decay, x_dt)
    decay_states = jnp.exp(a_cs[..., -1:] - a_cs)
    states = jnp.einsum('bclgn,bgrcl,bclgrp->bcgrpn', bm, decay_states, x_dt)
    chunk_decay = jnp.exp(a_cs[..., -1])

    def step(h, inp):
        s, d = inp
        return d[..., None, None] * h + s, h

    h0 = jnp.zeros((b, G, R, P, N), jnp.float32)
    _, prev = lax.scan(step, h0, (states.transpose(1, 0, 2, 3, 4, 5),
                                  chunk_decay.transpose(3, 0, 1, 2)))
    prev = prev.transpose(1, 0, 2, 3, 4, 5)
    y_off = jnp.einsum('bclgn,bcgrpn,bgrcl->bclgrp', cm, prev, jnp.exp(a_cs))
    y = y_diag + y_off + xs * d_skip.reshape(G, R)[..., None]
    y = y.reshape(b, lp, D_SSD)[:, PAD:]
    gy = (y * jax.nn.silu(z.astype(jnp.float32))).reshape(b, l, G, D_SSD // G)
    gy = gy * lax.rsqrt(jnp.mean(jnp.square(gy), axis=-1, keepdims=True) + RMS_EPS)
    return (gy.reshape(b, l, D_SSD) * norm_w).astype(z.dtype)


def hybrid_mixer(h, w_in, b_fgate, pool_w, pool_scale, conv_w, conv_b, dt_bias, a_log,
                 d_skip, ssd_norm_w, w_out):
    b, l, _ = h.shape
    proj = h @ w_in
    cuts = [int(v) for v in np.cumsum(SPLIT_SIZES)[:-1]]
    q, k, v, f_logit, u, z, xbc, dt_raw = jnp.split(proj, cuts, axis=-1)
    heads = (b, l, ATTN_HEADS, ATTN_HEAD_DIM)
    y_a = forgetting_attention(q.reshape(heads), k.reshape(heads), v.reshape(heads),
                               f_logit + b_fgate)
    y_b = multiscale_pool(u, pool_w, pool_scale)
    y_c = ssd_mixer(z, xbc, dt_raw, conv_w, conv_b, dt_bias, a_log, d_skip, ssd_norm_w)
    return jnp.concatenate([y_a, y_b, y_c], axis=-1) @ w_out


def setup_inputs(seed: int = 0) -> dict:
    key = jax.random.key(seed)
    ks = jax.random.split(key, 32)
    f32 = jnp.float32

    def nrm(i, shape, scale):
        return jax.random.normal(ks[i], shape, f32) * scale

    def gain(i, shape):
        return 1.0 + 0.02 * jax.random.normal(ks[i], shape, f32)

    D = D_MODEL
    dt0 = jnp.exp(jax.random.uniform(ks[13], (DEPTH, SSD_HEADS), f32)
                  * (math.log(0.1) - math.log(0.001)) + math.log(0.001))
    return {
        "x": nrm(0, (BATCH, SEQ, D), 1.0),
        "meta": nrm(1, (N_META, D), 1.0),
        "f1_gate": nrm(2, (DEPTH, D, D_FF), D ** -0.5),
        "f1_up": nrm(3, (DEPTH, D, D_FF), D ** -0.5),
        "f1_down": nrm(4, (DEPTH, D_FF, D), BETA * D_FF ** -0.5),
        "ln1_g": gain(5, (DEPTH, D)),
        "ln1_b": nrm(6, (DEPTH, D), 0.02),
        "w_in": nrm(7, (DEPTH, D, D_IN), D ** -0.5),
        "b_fgate": jax.random.uniform(ks[8], (DEPTH, ATTN_HEADS), f32, 1.0, 6.0),
        "pool_w": nrm(9, (DEPTH, POOL_GROUPS, POOL_GROUP_DIM, POOL_GROUP_DIM), POOL_GROUP_DIM ** -0.5),
        "pool_scale": gain(10, (DEPTH, D_POOL)),
        "conv_w": nrm(11, (DEPTH, CONV_K, D_CONV), CONV_K ** -0.5),
        "conv_b": nrm(12, (DEPTH, D_CONV), 0.02),
        "dt_bias": dt0 + jnp.log(-jnp.expm1(-dt0)),
        "a_log": jnp.log(jax.random.uniform(ks[14], (DEPTH, SSD_HEADS), f32, 1.0, 16.0)),
        "d_skip": gain(15, (DEPTH, SSD_HEADS)),
        "ssd_norm_w": gain(16, (DEPTH, D_SSD)),
        "w_out": nrm(17, (DEPTH, D_MIX, D), BETA * D_MIX ** -0.5),
        "ln2_g": gain(18, (DEPTH, D)),
        "ln2_b": nrm(19, (DEPTH, D), 0.02),
        "f2_gate": nrm(20, (DEPTH, D, D_FF), D ** -0.5),
        "f2_up": nrm(21, (DEPTH, D, D_FF), D ** -0.5),
        "f2_down": nrm(22, (DEPTH, D_FF, D), BETA * D_FF ** -0.5),
        "ln3_g": gain(23, (DEPTH, D)),
        "ln3_b": nrm(24, (DEPTH, D), 0.02),
    }


def reference(x, meta, f1_gate, f1_up, f1_down, ln1_g, ln1_b, w_in, b_fgate, pool_w,
              pool_scale, conv_w, conv_b, dt_bias, a_log, d_skip, ssd_norm_w, w_out,
              ln2_g, ln2_b, f2_gate, f2_up, f2_down, ln3_g, ln3_b):
    b = x.shape[0]
    h = jnp.concatenate([jnp.broadcast_to(meta[None].astype(x.dtype), (b, N_META, D_MODEL)), x],
                        axis=1)
    for i in range(DEPTH):
        h = layer_norm(ALPHA * h + 0.5 * swiglu(h, f1_gate[i], f1_up[i], f1_down[i]),
                       ln1_g[i], ln1_b[i])
        h = layer_norm(ALPHA * h + hybrid_mixer(h, w_in[i], b_fgate[i], pool_w[i], pool_scale[i],
                                                conv_w[i], conv_b[i], dt_bias[i], a_log[i],
                                                d_skip[i], ssd_norm_w[i], w_out[i]),
                       ln2_g[i], ln2_b[i])
        h = layer_norm(ALPHA * h + 0.5 * swiglu(h, f2_gate[i], f2_up[i], f2_down[i]),
                       ln3_g[i], ln3_b[i])
    return h[:, N_META:]
```

```python
import functools

import jax
import jax.numpy as jnp
from jax import lax
from jax.experimental import pallas as pl
from jax.experimental.pallas import tpu as pltpu

F32 = jnp.float32
BF16 = jnp.bfloat16

D_MODEL = 2048
N_META = 16
CHUNK = 128
LANES = 128

ATTN_HEADS = 8
ATTN_HEAD_DIM = 64
D_ATTN = ATTN_HEADS * ATTN_HEAD_DIM
POOL_WINDOWS = (2, 4, 8, 16)
POOL_GROUPS = 4
D_POOL = 512
POOL_GROUP_DIM = D_POOL // POOL_GROUPS
POOL_HALO = 16
D_SSD = 1024
SSD_HEAD_DIM = 64
SSD_HEADS = D_SSD // SSD_HEAD_DIM
SSD_GROUPS = 2
SSD_STATE = 128
CONV_K = 4
CONV_HALO = 8
D_CONV = D_SSD + 2 * SSD_GROUPS * SSD_STATE
D_FF_TILE = 512
DEPTH = 2
ALPHA = (2 * DEPTH) ** 0.25
LN_EPS = 1e-5
RMS_EPS = 1e-5
NEG_BIG = -1e30

PROJ_TILE = 512
REST_XBC, REST_POOL, REST_Z, REST_MISC = 0, D_CONV, D_CONV + D_POOL, D_CONV + D_POOL + D_SSD
D_REST = REST_MISC + PROJ_TILE
MISC_F0 = 0
MISC_DT0 = ATTN_HEADS

VMEM_LIMIT = 56 * 1024 * 1024


def _params(*sem):
    return pltpu.CompilerParams(dimension_semantics=sem, vmem_limit_bytes=VMEM_LIMIT)


def _pick_tile(n, candidates):
    for c in candidates:
        if n % c == 0:
            return c
    raise ValueError(f"no tile in {candidates} divides {n}")


def _layer_norm(y, g, b):
    mu = jnp.mean(y, axis=-1, keepdims=True)
    yc = y - mu
    var = jnp.mean(yc * yc, axis=-1, keepdims=True)
    return yc * lax.rsqrt(var + LN_EPS) * g + b


def _silu(x):
    return x * (1.0 / (1.0 + jnp.exp(-x)))


def _softplus(x):
    return jnp.maximum(x, 0.0) + jnp.log1p(jnp.exp(-jnp.abs(x)))


def _cumsum_rows(x):
    n = x.shape[0]
    row = lax.broadcasted_iota(jnp.int32, x.shape, 0)
    d = 1
    while d < n:
        x = x + jnp.where(row >= d, pltpu.roll(x, d, axis=0), 0.0)
        d *= 2
    return x


def _ffn_ln_kernel(x_ref, wg_ref, wu_ref, wd_ref, g_ref, b_ref, o_ref, xb_ref, acc_ref):
    f = pl.program_id(1)

    @pl.when(f == 0)
    def _():
        xb_ref[...] = x_ref[...].astype(BF16)
        acc_ref[...] = jnp.zeros_like(acc_ref)

    xb = xb_ref[...]
    gate = jnp.dot(xb, wg_ref[...], preferred_element_type=F32)
    up = jnp.dot(xb, wu_ref[...], preferred_element_type=F32)
    act = (_silu(gate) * up).astype(BF16)
    acc_ref[...] += jnp.dot(act, wd_ref[...], preferred_element_type=F32)

    @pl.when(f == pl.num_programs(1) - 1)
    def _():
        y = ALPHA * x_ref[...] + 0.5 * acc_ref[...]
        o_ref[...] = _layer_norm(y, g_ref[...], b_ref[...])


def _ffn_ln(h, wg, wu, wd, g, b):
    t, d = h.shape
    ff = wg.shape[1]
    tm = _pick_tile(t, (512, 384, 256, 128))
    tf = _pick_tile(ff, (D_FF_TILE, 256, 128))
    return pl.pallas_call(
        _ffn_ln_kernel,
        grid=(t // tm, ff // tf),
        in_specs=[
            pl.BlockSpec((tm, d), lambda i, f: (i, 0)),
            pl.BlockSpec((d, tf), lambda i, f: (0, f)),
            pl.BlockSpec((d, tf), lambda i, f: (0, f)),
            pl.BlockSpec((tf, d), lambda i, f: (f, 0)),
            pl.BlockSpec((1, d), lambda i, f: (0, 0)),
            pl.BlockSpec((1, d), lambda i, f: (0, 0)),
        ],
        out_specs=pl.BlockSpec((tm, d), lambda i, f: (i, 0)),
        out_shape=jax.ShapeDtypeStruct((t, d), F32),
        scratch_shapes=[pltpu.VMEM((tm, d), BF16), pltpu.VMEM((tm, d), F32)],
        compiler_params=_params("parallel", "arbitrary"),
        name="ffn_ln",
    )(h, wg, wu, wd, g, b)


N_QKV_TILES = 3 * D_ATTN // PROJ_TILE


def _proj_kernel(x_ref, w_ref, qkv_ref, rest_ref, xb_ref):
    j = pl.program_id(1)

    @pl.when(j == 0)
    def _():
        xb_ref[...] = x_ref[...].astype(BF16)

    r = jnp.dot(xb_ref[...], w_ref[...], preferred_element_type=F32)

    @pl.when(j < N_QKV_TILES)
    def _():
        qkv_ref[...] = r.astype(BF16)

    @pl.when(j >= N_QKV_TILES)
    def _():
        rest_ref[...] = r


def _proj(h, w):
    t, d = h.shape
    n = w.shape[1]
    tm = _pick_tile(t, (512, 384, 256, 128))
    return pl.pallas_call(
        _proj_kernel,
        grid=(t // tm, n // PROJ_TILE),
        in_specs=[
            pl.BlockSpec((tm, d), lambda i, j: (i, 0)),
            pl.BlockSpec((d, PROJ_TILE), lambda i, j: (0, j)),
        ],
        out_specs=[
            pl.BlockSpec((tm, PROJ_TILE), lambda i, j: (i, jnp.minimum(j, N_QKV_TILES - 1))),
            pl.BlockSpec((tm, PROJ_TILE), lambda i, j: (i, jnp.maximum(j - N_QKV_TILES, 0))),
        ],
        out_shape=[
            jax.ShapeDtypeStruct((t, 3 * D_ATTN), BF16),
            jax.ShapeDtypeStruct((t, D_REST), F32),
        ],
        scratch_shapes=[pltpu.VMEM((tm, d), BF16)],
        compiler_params=_params("parallel", "arbitrary"),
        name="in_proj",
    )(h, w)


def _gate_kernel(m_ref, bf_ref, c_ref, ct_ref, carry_ref):
    @pl.when(pl.program_id(1) == 0)
    def _():
        carry_ref[...] = jnp.zeros_like(carry_ref)

    x = m_ref[...] + bf_ref[...]
    log_f = jnp.minimum(x, 0.0) - jnp.log1p(jnp.exp(-jnp.abs(x)))
    c = _cumsum_rows(log_f) + carry_ref[...]
    carry_ref[...] = c[CHUNK - 1:CHUNK, :]
    c_ref[...] = c
    ct_ref[0, 0] = c.T[:ATTN_HEADS, :]


def _gate_cumsum(rest, bf, nb, nblk):
    t = rest.shape[0]
    misc_blk = REST_MISC // LANES
    return pl.pallas_call(
        _gate_kernel,
        grid=(nb, nblk),
        in_specs=[
            pl.BlockSpec((CHUNK, LANES), lambda b, j: (b * nblk + j, misc_blk)),
            pl.BlockSpec((1, LANES), lambda b, j: (0, 0)),
        ],
        out_specs=[
            pl.BlockSpec((CHUNK, LANES), lambda b, j: (b * nblk + j, 0)),
            pl.BlockSpec((1, 1, ATTN_HEADS, CHUNK), lambda b, j: (b, j, 0, 0)),
        ],
        out_shape=[
            jax.ShapeDtypeStruct((t, LANES), F32),
            jax.ShapeDtypeStruct((nb, nblk, ATTN_HEADS, CHUNK), F32),
        ],
        scratch_shapes=[pltpu.VMEM((1, LANES), F32)],
        compiler_params=_params("parallel", "arbitrary"),
        name="gate_cumsum",
    )(rest, bf)


def _attn_kernel(q_ref, k_ref, v_ref, c_ref, ct_ref, o_ref):
    i = pl.program_id(1)
    row = lax.broadcasted_iota(jnp.int32, (CHUNK, CHUNK), 0)
    col = lax.broadcasted_iota(jnp.int32, (CHUNK, CHUNK), 1)
    scale = ATTN_HEAD_DIM ** -0.5
    c_blk = c_ref[...]
    for h in range(ATTN_HEADS):
        lo, hi = h * ATTN_HEAD_DIM, (h + 1) * ATTN_HEAD_DIM
        qh = q_ref[:, lo:hi]
        c_q = c_blk[:, h:h + 1]

        def body(j, carry, lo=lo, hi=hi, qh=qh, c_q=c_q, h=h):
            m, l, acc = carry
            start = pl.multiple_of(j * CHUNK, CHUNK)
            kj = k_ref[pl.ds(start, CHUNK), lo:hi]
            vj = v_ref[pl.ds(start, CHUNK), lo:hi]
            s = lax.dot_general(qh, kj, (((1,), (1,)), ((), ())), preferred_element_type=F32)
            s = s * scale + c_q - ct_ref[0, j, h:h + 1, :]
            s = jnp.where(col + (j - i) * CHUNK <= row, s, NEG_BIG)
            m_new = jnp.maximum(m, jnp.max(s, axis=1, keepdims=True))
            a = jnp.exp(m - m_new)
            p = jnp.exp(s - m_new)
            l = a * l + jnp.sum(p, axis=1, keepdims=True)
            acc = a * acc + jnp.dot(p.astype(BF16), vj, preferred_element_type=F32)
            return m_new, l, acc

        init = (jnp.full((CHUNK, 1), NEG_BIG, F32), jnp.zeros((CHUNK, 1), F32),
                jnp.zeros((CHUNK, ATTN_HEAD_DIM), F32))
        m, l, acc = lax.fori_loop(0, i + 1, body, init)
        o_ref[:, lo:hi] = (acc / l).astype(BF16)


def _attention(qkv, c, ct, nb, nblk):
    t = qkv.shape[0]
    lp = nblk * CHUNK
    return pl.pallas_call(
        _attn_kernel,
        grid=(nb, nblk),
        in_specs=[
            pl.BlockSpec((CHUNK, D_ATTN), lambda b, i: (b * nblk + i, 0)),
            pl.BlockSpec((lp, D_ATTN), lambda b, i: (b, 1)),
            pl.BlockSpec((lp, D_ATTN), lambda b, i: (b, 2)),
            pl.BlockSpec((CHUNK, LANES), lambda b, i: (b * nblk + i, 0)),
            pl.BlockSpec((1, nblk, ATTN_HEADS, CHUNK), lambda b, i: (b, 0, 0, 0)),
        ],
        out_specs=pl.BlockSpec((CHUNK, D_ATTN), lambda b, i: (b * nblk + i, 0)),
        out_shape=jax.ShapeDtypeStruct((t, D_ATTN), BF16),
        compiler_params=_params("parallel", "arbitrary"),
        name="fox_attention",
    )(qkv, qkv, qkv, c, ct)


def _pool_kernel(u_ref, pw_ref, ps_ref, o_ref, buf_ref, *, tp):
    t = pl.program_id(1)

    @pl.when(t == 0)
    def _():
        buf_ref[0:POOL_HALO, :] = jnp.zeros((POOL_HALO, D_POOL), F32)

    @pl.when(t > 0)
    def _():
        buf_ref[0:POOL_HALO, :] = buf_ref[tp:tp + POOL_HALO, :]

    buf_ref[POOL_HALO:POOL_HALO + tp, :] = u_ref[...]
    pos = t * tp + lax.broadcasted_iota(jnp.int32, (tp, POOL_GROUP_DIM), 0)
    for g, w in enumerate(POOL_WINDOWS):
        lo, hi = g * POOL_GROUP_DIM, (g + 1) * POOL_GROUP_DIM
        u = buf_ref[POOL_HALO:POOL_HALO + tp, lo:hi]
        win = u
        for j in range(1, w):
            win = win + buf_ref[POOL_HALO - j:POOL_HALO - j + tp, lo:hi]
        cnt = jnp.minimum(pos + 1, w).astype(F32)
        diff = (win / cnt - u).astype(BF16)
        mixed = jnp.dot(diff, pw_ref[g], preferred_element_type=F32)
        o_ref[:, lo:hi] = (mixed * ps_ref[:, lo:hi]).astype(BF16)


def _pool(rest, pw, ps, nb, nblk):
    t = rest.shape[0]
    lp = nblk * CHUNK
    tp = _pick_tile(lp, (384, 256, 128))
    nt = lp // tp
    col_blk = REST_POOL // D_POOL
    return pl.pallas_call(
        functools.partial(_pool_kernel, tp=tp),
        grid=(nb, nt),
        in_specs=[
            pl.BlockSpec((tp, D_POOL), lambda b, i: (b * nt + i, col_blk)),
            pl.BlockSpec((POOL_GROUPS, POOL_GROUP_DIM, POOL_GROUP_DIM), lambda b, i: (0, 0, 0)),
            pl.BlockSpec((1, D_POOL), lambda b, i: (0, 0)),
        ],
        out_specs=pl.BlockSpec((tp, D_POOL), lambda b, i: (b * nt + i, 0)),
        out_shape=jax.ShapeDtypeStruct((t, D_POOL), BF16),
        scratch_shapes=[pltpu.VMEM((POOL_HALO + tp, D_POOL), F32)],
        compiler_params=_params("parallel", "arbitrary"),
        name="pool_mixer",
    )(rest, pw, ps)


def _expand_heads(v):
    r = v.shape[0]
    lane = lax.broadcasted_iota(jnp.int32, (r, LANES), 1)
    parts = []
    for k in range(SSD_HEADS // 2):
        l0 = MISC_DT0 + 2 * k
        a = jnp.broadcast_to(v[:, l0:l0 + 1], (r, LANES))
        b = jnp.broadcast_to(v[:, l0 + 1:l0 + 2], (r, LANES))
        parts.append(jnp.where(lane < SSD_HEAD_DIM, a, b))
    return jnp.concatenate(parts, axis=1)


def _ssd_kernel(xbc_ref, z_ref, misc_ref, cw_ref, cb_ref, dtb_ref, a_ref, dsk_ref, nw_ref,
                o_ref, state_ref, ext_ref):
    c = pl.program_id(1)

    @pl.when(c == 0)
    def _():
        state_ref[...] = jnp.zeros_like(state_ref)
        ext_ref[0:CONV_HALO, :] = jnp.zeros((CONV_HALO, D_CONV), F32)

    @pl.when(c > 0)
    def _():
        ext_ref[0:CONV_HALO, :] = ext_ref[CHUNK:CHUNK + CONV_HALO, :]

    ext_ref[CONV_HALO:CONV_HALO + CHUNK, :] = xbc_ref[...]

    conv = cb_ref[...]
    for k in range(CONV_K):
        off = CONV_HALO - (CONV_K - 1) + k
        conv = conv + cw_ref[k:k + 1, :] * ext_ref[off:off + CHUNK, :]
    xc = _silu(conv)
    xs = xc[:, :D_SSD]
    gn = SSD_GROUPS * SSD_STATE
    bm = xc[:, D_SSD:D_SSD + gn]
    cm = xc[:, D_SSD + gn:D_SSD + 2 * gn]

    dt = _softplus(misc_ref[...] + dtb_ref[...])
    a_cs = _cumsum_rows(dt * a_ref[...])
    a_cs_t = a_cs.T
    a_last = a_cs[CHUNK - 1:CHUNK, :]
    x_dt = xs * _expand_heads(dt)
    x_dt_b = x_dt.astype(BF16)
    decay_out = _expand_heads(jnp.exp(a_cs))
    x_state = (x_dt * _expand_heads(jnp.exp(a_last - a_cs))).astype(BF16)
    chunk_decay = _expand_heads(jnp.exp(a_last))

    row = lax.broadcasted_iota(jnp.int32, (CHUNK, CHUNK), 0)
    col = lax.broadcasted_iota(jnp.int32, (CHUNK, CHUNK), 1)
    causal = col <= row
    hpg = SSD_HEADS // SSD_GROUPS
    gw = hpg * SSD_HEAD_DIM
    for g in range(SSD_GROUPS):
        n0, n1 = g * SSD_STATE, (g + 1) * SSD_STATE
        cm_g = cm[:, n0:n1].astype(BF16)
        bm_g = bm[:, n0:n1]
        cb = lax.dot_general(cm_g, bm_g.astype(BF16), (((1,), (1,)), ((), ())),
                             preferred_element_type=F32)
        st = state_ref[:, g * gw:(g + 1) * gw]
        y_off = jnp.dot(cm_g, st.astype(BF16), preferred_element_type=F32)
        y_g = y_off * decay_out[:, g * gw:(g + 1) * gw]
        diag = []
        for r in range(hpg):
            h = g * hpg + r
            lane = MISC_DT0 + h
            seg = a_cs[:, lane:lane + 1] - a_cs_t[lane:lane + 1, :]
            m = (cb * jnp.exp(jnp.where(causal, seg, NEG_BIG))).astype(BF16)
            diag.append(jnp.dot(m, x_dt_b[:, h * SSD_HEAD_DIM:(h + 1) * SSD_HEAD_DIM],
                                preferred_element_type=F32))
        y_g = y_g + jnp.concatenate(diag, axis=1)
        new = jnp.dot(bm_g.T.astype(BF16), x_state[:, g * gw:(g + 1) * gw],
                      preferred_element_type=F32)
        state_ref[:, g * gw:(g + 1) * gw] = chunk_decay[:, g * gw:(g + 1) * gw] * st + new

        sl = slice(g * gw, (g + 1) * gw)
        y_g = y_g + xs[:, sl] * dsk_ref[:, sl]
        gy = y_g * _silu(z_ref[:, sl])
        ms = jnp.mean(gy * gy, axis=-1, keepdims=True)
        o_ref[:, sl] = (gy * lax.rsqrt(ms + RMS_EPS) * nw_ref[:, sl]).astype(BF16)


def _ssd(rest, cw, cb, dtb, a_neg, dsk, nw, nb, nblk):
    t = rest.shape[0]
    vec = lambda n: pl.BlockSpec((1, n), lambda b, c: (0, 0))
    return pl.pallas_call(
        _ssd_kernel,
        grid=(nb, nblk),
        in_specs=[
            pl.BlockSpec((CHUNK, D_CONV), lambda b, c: (b * nblk + c, REST_XBC // D_CONV)),
            pl.BlockSpec((CHUNK, D_SSD), lambda b, c: (b * nblk + c, REST_Z // D_SSD)),
            pl.BlockSpec((CHUNK, LANES), lambda b, c: (b * nblk + c, REST_MISC // LANES)),
            pl.BlockSpec((CONV_K, D_CONV), lambda b, c: (0, 0)),
            vec(D_CONV), vec(LANES), vec(LANES), vec(D_SSD), vec(D_SSD),
        ],
        out_specs=pl.BlockSpec((CHUNK, D_SSD), lambda b, c: (b * nblk + c, 0)),
        out_shape=jax.ShapeDtypeStruct((t, D_SSD), BF16),
        scratch_shapes=[pltpu.VMEM((SSD_STATE, D_SSD), F32),
                        pltpu.VMEM((CONV_HALO + CHUNK, D_CONV), F32)],
        compiler_params=_params("parallel", "arbitrary"),
        name="ssd_mixer",
    )(rest, rest, rest, cw, cb, dtb, a_neg, dsk, nw)


def _out_ln_kernel(ya_ref, yb_ref, yc_ref, h_ref, w_ref, g_ref, b_ref, o_ref):
    acc = jnp.dot(ya_ref[...], w_ref[0:D_ATTN, :], preferred_element_type=F32)
    acc += jnp.dot(yb_ref[...], w_ref[D_ATTN:D_ATTN + D_POOL, :], preferred_element_type=F32)
    acc += jnp.dot(yc_ref[...], w_ref[D_ATTN + D_POOL:, :], preferred_element_type=F32)
    o_ref[...] = _layer_norm(ALPHA * h_ref[...] + acc, g_ref[...], b_ref[...])


def _out_ln(ya, yb, yc, h, w, g, b):
    t, d = h.shape
    tm = _pick_tile(t, (512, 384, 256, 128))
    rows = lambda n: pl.BlockSpec((tm, n), lambda i: (i, 0))
    return pl.pallas_call(
        _out_ln_kernel,
        grid=(t // tm,),
        in_specs=[
            rows(D_ATTN), rows(D_POOL), rows(D_SSD), rows(d),
            pl.BlockSpec((d, d), lambda i: (0, 0)),
            pl.BlockSpec((1, d), lambda i: (0, 0)),
            pl.BlockSpec((1, d), lambda i: (0, 0)),
        ],
        out_specs=rows(d),
        out_shape=jax.ShapeDtypeStruct((t, d), F32),
        compiler_params=_params("parallel"),
        name="out_proj_ln",
    )(ya, yb, yc, h, w, g, b)


def _lane_row(vals, offset):
    return jnp.zeros((1, LANES), F32).at[0, offset:offset + vals.shape[0]].set(vals.astype(F32))


def _proj_weight(w_in):
    c = D_ATTN
    q_k_v = w_in[:, :3 * c]
    f = w_in[:, 3 * c:3 * c + ATTN_HEADS]
    o = 3 * c + ATTN_HEADS
    pool = w_in[:, o:o + D_POOL]
    z = w_in[:, o + D_POOL:o + D_POOL + D_SSD]
    xbc = w_in[:, o + D_POOL + D_SSD:o + D_POOL + D_SSD + D_CONV]
    dt = w_in[:, o + D_POOL + D_SSD + D_CONV:]
    pad = jnp.zeros((w_in.shape[0], PROJ_TILE - ATTN_HEADS - SSD_HEADS), w_in.dtype)
    return jnp.concatenate([q_k_v, xbc, pool, z, f, dt, pad], axis=1).astype(BF16)


def _mixer(h, nb, nblk, w_in, b_fgate, pool_w, pool_scale, conv_w, conv_b, dt_bias, a_log,
           d_skip, ssd_norm_w, w_out, ln_g, ln_b):
    qkv, rest = _proj(h, _proj_weight(w_in))
    c, ct = _gate_cumsum(rest, _lane_row(b_fgate, MISC_F0), nb, nblk)
    ya = _attention(qkv, c, ct, nb, nblk)
    yb = _pool(rest, pool_w.astype(BF16), pool_scale.reshape(1, D_POOL), nb, nblk)
    yc = _ssd(rest, conv_w, conv_b.reshape(1, D_CONV), _lane_row(dt_bias, MISC_DT0),
              _lane_row(-jnp.exp(a_log.astype(F32)), MISC_DT0),
              jnp.repeat(d_skip, SSD_HEAD_DIM).reshape(1, D_SSD),
              ssd_norm_w.reshape(1, D_SSD), nb, nblk)
    return _out_ln(ya, yb, yc, h, w_out.astype(BF16), ln_g, ln_b)


def kernel(x, meta, f1_gate, f1_up, f1_down, ln1_g, ln1_b, w_in, b_fgate, pool_w, pool_scale,
           conv_w, conv_b, dt_bias, a_log, d_skip, ssd_norm_w, w_out, ln2_g, ln2_b, f2_gate,
           f2_up, f2_down, ln3_g, ln3_b):
    nb, seq, d = x.shape
    assert d == D_MODEL and meta.shape == (N_META, D_MODEL)
    l = N_META + seq
    nblk = pl.cdiv(l, CHUNK)
    lp = nblk * CHUNK
    h = jnp.concatenate([jnp.broadcast_to(meta[None].astype(x.dtype), (nb, N_META, d)), x,
                         jnp.zeros((nb, lp - l, d), x.dtype)], axis=1).reshape(nb * lp, d)
    row = lambda v: v.reshape(1, d)
    for i in range(f1_gate.shape[0]):
        h = _ffn_ln(h, f1_gate[i].astype(BF16), f1_up[i].astype(BF16), f1_down[i].astype(BF16),
                    row(ln1_g[i]), row(ln1_b[i]))
        h = _mixer(h, nb, nblk, w_in[i], b_fgate[i], pool_w[i], pool_scale[i], conv_w[i],
                   conv_b[i], dt_bias[i], a_log[i], d_skip[i], ssd_norm_w[i], w_out[i],
                   row(ln2_g[i]), row(ln2_b[i]))
        h = _ffn_ln(h, f2_gate[i].astype(BF16), f2_up[i].astype(BF16), f2_down[i].astype(BF16),
                    row(ln3_g[i]), row(ln3_b[i]))
    return h.reshape(nb, lp, d)[:, N_META:l]
```

```python
import functools

import jax
import jax.numpy as jnp
from jax import lax
from jax.experimental import pallas as pl
from jax.experimental.pallas import tpu as pltpu

F32 = jnp.float32
BF16 = jnp.bfloat16

D_MODEL = 2048
N_META = 16
CHUNK = 128
LANES = 128

ATTN_HEADS = 8
ATTN_HEAD_DIM = 64
D_ATTN = ATTN_HEADS * ATTN_HEAD_DIM
POOL_WINDOWS = (2, 4, 8, 16)
POOL_GROUPS = 4
D_POOL = 512
POOL_GROUP_DIM = D_POOL // POOL_GROUPS
POOL_HALO = 16
D_SSD = 1024
SSD_HEAD_DIM = 64
SSD_HEADS = D_SSD // SSD_HEAD_DIM
SSD_GROUPS = 2
SSD_STATE = 128
CONV_K = 4
CONV_HALO = 8
D_CONV = D_SSD + 2 * SSD_GROUPS * SSD_STATE
D_FF_TILE = 512
DEPTH = 2
ALPHA = (2 * DEPTH) ** 0.25
LN_EPS = 1e-5
RMS_EPS = 1e-5
NEG_BIG = -1e30

PROJ_TILE = 512
REST_XBC, REST_POOL, REST_Z, REST_MISC = 0, D_CONV, D_CONV + D_POOL, D_CONV + D_POOL + D_SSD
D_REST = REST_MISC + PROJ_TILE
MISC_F0 = 0
MISC_DT0 = ATTN_HEADS

VMEM_LIMIT = 56 * 1024 * 1024


def _params(*sem):
    return pltpu.CompilerParams(dimension_semantics=sem, vmem_limit_bytes=VMEM_LIMIT)


def _pick_tile(n, candidates):
    for c in candidates:
        if n % c == 0:
            return c
    raise ValueError(f"no tile in {candidates} divides {n}")


def _layer_norm(y, g, b):
    mu = jnp.mean(y, axis=-1, keepdims=True)
    yc = y - mu
    var = jnp.mean(yc * yc, axis=-1, keepdims=True)
    return yc * lax.rsqrt(var + LN_EPS) * g + b


def _silu(x):
    return x * (1.0 / (1.0 + jnp.exp(-x)))


def _softplus(x):
    return jnp.maximum(x, 0.0) + jnp.log1p(jnp.exp(-jnp.abs(x)))


def _cumsum_rows(x):
    n = x.shape[0]
    row = lax.broadcasted_iota(jnp.int32, x.shape, 0)
    d = 1
    while d < n:
        x = x + jnp.where(row >= d, pltpu.roll(x, d, axis=0), 0.0)
        d *= 2
    return x


def _ffn_ln_kernel(x_ref, wg_ref, wu_ref, wd_ref, g_ref, b_ref, o_ref, xb_ref, acc_ref):
    f = pl.program_id(1)

    @pl.when(f == 0)
    def _():
        xb_ref[...] = x_ref[...].astype(BF16)
        acc_ref[...] = jnp.zeros_like(acc_ref)

    xb = xb_ref[...]
    gate = jnp.dot(xb, wg_ref[...], preferred_element_type=F32)
    up = jnp.dot(xb, wu_ref[...], preferred_element_type=F32)
    act = (_silu(gate) * up).astype(BF16)
    acc_ref[...] += jnp.dot(act, wd_ref[...], preferred_element_type=F32)

    @pl.when(f == pl.num_programs(1) - 1)
    def _():
        y = ALPHA * x_ref[...] + 0.5 * acc_ref[...]
        o_ref[...] = _layer_norm(y, g_ref[...], b_ref[...])


def _ffn_ln(h, wg, wu, wd, g, b):
    t, d = h.shape
    ff = wg.shape[1]
    tm = _pick_tile(t, (512, 384, 256, 128))
    tf = _pick_tile(ff, (D_FF_TILE, 256, 128))
    return pl.pallas_call(
        _ffn_ln_kernel,
        grid=(t // tm, ff // tf),
        in_specs=[
            pl.BlockSpec((tm, d), lambda i, f: (i, 0)),
            pl.BlockSpec((d, tf), lambda i, f: (0, f)),
            pl.BlockSpec((d, tf), lambda i, f: (0, f)),
            pl.BlockSpec((tf, d), lambda i, f: (f, 0)),
            pl.BlockSpec((1, d), lambda i, f: (0, 0)),
            pl.BlockSpec((1, d), lambda i, f: (0, 0)),
        ],
        out_specs=pl.BlockSpec((tm, d), lambda i, f: (i, 0)),
        out_shape=jax.ShapeDtypeStruct((t, d), F32),
        scratch_shapes=[pltpu.VMEM((tm, d), BF16), pltpu.VMEM((tm, d), F32)],
        compiler_params=_params("parallel", "arbitrary"),
        name="ffn_ln",
    )(h, wg, wu, wd, g, b)


N_QKV_TILES = 3 * D_ATTN // PROJ_TILE


def _proj_kernel(x_ref, w_ref, qkv_ref, rest_ref, xb_ref):
    j = pl.program_id(1)

    @pl.when(j == 0)
    def _():
        xb_ref[...] = x_ref[...].astype(BF16)

    r = jnp.dot(xb_ref[...], w_ref[...], preferred_element_type=F32)

    @pl.when(j < N_QKV_TILES)
    def _():
        qkv_ref[...] = r.astype(BF16)

    @pl.when(j >= N_QKV_TILES)
    def _():
        rest_ref[...] = r


def _proj(h, w):
    t, d = h.shape
    n = w.shape[1]
    tm = _pick_tile(t, (512, 384, 256, 128))
    return pl.pallas_call(
        _proj_kernel,
        grid=(t // tm, n // PROJ_TILE),
        in_specs=[
            pl.BlockSpec((tm, d), lambda i, j: (i, 0)),
            pl.BlockSpec((d, PROJ_TILE), lambda i, j: (0, j)),
        ],
        out_specs=[
            pl.BlockSpec((tm, PROJ_TILE), lambda i, j: (i, jnp.minimum(j, N_QKV_TILES - 1))),
            pl.BlockSpec((tm, PROJ_TILE), lambda i, j: (i, jnp.maximum(j - N_QKV_TILES, 0))),
        ],
        out_shape=[
            jax.ShapeDtypeStruct((t, 3 * D_ATTN), BF16),
            jax.ShapeDtypeStruct((t, D_REST), F32),
        ],
        scratch_shapes=[pltpu.VMEM((tm, d), BF16)],
        compiler_params=_params("parallel", "arbitrary"),
        name="in_proj",
    )(h, w)


AUG_Q = ATTN_HEAD_DIM
VT_ROWS = ATTN_HEAD_DIM + 16
ATTN_TQ = 2 * CHUNK


def _split3(c):
    hi = c.astype(BF16).astype(F32)
    mid = (c - hi).astype(BF16).astype(F32)
    lo = (c - hi - mid).astype(BF16).astype(F32)
    return hi, mid, lo


def _gate_kernel(m_ref, bf_ref, qkv_ref, qa_ref, ka_ref, vt_ref, carry_ref):
    @pl.when(pl.program_id(1) == 0)
    def _():
        carry_ref[...] = jnp.zeros_like(carry_ref)

    x = m_ref[...] + bf_ref[...]
    log_f = jnp.minimum(x, 0.0) - jnp.log1p(jnp.exp(-jnp.abs(x)))
    c = _cumsum_rows(log_f) + carry_ref[...]
    carry_ref[...] = c[CHUNK - 1:CHUNK, :]
    parts = _split3(c)

    lane = lax.broadcasted_iota(jnp.int32, (CHUNK, LANES), 1)
    scale = ATTN_HEAD_DIM ** -0.5
    ones_row = jnp.where(lax.broadcasted_iota(jnp.int32, (VT_ROWS - ATTN_HEAD_DIM, CHUNK), 0) == 0,
                         1.0, 0.0).astype(BF16)
    for pair in range(ATTN_HEADS // 2):
        q2 = qkv_ref[:, pair * LANES:(pair + 1) * LANES].astype(F32) * scale
        k2 = qkv_ref[:, D_ATTN + pair * LANES:D_ATTN + (pair + 1) * LANES].astype(F32)
        v2t = qkv_ref[:, 2 * D_ATTN + pair * LANES:2 * D_ATTN + (pair + 1) * LANES].astype(F32).T
        for sub in range(2):
            h = 2 * pair + sub
            qh = q2 if sub == 0 else pltpu.roll(q2, ATTN_HEAD_DIM, axis=1)
            kh = k2 if sub == 0 else pltpu.roll(k2, ATTN_HEAD_DIM, axis=1)
            q_extra = jnp.where((lane >= AUG_Q + 3) & (lane < AUG_Q + 6), 1.0, 0.0)
            k_extra = jnp.where((lane >= AUG_Q) & (lane < AUG_Q + 3), 1.0, 0.0)
            for n, part in enumerate(parts):
                col = jnp.broadcast_to(part[:, h:h + 1], (CHUNK, LANES))
                q_extra = jnp.where(lane == AUG_Q + n, col, q_extra)
                k_extra = jnp.where(lane == AUG_Q + 3 + n, -col, k_extra)
            qa_ref[0, :, h * LANES:(h + 1) * LANES] = jnp.where(lane < AUG_Q, qh, q_extra).astype(BF16)
            ka_ref[0, :, h * LANES:(h + 1) * LANES] = jnp.where(lane < AUG_Q, kh, k_extra).astype(BF16)
            vt_ref[0, 0, h, 0:ATTN_HEAD_DIM, :] = (
                v2t[sub * ATTN_HEAD_DIM:(sub + 1) * ATTN_HEAD_DIM, :].astype(BF16))
            vt_ref[0, 0, h, ATTN_HEAD_DIM:VT_ROWS, :] = ones_row


def _gate_prep(rest, qkv, bf, nb, nblk):
    misc_blk = REST_MISC // LANES
    aug = ATTN_HEADS * LANES
    nq = pl.cdiv(nblk, 2)
    src = lambda b, j: b * nblk + jnp.minimum(j, nblk - 1)
    return pl.pallas_call(
        _gate_kernel,
        grid=(nb, 2 * nq),
        in_specs=[
            pl.BlockSpec((CHUNK, LANES), lambda b, j: (src(b, j), misc_blk)),
            pl.BlockSpec((1, LANES), lambda b, j: (0, 0)),
            pl.BlockSpec((CHUNK, 3 * D_ATTN), lambda b, j: (src(b, j), 0)),
        ],
        out_specs=[
            pl.BlockSpec((1, CHUNK, aug), lambda b, j: (b, j, 0)),
            pl.BlockSpec((1, CHUNK, aug), lambda b, j: (b, j, 0)),
            pl.BlockSpec((1, 1, ATTN_HEADS, VT_ROWS, CHUNK), lambda b, j: (b, j // 2, 0, 0, j % 2)),
        ],
        out_shape=[
            jax.ShapeDtypeStruct((nb, nq * ATTN_TQ, aug), BF16),
            jax.ShapeDtypeStruct((nb, nq * ATTN_TQ, aug), BF16),
            jax.ShapeDtypeStruct((nb, nq, ATTN_HEADS, VT_ROWS, ATTN_TQ), BF16),
        ],
        scratch_shapes=[pltpu.VMEM((1, LANES), F32)],
        compiler_params=_params("parallel", "arbitrary"),
        name="gate_prep",
    )(rest, bf, qkv)


def _attn_kernel(qa_ref, ka_ref, vt_ref, o_ref, acc_ref, st_ref):
    i = pl.program_id(1)
    key = lax.broadcasted_iota(jnp.int32, (CHUNK, ATTN_TQ), 0)
    qry = lax.broadcasted_iota(jnp.int32, (CHUNK, ATTN_TQ), 1)
    acc_ref[...] = jnp.zeros_like(acc_ref)

    def scores(start, nkeys):
        for h in range(ATTN_HEADS):
            ka = ka_ref[0, pl.ds(start, nkeys), h * LANES:(h + 1) * LANES]
            qa = qa_ref[0, :, h * LANES:(h + 1) * LANES]
            st_ref[h, 0:nkeys, :] = lax.dot_general(ka, qa, (((1,), (1,)), ((), ())),
                                                    preferred_element_type=F32)

    def softmax_pv(ms, nkeys, vt_of, mask):
        out = []
        for h in range(ATTN_HEADS):
            st = st_ref[h, 0:nkeys, :]
            if mask is not None:
                st = jnp.where(mask, st, NEG_BIG)
            m_new = jnp.maximum(ms[h], jnp.max(st, axis=0, keepdims=True))
            alpha = jnp.exp(ms[h] - m_new)
            p = jnp.exp(st - m_new).astype(BF16)
            acc_ref[h] = alpha * acc_ref[h] + jnp.dot(vt_of(h), p, preferred_element_type=F32)
            out.append(m_new)
        return tuple(out)

    def pair_step(jj, ms):
        scores(pl.multiple_of(jj * ATTN_TQ, ATTN_TQ), ATTN_TQ)
        return softmax_pv(ms, ATTN_TQ, lambda h: vt_ref[0, jj, h], None)

    ms = tuple(jnp.full((1, ATTN_TQ), NEG_BIG, F32) for _ in range(ATTN_HEADS))
    ms = lax.fori_loop(0, i, pair_step, ms)
    for d in range(ATTN_TQ // CHUNK):
        scores(pl.multiple_of(i * ATTN_TQ + d * CHUNK, CHUNK), CHUNK)
        ms = softmax_pv(ms, CHUNK, lambda h, d=d: vt_ref[0, i, h, :, d * CHUNK:(d + 1) * CHUNK],
                        key + d * CHUNK <= qry)

    for pair in range(ATTN_HEADS // 2):
        halves = []
        for sub in range(2):
            a = acc_ref[2 * pair + sub]
            halves.append(a[0:ATTN_HEAD_DIM, :] * (1.0 / a[ATTN_HEAD_DIM:ATTN_HEAD_DIM + 1, :]))
        o_ref[0, :, pair * LANES:(pair + 1) * LANES] = (
            jnp.concatenate(halves, axis=0).T.astype(BF16))


def _attention(qa, ka, vt):
    nb, lq, aug = qa.shape
    nq = lq // ATTN_TQ
    return pl.pallas_call(
        _attn_kernel,
        grid=(nb, nq),
        in_specs=[
            pl.BlockSpec((1, ATTN_TQ, aug), lambda b, i: (b, i, 0)),
            pl.BlockSpec((1, lq, aug), lambda b, i: (b, 0, 0)),
            pl.BlockSpec((1, nq, ATTN_HEADS, VT_ROWS, ATTN_TQ), lambda b, i: (b, 0, 0, 0, 0)),
        ],
        out_specs=pl.BlockSpec((1, ATTN_TQ, D_ATTN), lambda b, i: (b, i, 0)),
        out_shape=jax.ShapeDtypeStruct((nb, lq, D_ATTN), BF16),
        scratch_shapes=[pltpu.VMEM((ATTN_HEADS, VT_ROWS, ATTN_TQ), F32),
                        pltpu.VMEM((ATTN_HEADS, ATTN_TQ, ATTN_TQ), F32)],
        compiler_params=_params("parallel", "arbitrary"),
        name="fox_attention",
    )(qa, ka, vt)


def _pool_kernel(u_ref, pw_ref, ps_ref, o_ref, buf_ref, *, tp):
    t = pl.program_id(1)

    @pl.when(t == 0)
    def _():
        buf_ref[0:POOL_HALO, :] = jnp.zeros((POOL_HALO, D_POOL), F32)

    @pl.when(t > 0)
    def _():
        buf_ref[0:POOL_HALO, :] = buf_ref[tp:tp + POOL_HALO, :]

    buf_ref[POOL_HALO:POOL_HALO + tp, :] = u_ref[...]
    pos = t * tp + lax.broadcasted_iota(jnp.int32, (tp, POOL_GROUP_DIM), 0)
    for g, w in enumerate(POOL_WINDOWS):
        lo, hi = g * POOL_GROUP_DIM, (g + 1) * POOL_GROUP_DIM
        u = buf_ref[POOL_HALO:POOL_HALO + tp, lo:hi]
        win = u
        for j in range(1, w):
            win = win + buf_ref[POOL_HALO - j:POOL_HALO - j + tp, lo:hi]
        cnt = jnp.minimum(pos + 1, w).astype(F32)
        diff = (win / cnt - u).astype(BF16)
        mixed = jnp.dot(diff, pw_ref[g], preferred_element_type=F32)
        o_ref[:, lo:hi] = (mixed * ps_ref[:, lo:hi]).astype(BF16)


def _pool(rest, pw, ps, nb, nblk):
    t = rest.shape[0]
    lp = nblk * CHUNK
    tp = _pick_tile(lp, (384, 256, 128))
    nt = lp // tp
    col_blk = REST_POOL // D_POOL
    return pl.pallas_call(
        functools.partial(_pool_kernel, tp=tp),
        grid=(nb, nt),
        in_specs=[
            pl.BlockSpec((tp, D_POOL), lambda b, i: (b * nt + i, col_blk)),
            pl.BlockSpec((POOL_GROUPS, POOL_GROUP_DIM, POOL_GROUP_DIM), lambda b, i: (0, 0, 0)),
            pl.BlockSpec((1, D_POOL), lambda b, i: (0, 0)),
        ],
        out_specs=pl.BlockSpec((tp, D_POOL), lambda b, i: (b * nt + i, 0)),
        out_shape=jax.ShapeDtypeStruct((t, D_POOL), BF16),
        scratch_shapes=[pltpu.VMEM((POOL_HALO + tp, D_POOL), F32)],
        compiler_params=_params("parallel", "arbitrary"),
        name="pool_mixer",
    )(rest, pw, ps)


def _expand_heads(v):
    r = v.shape[0]
    lane = lax.broadcasted_iota(jnp.int32, (r, LANES), 1)
    parts = []
    for k in range(SSD_HEADS // 2):
        l0 = MISC_DT0 + 2 * k
        a = jnp.broadcast_to(v[:, l0:l0 + 1], (r, LANES))
        b = jnp.broadcast_to(v[:, l0 + 1:l0 + 2], (r, LANES))
        parts.append(jnp.where(lane < SSD_HEAD_DIM, a, b))
    return jnp.concatenate(parts, axis=1)


def _ssd_kernel(xbc_ref, z_ref, misc_ref, cw_ref, cb_ref, dtb_ref, a_ref, dsk_ref, nw_ref,
                o_ref, state_ref, ext_ref):
    c = pl.program_id(1)

    @pl.when(c == 0)
    def _():
        state_ref[...] = jnp.zeros_like(state_ref)
        ext_ref[0:CONV_HALO, :] = jnp.zeros((CONV_HALO, D_CONV), F32)

    @pl.when(c > 0)
    def _():
        ext_ref[0:CONV_HALO, :] = ext_ref[CHUNK:CHUNK + CONV_HALO, :]

    ext_ref[CONV_HALO:CONV_HALO + CHUNK, :] = xbc_ref[...]

    conv = cb_ref[...]
    for k in range(CONV_K):
        off = CONV_HALO - (CONV_K - 1) + k
        conv = conv + cw_ref[k:k + 1, :] * ext_ref[off:off + CHUNK, :]
    xc = _silu(conv)
    xs = xc[:, :D_SSD]
    gn = SSD_GROUPS * SSD_STATE
    bm = xc[:, D_SSD:D_SSD + gn]
    cm = xc[:, D_SSD + gn:D_SSD + 2 * gn]

    dt = _softplus(misc_ref[...] + dtb_ref[...])
    a_cs = _cumsum_rows(dt * a_ref[...])
    a_cs_t = a_cs.T
    a_last = a_cs[CHUNK - 1:CHUNK, :]
    x_dt = xs * _expand_heads(dt)
    x_dt_b = x_dt.astype(BF16)
    decay_out = _expand_heads(jnp.exp(a_cs))
    x_state = (x_dt * _expand_heads(jnp.exp(a_last - a_cs))).astype(BF16)
    chunk_decay = _expand_heads(jnp.exp(a_last))

    row = lax.broadcasted_iota(jnp.int32, (CHUNK, CHUNK), 0)
    col = lax.broadcasted_iota(jnp.int32, (CHUNK, CHUNK), 1)
    causal = col <= row
    hpg = SSD_HEADS // SSD_GROUPS
    gw = hpg * SSD_HEAD_DIM
    for g in range(SSD_GROUPS):
        n0, n1 = g * SSD_STATE, (g + 1) * SSD_STATE
        cm_g = cm[:, n0:n1].astype(BF16)
        bm_g = bm[:, n0:n1]
        cb = lax.dot_general(cm_g, bm_g.astype(BF16), (((1,), (1,)), ((), ())),
                             preferred_element_type=F32)
        st = state_ref[:, g * gw:(g + 1) * gw]
        y_off = jnp.dot(cm_g, st.astype(BF16), preferred_element_type=F32)
        y_g = y_off * decay_out[:, g * gw:(g + 1) * gw]
        diag = []
        for r in range(hpg):
            h = g * hpg + r
            lane = MISC_DT0 + h
            seg = a_cs[:, lane:lane + 1] - a_cs_t[lane:lane + 1, :]
            m = (cb * jnp.exp(jnp.where(causal, seg, NEG_BIG))).astype(BF16)
            diag.append(jnp.dot(m, x_dt_b[:, h * SSD_HEAD_DIM:(h + 1) * SSD_HEAD_DIM],
                                preferred_element_type=F32))
        y_g = y_g + jnp.concatenate(diag, axis=1)
        new = jnp.dot(bm_g.T.astype(BF16), x_state[:, g * gw:(g + 1) * gw],
                      preferred_element_type=F32)
        state_ref[:, g * gw:(g + 1) * gw] = chunk_decay[:, g * gw:(g + 1) * gw] * st + new

        sl = slice(g * gw, (g + 1) * gw)
        y_g = y_g + xs[:, sl] * dsk_ref[:, sl]
        gy = y_g * _silu(z_ref[:, sl])
        ms = jnp.mean(gy * gy, axis=-1, keepdims=True)
        o_ref[:, sl] = (gy * lax.rsqrt(ms + RMS_EPS) * nw_ref[:, sl]).astype(BF16)


def _ssd(rest, cw, cb, dtb, a_neg, dsk, nw, nb, nblk):
    t = rest.shape[0]
    vec = lambda n: pl.BlockSpec((1, n), lambda b, c: (0, 0))
    return pl.pallas_call(
        _ssd_kernel,
        grid=(nb, nblk),
        in_specs=[
            pl.BlockSpec((CHUNK, D_CONV), lambda b, c: (b * nblk + c, REST_XBC // D_CONV)),
            pl.BlockSpec((CHUNK, D_SSD), lambda b, c: (b * nblk + c, REST_Z // D_SSD)),
            pl.BlockSpec((CHUNK, LANES), lambda b, c: (b * nblk + c, REST_MISC // LANES)),
            pl.BlockSpec((CONV_K, D_CONV), lambda b, c: (0, 0)),
            vec(D_CONV), vec(LANES), vec(LANES), vec(D_SSD), vec(D_SSD),
        ],
        out_specs=pl.BlockSpec((CHUNK, D_SSD), lambda b, c: (b * nblk + c, 0)),
        out_shape=jax.ShapeDtypeStruct((t, D_SSD), BF16),
        scratch_shapes=[pltpu.VMEM((SSD_STATE, D_SSD), F32),
                        pltpu.VMEM((CONV_HALO + CHUNK, D_CONV), F32)],
        compiler_params=_params("parallel", "arbitrary"),
        name="ssd_mixer",
    )(rest, rest, rest, cw, cb, dtb, a_neg, dsk, nw)


def _out_ln_kernel(ya_ref, yb_ref, yc_ref, h_ref, w_ref, g_ref, b_ref, o_ref):
    acc = jnp.dot(ya_ref[...], w_ref[0:D_ATTN, :], preferred_element_type=F32)
    acc += jnp.dot(yb_ref[...], w_ref[D_ATTN:D_ATTN + D_POOL, :], preferred_element_type=F32)
    acc += jnp.dot(yc_ref[...], w_ref[D_ATTN + D_POOL:, :], preferred_element_type=F32)
    o_ref[...] = _layer_norm(ALPHA * h_ref[...] + acc, g_ref[...], b_ref[...])


def _out_ln(ya, yb, yc, h, w, g, b):
    t, d = h.shape
    tm = _pick_tile(t, (512, 384, 256, 128))
    rows = lambda n: pl.BlockSpec((tm, n), lambda i: (i, 0))
    return pl.pallas_call(
        _out_ln_kernel,
        grid=(t // tm,),
        in_specs=[
            rows(D_ATTN), rows(D_POOL), rows(D_SSD), rows(d),
            pl.BlockSpec((d, d), lambda i: (0, 0)),
            pl.BlockSpec((1, d), lambda i: (0, 0)),
            pl.BlockSpec((1, d), lambda i: (0, 0)),
        ],
        out_specs=rows(d),
        out_shape=jax.ShapeDtypeStruct((t, d), F32),
        compiler_params=_params("parallel"),
        name="out_proj_ln",
    )(ya, yb, yc, h, w, g, b)


def _lane_row(vals, offset):
    return jnp.zeros((1, LANES), F32).at[0, offset:offset + vals.shape[0]].set(vals.astype(F32))


def _proj_weight(w_in):
    c = D_ATTN
    q_k_v = w_in[:, :3 * c]
    f = w_in[:, 3 * c:3 * c + ATTN_HEADS]
    o = 3 * c + ATTN_HEADS
    pool = w_in[:, o:o + D_POOL]
    z = w_in[:, o + D_POOL:o + D_POOL + D_SSD]
    xbc = w_in[:, o + D_POOL + D_SSD:o + D_POOL + D_SSD + D_CONV]
    dt = w_in[:, o + D_POOL + D_SSD + D_CONV:]
    pad = jnp.zeros((w_in.shape[0], PROJ_TILE - ATTN_HEADS - SSD_HEADS), w_in.dtype)
    return jnp.concatenate([q_k_v, xbc, pool, z, f, dt, pad], axis=1).astype(BF16)


def _mixer(h, nb, nblk, w_in, b_fgate, pool_w, pool_scale, conv_w, conv_b, dt_bias, a_log,
           d_skip, ssd_norm_w, w_out, ln_g, ln_b):
    qkv, rest = _proj(h, _proj_weight(w_in))
    qa, ka, vt = _gate_prep(rest, qkv, _lane_row(b_fgate, MISC_F0), nb, nblk)
    ya = _attention(qa, ka, vt)[:, :nblk * CHUNK].reshape(nb * nblk * CHUNK, D_ATTN)
    yb = _pool(rest, pool_w.astype(BF16), pool_scale.reshape(1, D_POOL), nb, nblk)
    yc = _ssd(rest, conv_w, conv_b.reshape(1, D_CONV), _lane_row(dt_bias, MISC_DT0),
              _lane_row(-jnp.exp(a_log.astype(F32)), MISC_DT0),
              jnp.repeat(d_skip, SSD_HEAD_DIM).reshape(1, D_SSD),
              ssd_norm_w.reshape(1, D_SSD), nb, nblk)
    return _out_ln(ya, yb, yc, h, w_out.astype(BF16), ln_g, ln_b)


def kernel(x, meta, f1_gate, f1_up, f1_down, ln1_g, ln1_b, w_in, b_fgate, pool_w, pool_scale,
           conv_w, conv_b, dt_bias, a_log, d_skip, ssd_norm_w, w_out, ln2_g, ln2_b, f2_gate,
           f2_up, f2_down, ln3_g, ln3_b):
    nb, seq, d = x.shape
    assert d == D_MODEL and meta.shape == (N_META, D_MODEL)
    l = N_META + seq
    nblk = pl.cdiv(l, CHUNK)
    lp = nblk * CHUNK
    h = jnp.concatenate([jnp.broadcast_to(meta[None].astype(x.dtype), (nb, N_META, d)), x,
                         jnp.zeros((nb, lp - l, d), x.dtype)], axis=1).reshape(nb * lp, d)
    row = lambda v: v.reshape(1, d)
    for i in range(f1_gate.shape[0]):
        h = _ffn_ln(h, f1_gate[i].astype(BF16), f1_up[i].astype(BF16), f1_down[i].astype(BF16),
                    row(ln1_g[i]), row(ln1_b[i]))
        h = _mixer(h, nb, nblk, w_in[i], b_fgate[i], pool_w[i], pool_scale[i], conv_w[i],
                   conv_b[i], dt_bias[i], a_log[i], d_skip[i], ssd_norm_w[i], w_out[i],
                   row(ln2_g[i]), row(ln2_b[i]))
        h = _ffn_ln(h, f2_gate[i].astype(BF16), f2_up[i].astype(BF16), f2_down[i].astype(BF16),
                    row(ln3_g[i]), row(ln3_b[i]))
    return h.reshape(nb, lp, d)[:, N_META:l]
```

```python
import functools

import jax
import jax.numpy as jnp
from jax import lax
from jax.experimental import pallas as pl
from jax.experimental.pallas import tpu as pltpu

F32 = jnp.float32
BF16 = jnp.bfloat16

D_MODEL = 2048
N_META = 16
CHUNK = 128
LANES = 128

ATTN_HEADS = 8
ATTN_HEAD_DIM = 64
D_ATTN = ATTN_HEADS * ATTN_HEAD_DIM
POOL_WINDOWS = (2, 4, 8, 16)
POOL_GROUPS = 4
D_POOL = 512
POOL_GROUP_DIM = D_POOL // POOL_GROUPS
POOL_HALO = 16
D_SSD = 1024
SSD_HEAD_DIM = 64
SSD_HEADS = D_SSD // SSD_HEAD_DIM
SSD_GROUPS = 2
SSD_STATE = 128
CONV_K = 4
CONV_HALO = 8
D_CONV = D_SSD + 2 * SSD_GROUPS * SSD_STATE
D_FF_TILE = 512
FFN_ROW_TILES = (512, 384, 256, 128)
DEPTH = 2
ALPHA = (2 * DEPTH) ** 0.25
LN_EPS = 1e-5
RMS_EPS = 1e-5
NEG_BIG = -1e30

PROJ_TILE = 512
REST_XBC, REST_POOL, REST_Z, REST_MISC = 0, D_CONV, D_CONV + D_POOL, D_CONV + D_POOL + D_SSD
D_REST = REST_MISC + PROJ_TILE
MISC_F0 = 0
MISC_DT0 = ATTN_HEADS

VMEM_LIMIT = 56 * 1024 * 1024


def _params(*sem):
    return pltpu.CompilerParams(dimension_semantics=sem, vmem_limit_bytes=VMEM_LIMIT)


def _pick_tile(n, candidates):
    for c in candidates:
        if n % c == 0:
            return c
    raise ValueError(f"no tile in {candidates} divides {n}")


def _layer_norm(y, g, b):
    mu = jnp.mean(y, axis=-1, keepdims=True)
    yc = y - mu
    var = jnp.mean(yc * yc, axis=-1, keepdims=True)
    return yc * lax.rsqrt(var + LN_EPS) * g + b


def _silu(x):
    return x * (1.0 / (1.0 + jnp.exp(-x)))


def _softplus(x):
    return jnp.maximum(x, 0.0) + jnp.log1p(jnp.exp(-jnp.abs(x)))


def _cumsum_rows(x):
    n = x.shape[0]
    row = lax.broadcasted_iota(jnp.int32, x.shape, 0)
    d = 1
    while d < n:
        x = x + jnp.where(row >= d, pltpu.roll(x, d, axis=0), 0.0)
        d *= 2
    return x


def _ffn_ln_kernel(x_ref, wg_ref, wu_ref, wd_ref, g_ref, b_ref, o_ref, xb_ref, acc_ref):
    f = pl.program_id(1)

    @pl.when(f == 0)
    def _():
        xb_ref[...] = x_ref[...].astype(BF16)
        acc_ref[...] = jnp.zeros_like(acc_ref)

    xb = xb_ref[...]
    gate = jnp.dot(xb, wg_ref[...], preferred_element_type=F32)
    up = jnp.dot(xb, wu_ref[...], preferred_element_type=F32)
    act = (_silu(gate) * up).astype(BF16)
    acc_ref[...] += jnp.dot(act, wd_ref[...], preferred_element_type=F32)

    @pl.when(f == pl.num_programs(1) - 1)
    def _():
        y = ALPHA * x_ref[...] + 0.5 * acc_ref[...]
        o_ref[...] = _layer_norm(y, g_ref[...], b_ref[...])


def _ffn_ln(h, wg, wu, wd, g, b, layer):
    t, d = h.shape
    ff = wg.shape[2]
    tm = _pick_tile(t, FFN_ROW_TILES)
    tf = _pick_tile(ff, (D_FF_TILE, 256, 128))
    return pl.pallas_call(
        _ffn_ln_kernel,
        grid=(t // tm, ff // tf),
        in_specs=[
            pl.BlockSpec((tm, d), lambda i, f: (i, 0)),
            pl.BlockSpec((None, d, tf), lambda i, f: (layer, 0, f)),
            pl.BlockSpec((None, d, tf), lambda i, f: (layer, 0, f)),
            pl.BlockSpec((None, tf, d), lambda i, f: (layer, f, 0)),
            pl.BlockSpec((1, d), lambda i, f: (0, 0)),
            pl.BlockSpec((1, d), lambda i, f: (0, 0)),
        ],
        out_specs=pl.BlockSpec((tm, d), lambda i, f: (i, 0)),
        out_shape=jax.ShapeDtypeStruct((t, d), F32),
        scratch_shapes=[pltpu.VMEM((tm, d), BF16), pltpu.VMEM((tm, d), F32)],
        compiler_params=_params("parallel", "arbitrary"),
        name="ffn_ln",
    )(h, wg, wu, wd, g, b)


N_QKV_TILES = 3 * D_ATTN // PROJ_TILE


def _proj_kernel(x_ref, w_ref, qkv_ref, rest_ref):
    xb = x_ref[...].astype(BF16)
    for j in range(w_ref.shape[1] // PROJ_TILE):
        r = jnp.dot(xb, w_ref[:, j * PROJ_TILE:(j + 1) * PROJ_TILE], preferred_element_type=F32)
        if j < N_QKV_TILES:
            qkv_ref[:, j * PROJ_TILE:(j + 1) * PROJ_TILE] = r.astype(BF16)
        else:
            k = j - N_QKV_TILES
            rest_ref[:, k * PROJ_TILE:(k + 1) * PROJ_TILE] = r


def _proj(h, w, layer):
    t, d = h.shape
    n = w.shape[2]
    tm = _pick_tile(t, (512, 384, 256, 128))
    return pl.pallas_call(
        _proj_kernel,
        grid=(t // tm,),
        in_specs=[
            pl.BlockSpec((tm, d), lambda i: (i, 0)),
            pl.BlockSpec((None, d, n), lambda i: (layer, 0, 0), pipeline_mode=pl.Buffered(1)),
        ],
        out_specs=[
            pl.BlockSpec((tm, 3 * D_ATTN), lambda i: (i, 0)),
            pl.BlockSpec((tm, D_REST), lambda i: (i, 0)),
        ],
        out_shape=[
            jax.ShapeDtypeStruct((t, 3 * D_ATTN), BF16),
            jax.ShapeDtypeStruct((t, D_REST), F32),
        ],
        compiler_params=_params("parallel"),
        name="in_proj",
    )(h, w)


AUG_Q = ATTN_HEAD_DIM
VT_ROWS = ATTN_HEAD_DIM + 16
ATTN_TQ = 2 * CHUNK


def _split3(c):
    hi = c.astype(BF16).astype(F32)
    mid = (c - hi).astype(BF16).astype(F32)
    lo = (c - hi - mid).astype(BF16).astype(F32)
    return hi, mid, lo


def _gate_kernel(m_ref, bf_ref, qkv_ref, qa_ref, ka_ref, vt_ref, carry_ref):
    @pl.when(pl.program_id(1) == 0)
    def _():
        carry_ref[...] = jnp.zeros_like(carry_ref)

    x = m_ref[...] + bf_ref[...]
    log_f = jnp.minimum(x, 0.0) - jnp.log1p(jnp.exp(-jnp.abs(x)))
    c = _cumsum_rows(log_f) + carry_ref[...]
    carry_ref[...] = c[CHUNK - 1:CHUNK, :]
    parts = _split3(c)

    lane = lax.broadcasted_iota(jnp.int32, (CHUNK, LANES), 1)
    scale = ATTN_HEAD_DIM ** -0.5
    ones_row = jnp.where(lax.broadcasted_iota(jnp.int32, (VT_ROWS - ATTN_HEAD_DIM, CHUNK), 0) == 0,
                         1.0, 0.0).astype(BF16)
    for pair in range(ATTN_HEADS // 2):
        q2 = qkv_ref[:, pair * LANES:(pair + 1) * LANES].astype(F32) * scale
        k2 = qkv_ref[:, D_ATTN + pair * LANES:D_ATTN + (pair + 1) * LANES].astype(F32)
        v2t = qkv_ref[:, 2 * D_ATTN + pair * LANES:2 * D_ATTN + (pair + 1) * LANES].astype(F32).T
        for sub in range(2):
            h = 2 * pair + sub
            qh = q2 if sub == 0 else pltpu.roll(q2, ATTN_HEAD_DIM, axis=1)
            kh = k2 if sub == 0 else pltpu.roll(k2, ATTN_HEAD_DIM, axis=1)
            q_extra = jnp.where((lane >= AUG_Q + 3) & (lane < AUG_Q + 6), 1.0, 0.0)
            k_extra = jnp.where((lane >= AUG_Q) & (lane < AUG_Q + 3), 1.0, 0.0)
            for n, part in enumerate(parts):
                col = jnp.broadcast_to(part[:, h:h + 1], (CHUNK, LANES))
                q_extra = jnp.where(lane == AUG_Q + n, col, q_extra)
                k_extra = jnp.where(lane == AUG_Q + 3 + n, -col, k_extra)
            qa_ref[0, :, h * LANES:(h + 1) * LANES] = jnp.where(lane < AUG_Q, qh, q_extra).astype(BF16)
            ka_ref[0, :, h * LANES:(h + 1) * LANES] = jnp.where(lane < AUG_Q, kh, k_extra).astype(BF16)
            vt_ref[0, 0, h, 0:ATTN_HEAD_DIM, :] = (
                v2t[sub * ATTN_HEAD_DIM:(sub + 1) * ATTN_HEAD_DIM, :].astype(BF16))
            vt_ref[0, 0, h, ATTN_HEAD_DIM:VT_ROWS, :] = ones_row


def _gate_prep(rest, qkv, bf, nb, nblk):
    misc_blk = REST_MISC // LANES
    aug = ATTN_HEADS * LANES
    nq = pl.cdiv(nblk, 2)
    src = lambda b, j: b * nblk + jnp.minimum(j, nblk - 1)
    return pl.pallas_call(
        _gate_kernel,
        grid=(nb, 2 * nq),
        in_specs=[
            pl.BlockSpec((CHUNK, LANES), lambda b, j: (src(b, j), misc_blk)),
            pl.BlockSpec((1, LANES), lambda b, j: (0, 0)),
            pl.BlockSpec((CHUNK, 3 * D_ATTN), lambda b, j: (src(b, j), 0)),
        ],
        out_specs=[
            pl.BlockSpec((1, CHUNK, aug), lambda b, j: (b, j, 0)),
            pl.BlockSpec((1, CHUNK, aug), lambda b, j: (b, j, 0)),
            pl.BlockSpec((1, 1, ATTN_HEADS, VT_ROWS, CHUNK), lambda b, j: (b, j // 2, 0, 0, j % 2)),
        ],
        out_shape=[
            jax.ShapeDtypeStruct((nb, nq * ATTN_TQ, aug), BF16),
            jax.ShapeDtypeStruct((nb, nq * ATTN_TQ, aug), BF16),
            jax.ShapeDtypeStruct((nb, nq, ATTN_HEADS, VT_ROWS, ATTN_TQ), BF16),
        ],
        scratch_shapes=[pltpu.VMEM((1, LANES), F32)],
        compiler_params=_params("parallel", "arbitrary"),
        name="gate_prep",
    )(rest, bf, qkv)


def _attn_kernel(qa_ref, ka_ref, vt_ref, o_ref, acc_ref, st_ref):
    i = pl.program_id(1)
    key = lax.broadcasted_iota(jnp.int32, (CHUNK, ATTN_TQ), 0)
    qry = lax.broadcasted_iota(jnp.int32, (CHUNK, ATTN_TQ), 1)
    acc_ref[...] = jnp.zeros_like(acc_ref)

    def scores(start, nkeys):
        for h in range(ATTN_HEADS):
            ka = ka_ref[0, pl.ds(start, nkeys), h * LANES:(h + 1) * LANES]
            qa = qa_ref[0, :, h * LANES:(h + 1) * LANES]
            st_ref[h, 0:nkeys, :] = lax.dot_general(ka, qa, (((1,), (1,)), ((), ())),
                                                    preferred_element_type=F32)

    def softmax_pv(ms, nkeys, vt_of, mask):
        out = []
        for h in range(ATTN_HEADS):
            st = st_ref[h, 0:nkeys, :]
            if mask is not None:
                st = jnp.where(mask, st, NEG_BIG)
            m_new = jnp.maximum(ms[h], jnp.max(st, axis=0, keepdims=True))
            alpha = jnp.exp(ms[h] - m_new)
            p = jnp.exp(st - m_new).astype(BF16)
            acc_ref[h] = alpha * acc_ref[h] + jnp.dot(vt_of(h), p, preferred_element_type=F32)
            out.append(m_new)
        return tuple(out)

    def pair_step(jj, ms):
        scores(pl.multiple_of(jj * ATTN_TQ, ATTN_TQ), ATTN_TQ)
        return softmax_pv(ms, ATTN_TQ, lambda h: vt_ref[0, jj, h], None)

    ms = tuple(jnp.full((1, ATTN_TQ), NEG_BIG, F32) for _ in range(ATTN_HEADS))
    ms = lax.fori_loop(0, i, pair_step, ms)
    for d in range(ATTN_TQ // CHUNK):
        scores(pl.multiple_of(i * ATTN_TQ + d * CHUNK, CHUNK), CHUNK)
        ms = softmax_pv(ms, CHUNK, lambda h, d=d: vt_ref[0, i, h, :, d * CHUNK:(d + 1) * CHUNK],
                        key + d * CHUNK <= qry)

    for pair in range(ATTN_HEADS // 2):
        halves = []
        for sub in range(2):
            a = acc_ref[2 * pair + sub]
            halves.append(a[0:ATTN_HEAD_DIM, :] * (1.0 / a[ATTN_HEAD_DIM:ATTN_HEAD_DIM + 1, :]))
        o_ref[0, :, pair * LANES:(pair + 1) * LANES] = (
            jnp.concatenate(halves, axis=0).T.astype(BF16))


def _attention(qa, ka, vt):
    nb, lq, aug = qa.shape
    nq = lq // ATTN_TQ
    return pl.pallas_call(
        _attn_kernel,
        grid=(nb, nq),
        in_specs=[
            pl.BlockSpec((1, ATTN_TQ, aug), lambda b, i: (b, i, 0)),
            pl.BlockSpec((1, lq, aug), lambda b, i: (b, 0, 0)),
            pl.BlockSpec((1, nq, ATTN_HEADS, VT_ROWS, ATTN_TQ), lambda b, i: (b, 0, 0, 0, 0)),
        ],
        out_specs=pl.BlockSpec((1, ATTN_TQ, D_ATTN), lambda b, i: (b, i, 0)),
        out_shape=jax.ShapeDtypeStruct((nb, lq, D_ATTN), BF16),
        scratch_shapes=[pltpu.VMEM((ATTN_HEADS, VT_ROWS, ATTN_TQ), F32),
                        pltpu.VMEM((ATTN_HEADS, ATTN_TQ, ATTN_TQ), F32)],
        compiler_params=_params("parallel", "arbitrary"),
        name="fox_attention",
    )(qa, ka, vt)


def _pool_kernel(u_ref, pw_ref, ps_ref, o_ref, buf_ref, *, tp):
    t = pl.program_id(1)

    @pl.when(t == 0)
    def _():
        buf_ref[0:POOL_HALO, :] = jnp.zeros((POOL_HALO, D_POOL), F32)

    @pl.when(t > 0)
    def _():
        buf_ref[0:POOL_HALO, :] = buf_ref[tp:tp + POOL_HALO, :]

    buf_ref[POOL_HALO:POOL_HALO + tp, :] = u_ref[...]
    pos = t * tp + lax.broadcasted_iota(jnp.int32, (tp, POOL_GROUP_DIM), 0)
    for g, w in enumerate(POOL_WINDOWS):
        lo, hi = g * POOL_GROUP_DIM, (g + 1) * POOL_GROUP_DIM
        u = buf_ref[POOL_HALO:POOL_HALO + tp, lo:hi]
        win = u
        for j in range(1, w):
            win = win + buf_ref[POOL_HALO - j:POOL_HALO - j + tp, lo:hi]
        cnt = jnp.minimum(pos + 1, w).astype(F32)
        diff = (win / cnt - u).astype(BF16)
        mixed = jnp.dot(diff, pw_ref[g], preferred_element_type=F32)
        o_ref[:, lo:hi] = (mixed * ps_ref[:, lo:hi]).astype(BF16)


def _pool(rest, pw, ps, nb, nblk):
    t = rest.shape[0]
    lp = nblk * CHUNK
    tp = _pick_tile(lp, (384, 256, 128))
    nt = lp // tp
    col_blk = REST_POOL // D_POOL
    return pl.pallas_call(
        functools.partial(_pool_kernel, tp=tp),
        grid=(nb, nt),
        in_specs=[
            pl.BlockSpec((tp, D_POOL), lambda b, i: (b * nt + i, col_blk)),
            pl.BlockSpec((POOL_GROUPS, POOL_GROUP_DIM, POOL_GROUP_DIM), lambda b, i: (0, 0, 0)),
            pl.BlockSpec((1, D_POOL), lambda b, i: (0, 0)),
        ],
        out_specs=pl.BlockSpec((tp, D_POOL), lambda b, i: (b * nt + i, 0)),
        out_shape=jax.ShapeDtypeStruct((t, D_POOL), BF16),
        scratch_shapes=[pltpu.VMEM((POOL_HALO + tp, D_POOL), F32)],
        compiler_params=_params("parallel", "arbitrary"),
        name="pool_mixer",
    )(rest, pw, ps)


def _expand_heads(v):
    r = v.shape[0]
    lane = lax.broadcasted_iota(jnp.int32, (r, LANES), 1)
    parts = []
    for k in range(SSD_HEADS // 2):
        l0 = MISC_DT0 + 2 * k
        a = jnp.broadcast_to(v[:, l0:l0 + 1], (r, LANES))
        b = jnp.broadcast_to(v[:, l0 + 1:l0 + 2], (r, LANES))
        parts.append(jnp.where(lane < SSD_HEAD_DIM, a, b))
    return jnp.concatenate(parts, axis=1)


def _ssd_kernel(xbc_ref, z_ref, misc_ref, cw_ref, cb_ref, dtb_ref, a_ref, dsk_ref, nw_ref,
                o_ref, state_ref, ext_ref):
    c = pl.program_id(1)

    @pl.when(c == 0)
    def _():
        state_ref[...] = jnp.zeros_like(state_ref)
        ext_ref[0:CONV_HALO, :] = jnp.zeros((CONV_HALO, D_CONV), F32)

    @pl.when(c > 0)
    def _():
        ext_ref[0:CONV_HALO, :] = ext_ref[CHUNK:CHUNK + CONV_HALO, :]

    ext_ref[CONV_HALO:CONV_HALO + CHUNK, :] = xbc_ref[...]

    conv = cb_ref[...]
    for k in range(CONV_K):
        off = CONV_HALO - (CONV_K - 1) + k
        conv = conv + cw_ref[k:k + 1, :] * ext_ref[off:off + CHUNK, :]
    xc = _silu(conv)
    xs = xc[:, :D_SSD]
    gn = SSD_GROUPS * SSD_STATE
    bm = xc[:, D_SSD:D_SSD + gn]
    cm = xc[:, D_SSD + gn:D_SSD + 2 * gn]

    dt = _softplus(misc_ref[...] + dtb_ref[...])
    a_cs = _cumsum_rows(dt * a_ref[...])
    a_cs_t = a_cs.T
    a_last = a_cs[CHUNK - 1:CHUNK, :]
    x_dt = xs * _expand_heads(dt)
    x_dt_b = x_dt.astype(BF16)
    decay_out = _expand_heads(jnp.exp(a_cs))
    x_state = (x_dt * _expand_heads(jnp.exp(a_last - a_cs))).astype(BF16)
    chunk_decay = _expand_heads(jnp.exp(a_last))

    row = lax.broadcasted_iota(jnp.int32, (CHUNK, CHUNK), 0)
    col = lax.broadcasted_iota(jnp.int32, (CHUNK, CHUNK), 1)
    causal = col <= row
    hpg = SSD_HEADS // SSD_GROUPS
    gw = hpg * SSD_HEAD_DIM
    for g in range(SSD_GROUPS):
        n0, n1 = g * SSD_STATE, (g + 1) * SSD_STATE
        cm_g = cm[:, n0:n1].astype(BF16)
        bm_g = bm[:, n0:n1]
        cb = lax.dot_general(cm_g, bm_g.astype(BF16), (((1,), (1,)), ((), ())),
                             preferred_element_type=F32)
        st = state_ref[:, g * gw:(g + 1) * gw]
        y_off = jnp.dot(cm_g, st.astype(BF16), preferred_element_type=F32)
        y_g = y_off * decay_out[:, g * gw:(g + 1) * gw]
        diag = []
        for r in range(hpg):
            h = g * hpg + r
            lane = MISC_DT0 + h
            seg = a_cs[:, lane:lane + 1] - a_cs_t[lane:lane + 1, :]
            m = (cb * jnp.exp(jnp.where(causal, seg, NEG_BIG))).astype(BF16)
            diag.append(jnp.dot(m, x_dt_b[:, h * SSD_HEAD_DIM:(h + 1) * SSD_HEAD_DIM],
                                preferred_element_type=F32))
        y_g = y_g + jnp.concatenate(diag, axis=1)
        new = jnp.dot(bm_g.T.astype(BF16), x_state[:, g * gw:(g + 1) * gw],
                      preferred_element_type=F32)
        state_ref[:, g * gw:(g + 1) * gw] = chunk_decay[:, g * gw:(g + 1) * gw] * st + new

        sl = slice(g * gw, (g + 1) * gw)
        y_g = y_g + xs[:, sl] * dsk_ref[:, sl]
        gy = y_g * _silu(z_ref[:, sl])
        ms = jnp.mean(gy * gy, axis=-1, keepdims=True)
        o_ref[:, sl] = (gy * lax.rsqrt(ms + RMS_EPS) * nw_ref[:, sl]).astype(BF16)


def _ssd(rest, cw, cb, dtb, a_neg, dsk, nw, nb, nblk):
    t = rest.shape[0]
    vec = lambda n: pl.BlockSpec((1, n), lambda b, c: (0, 0))
    return pl.pallas_call(
        _ssd_kernel,
        grid=(nb, nblk),
        in_specs=[
            pl.BlockSpec((CHUNK, D_CONV), lambda b, c: (b * nblk + c, REST_XBC // D_CONV)),
            pl.BlockSpec((CHUNK, D_SSD), lambda b, c: (b * nblk + c, REST_Z // D_SSD)),
            pl.BlockSpec((CHUNK, LANES), lambda b, c: (b * nblk + c, REST_MISC // LANES)),
            pl.BlockSpec((CONV_K, D_CONV), lambda b, c: (0, 0)),
            vec(D_CONV), vec(LANES), vec(LANES), vec(D_SSD), vec(D_SSD),
        ],
        out_specs=pl.BlockSpec((CHUNK, D_SSD), lambda b, c: (b * nblk + c, 0)),
        out_shape=jax.ShapeDtypeStruct((t, D_SSD), BF16),
        scratch_shapes=[pltpu.VMEM((SSD_STATE, D_SSD), F32),
                        pltpu.VMEM((CONV_HALO + CHUNK, D_CONV), F32)],
        compiler_params=_params("parallel", "arbitrary"),
        name="ssd_mixer",
    )(rest, rest, rest, cw, cb, dtb, a_neg, dsk, nw)


def _out_ln_kernel(ya_ref, yb_ref, yc_ref, h_ref, w_ref, g_ref, b_ref, o_ref):
    tm = o_ref.shape[0]
    for r in range(0, tm, tm // 2):
        rows = slice(r, r + tm // 2)
        acc = jnp.dot(ya_ref[rows, :], w_ref[0:D_ATTN, :], preferred_element_type=F32)
        acc += jnp.dot(yb_ref[rows, :], w_ref[D_ATTN:D_ATTN + D_POOL, :],
                       preferred_element_type=F32)
        acc += jnp.dot(yc_ref[rows, :], w_ref[D_ATTN + D_POOL:, :], preferred_element_type=F32)
        o_ref[rows, :] = _layer_norm(ALPHA * h_ref[rows, :] + acc, g_ref[...], b_ref[...])


def _out_ln(ya, yb, yc, h, w, g, b, layer):
    t, d = h.shape
    tm = _pick_tile(t, (512, 384, 256, 128))
    rows = lambda n: pl.BlockSpec((tm, n), lambda i: (i, 0))
    return pl.pallas_call(
        _out_ln_kernel,
        grid=(t // tm,),
        in_specs=[
            rows(D_ATTN), rows(D_POOL), rows(D_SSD), rows(d),
            pl.BlockSpec((None, d, d), lambda i: (layer, 0, 0)),
            pl.BlockSpec((1, d), lambda i: (0, 0)),
            pl.BlockSpec((1, d), lambda i: (0, 0)),
        ],
        out_specs=rows(d),
        out_shape=jax.ShapeDtypeStruct((t, d), F32),
        compiler_params=_params("parallel"),
        name="out_proj_ln",
    )(ya, yb, yc, h, w, g, b)


def _lane_row(vals, offset):
    return jnp.zeros((1, LANES), F32).at[0, offset:offset + vals.shape[0]].set(vals.astype(F32))


def _proj_weight(w_in):
    c = D_ATTN
    q_k_v = w_in[..., :3 * c]
    f = w_in[..., 3 * c:3 * c + ATTN_HEADS]
    o = 3 * c + ATTN_HEADS
    pool = w_in[..., o:o + D_POOL]
    z = w_in[..., o + D_POOL:o + D_POOL + D_SSD]
    xbc = w_in[..., o + D_POOL + D_SSD:o + D_POOL + D_SSD + D_CONV]
    dt = w_in[..., o + D_POOL + D_SSD + D_CONV:]
    pad = jnp.zeros(w_in.shape[:-1] + (PROJ_TILE - ATTN_HEADS - SSD_HEADS,), w_in.dtype)
    return jnp.concatenate([q_k_v, xbc, pool, z, f, dt, pad], axis=-1).astype(BF16)


def _mixer(h, nb, nblk, layer, w_proj, b_fgate, pool_w, pool_scale, conv_w, conv_b, dt_bias,
           a_log, d_skip, ssd_norm_w, w_out, ln_g, ln_b):
    qkv, rest = _proj(h, w_proj, layer)
    qa, ka, vt = _gate_prep(rest, qkv, _lane_row(b_fgate, MISC_F0), nb, nblk)
    ya = _attention(qa, ka, vt)[:, :nblk * CHUNK].reshape(nb * nblk * CHUNK, D_ATTN)
    yb = _pool(rest, pool_w.astype(BF16), pool_scale.reshape(1, D_POOL), nb, nblk)
    yc = _ssd(rest, conv_w, conv_b.reshape(1, D_CONV), _lane_row(dt_bias, MISC_DT0),
              _lane_row(-jnp.exp(a_log.astype(F32)), MISC_DT0),
              jnp.repeat(d_skip, SSD_HEAD_DIM).reshape(1, D_SSD),
              ssd_norm_w.reshape(1, D_SSD), nb, nblk)
    return _out_ln(ya, yb, yc, h, w_out, ln_g, ln_b, layer)


def kernel(x, meta, f1_gate, f1_up, f1_down, ln1_g, ln1_b, w_in, b_fgate, pool_w, pool_scale,
           conv_w, conv_b, dt_bias, a_log, d_skip, ssd_norm_w, w_out, ln2_g, ln2_b, f2_gate,
           f2_up, f2_down, ln3_g, ln3_b):
    nb, seq, d = x.shape
    assert d == D_MODEL and meta.shape == (N_META, D_MODEL)
    l = N_META + seq
    nblk = pl.cdiv(l, CHUNK)
    lp = nblk * CHUNK
    h = jnp.concatenate([jnp.broadcast_to(meta[None].astype(x.dtype), (nb, N_META, d)), x,
                         jnp.zeros((nb, lp - l, d), x.dtype)], axis=1).reshape(nb * lp, d)
    row = lambda v: v.reshape(1, d)
    bf = lambda w: w.astype(BF16)
    f1 = (bf(f1_gate), bf(f1_up), bf(f1_down))
    f2 = (bf(f2_gate), bf(f2_up), bf(f2_down))
    w_proj, w_o = _proj_weight(w_in), bf(w_out)
    for i in range(f1_gate.shape[0]):
        h = _ffn_ln(h, *f1, row(ln1_g[i]), row(ln1_b[i]), i)
        h = _mixer(h, nb, nblk, i, w_proj, b_fgate[i], pool_w[i], pool_scale[i], conv_w[i],
                   conv_b[i], dt_bias[i], a_log[i], d_skip[i], ssd_norm_w[i], w_o,
                   row(ln2_g[i]), row(ln2_b[i]))
        h = _ffn_ln(h, *f2, row(ln3_g[i]), row(ln3_b[i]), i)
    return h.reshape(nb, lp, d)[:, N_META:l]
```

```python
import functools

import jax
import jax.numpy as jnp
from jax import lax
from jax.experimental import pallas as pl
from jax.experimental.pallas import tpu as pltpu

F32 = jnp.float32
BF16 = jnp.bfloat16

D_MODEL = 2048
N_META = 16
CHUNK = 128
LANES = 128

ATTN_HEADS = 8
ATTN_HEAD_DIM = 64
D_ATTN = ATTN_HEADS * ATTN_HEAD_DIM
POOL_WINDOWS = (2, 4, 8, 16)
POOL_GROUPS = 4
D_POOL = 512
POOL_GROUP_DIM = D_POOL // POOL_GROUPS
POOL_HALO = 16
D_SSD = 1024
SSD_HEAD_DIM = 64
SSD_HEADS = D_SSD // SSD_HEAD_DIM
SSD_GROUPS = 2
SSD_STATE = 128
CONV_K = 4
CONV_HALO = 8
D_CONV = D_SSD + 2 * SSD_GROUPS * SSD_STATE
D_FF_TILE = 512
FFN_ROW_TILES = (512, 384, 256, 128)
DEPTH = 2
ALPHA = (2 * DEPTH) ** 0.25
LN_EPS = 1e-5
RMS_EPS = 1e-5
NEG_BIG = -1e30

PROJ_TILE = 512
REST_XBC, REST_POOL, REST_Z, REST_MISC = 0, D_CONV, D_CONV + D_POOL, D_CONV + D_POOL + D_SSD
D_REST = REST_MISC + PROJ_TILE
MISC_F0 = 0
MISC_DT0 = ATTN_HEADS

VMEM_LIMIT = 56 * 1024 * 1024


def _params(*sem):
    return pltpu.CompilerParams(dimension_semantics=sem, vmem_limit_bytes=VMEM_LIMIT)


def _pick_tile(n, candidates):
    for c in candidates:
        if n % c == 0:
            return c
    raise ValueError(f"no tile in {candidates} divides {n}")


def _layer_norm(y, g, b):
    mu = jnp.mean(y, axis=-1, keepdims=True)
    yc = y - mu
    var = jnp.mean(yc * yc, axis=-1, keepdims=True)
    return yc * lax.rsqrt(var + LN_EPS) * g + b


def _silu(x):
    return x * (1.0 / (1.0 + jnp.exp(-x)))


def _softplus(x):
    return jnp.maximum(x, 0.0) + jnp.log1p(jnp.exp(-jnp.abs(x)))


def _cumsum_rows(x):
    n = x.shape[0]
    row = lax.broadcasted_iota(jnp.int32, x.shape, 0)
    d = 1
    while d < n:
        x = x + jnp.where(row >= d, pltpu.roll(x, d, axis=0), 0.0)
        d *= 2
    return x


def _ffn_ln_kernel(x_ref, wg_ref, wu_ref, wd_ref, g_ref, b_ref, o_ref, xb_ref, acc_ref):
    f = pl.program_id(1)

    @pl.when(f == 0)
    def _():
        xb_ref[...] = x_ref[...].astype(BF16)
        acc_ref[...] = jnp.zeros_like(acc_ref)

    xb = xb_ref[...]
    gate = jnp.dot(xb, wg_ref[...], preferred_element_type=F32)
    up = jnp.dot(xb, wu_ref[...], preferred_element_type=F32)
    act = (_silu(gate) * up).astype(BF16)
    acc_ref[...] += jnp.dot(act, wd_ref[...], preferred_element_type=F32)

    @pl.when(f == pl.num_programs(1) - 1)
    def _():
        y = ALPHA * x_ref[...] + 0.5 * acc_ref[...]
        o_ref[...] = _layer_norm(y, g_ref[...], b_ref[...])


def _ffn_ln(h, wg, wu, wd, g, b, layer):
    t, d = h.shape
    ff = wg.shape[2]
    tm = _pick_tile(t, FFN_ROW_TILES)
    tf = _pick_tile(ff, (D_FF_TILE, 256, 128))
    return pl.pallas_call(
        _ffn_ln_kernel,
        grid=(t // tm, ff // tf),
        in_specs=[
            pl.BlockSpec((tm, d), lambda i, f: (i, 0)),
            pl.BlockSpec((None, d, tf), lambda i, f: (layer, 0, f)),
            pl.BlockSpec((None, d, tf), lambda i, f: (layer, 0, f)),
            pl.BlockSpec((None, tf, d), lambda i, f: (layer, f, 0)),
            pl.BlockSpec((1, d), lambda i, f: (0, 0)),
            pl.BlockSpec((1, d), lambda i, f: (0, 0)),
        ],
        out_specs=pl.BlockSpec((tm, d), lambda i, f: (i, 0)),
        out_shape=jax.ShapeDtypeStruct((t, d), F32),
        scratch_shapes=[pltpu.VMEM((tm, d), BF16), pltpu.VMEM((tm, d), F32)],
        compiler_params=_params("parallel", "arbitrary"),
        name="ffn_ln",
    )(h, wg, wu, wd, g, b)


N_QKV_TILES = 3 * D_ATTN // PROJ_TILE


def _proj_kernel(x_ref, w_ref, qkv_ref, rest_ref):
    xb = x_ref[...].astype(BF16)
    for j in range(w_ref.shape[1] // PROJ_TILE):
        r = jnp.dot(xb, w_ref[:, j * PROJ_TILE:(j + 1) * PROJ_TILE], preferred_element_type=F32)
        if j < N_QKV_TILES:
            qkv_ref[:, j * PROJ_TILE:(j + 1) * PROJ_TILE] = r.astype(BF16)
        else:
            k = j - N_QKV_TILES
            rest_ref[:, k * PROJ_TILE:(k + 1) * PROJ_TILE] = r


def _proj(h, w, layer):
    t, d = h.shape
    n = w.shape[2]
    tm = _pick_tile(t, (512, 384, 256, 128))
    return pl.pallas_call(
        _proj_kernel,
        grid=(t // tm,),
        in_specs=[
            pl.BlockSpec((tm, d), lambda i: (i, 0)),
            pl.BlockSpec((None, d, n), lambda i: (layer, 0, 0), pipeline_mode=pl.Buffered(1)),
        ],
        out_specs=[
            pl.BlockSpec((tm, 3 * D_ATTN), lambda i: (i, 0)),
            pl.BlockSpec((tm, D_REST), lambda i: (i, 0)),
        ],
        out_shape=[
            jax.ShapeDtypeStruct((t, 3 * D_ATTN), BF16),
            jax.ShapeDtypeStruct((t, D_REST), F32),
        ],
        compiler_params=_params("parallel"),
        name="in_proj",
    )(h, w)


LOG2_E = 1.4426950408889634
Q_SCALE = LOG2_E * ATTN_HEAD_DIM ** -0.5
AUG_Q = ATTN_HEAD_DIM
VT_ROWS = ATTN_HEAD_DIM + 16
ATTN_TQ = 2 * CHUNK
ATTN_QK_LEAD = ATTN_HEADS


def _split3(c):
    hi = c.astype(BF16).astype(F32)
    mid = (c - hi).astype(BF16).astype(F32)
    lo = (c - hi - mid).astype(BF16).astype(F32)
    return hi, mid, lo


def _gate_kernel(m_ref, bf_ref, qkv_ref, qa_ref, ka_ref, vt_ref, carry_ref):
    @pl.when(pl.program_id(1) == 0)
    def _():
        carry_ref[...] = jnp.zeros_like(carry_ref)

    x = m_ref[...] + bf_ref[...]
    log_f = jnp.minimum(x, 0.0) - jnp.log1p(jnp.exp(-jnp.abs(x)))
    c = _cumsum_rows(log_f) + carry_ref[...]
    carry_ref[...] = c[CHUNK - 1:CHUNK, :]
    parts = _split3(c * LOG2_E)

    lane = lax.broadcasted_iota(jnp.int32, (CHUNK, LANES), 1)
    ones_row = jnp.where(lax.broadcasted_iota(jnp.int32, (VT_ROWS - ATTN_HEAD_DIM, CHUNK), 0) == 0,
                         1.0, 0.0).astype(BF16)
    for pair in range(ATTN_HEADS // 2):
        q2 = qkv_ref[:, pair * LANES:(pair + 1) * LANES].astype(F32)
        k2 = qkv_ref[:, D_ATTN + pair * LANES:D_ATTN + (pair + 1) * LANES].astype(F32)
        v2t = qkv_ref[:, 2 * D_ATTN + pair * LANES:2 * D_ATTN + (pair + 1) * LANES].astype(F32).T
        for sub in range(2):
            h = 2 * pair + sub
            qh = q2 if sub == 0 else pltpu.roll(q2, ATTN_HEAD_DIM, axis=1)
            kh = k2 if sub == 0 else pltpu.roll(k2, ATTN_HEAD_DIM, axis=1)
            q_extra = jnp.where((lane >= AUG_Q + 3) & (lane < AUG_Q + 6), 1.0, 0.0)
            k_extra = jnp.where((lane >= AUG_Q) & (lane < AUG_Q + 3), 1.0, 0.0)
            for n, part in enumerate(parts):
                col = jnp.broadcast_to(part[:, h:h + 1], (CHUNK, LANES))
                q_extra = jnp.where(lane == AUG_Q + n, col, q_extra)
                k_extra = jnp.where(lane == AUG_Q + 3 + n, -col, k_extra)
            qa_ref[0, h] = jnp.where(lane < AUG_Q, qh, q_extra).astype(BF16)
            ka_ref[0, h] = jnp.where(lane < AUG_Q, kh, k_extra).astype(BF16)
            vt_ref[0, 0, h, 0:ATTN_HEAD_DIM, :] = (
                v2t[sub * ATTN_HEAD_DIM:(sub + 1) * ATTN_HEAD_DIM, :].astype(BF16))
            vt_ref[0, 0, h, ATTN_HEAD_DIM:VT_ROWS, :] = ones_row


def _gate_prep(rest, qkv, bf, nb, nblk):
    misc_blk = REST_MISC // LANES
    aug = ATTN_HEADS * LANES
    nq = pl.cdiv(nblk, 2)
    src = lambda b, j: b * nblk + jnp.minimum(j, nblk - 1)
    return pl.pallas_call(
        _gate_kernel,
        grid=(nb, 2 * nq),
        in_specs=[
            pl.BlockSpec((CHUNK, LANES), lambda b, j: (src(b, j), misc_blk)),
            pl.BlockSpec((1, LANES), lambda b, j: (0, 0)),
            pl.BlockSpec((CHUNK, 3 * D_ATTN), lambda b, j: (src(b, j), 0)),
        ],
        out_specs=[
            pl.BlockSpec((1, ATTN_HEADS, CHUNK, LANES), lambda b, j: (b, 0, j, 0)),
            pl.BlockSpec((1, ATTN_HEADS, CHUNK, LANES), lambda b, j: (b, 0, j, 0)),
            pl.BlockSpec((1, 1, ATTN_HEADS, VT_ROWS, CHUNK), lambda b, j: (b, j // 2, 0, 0, j % 2)),
        ],
        out_shape=[
            jax.ShapeDtypeStruct((nb, ATTN_HEADS, nq * ATTN_TQ, LANES), BF16),
            jax.ShapeDtypeStruct((nb, ATTN_HEADS, nq * ATTN_TQ, LANES), BF16),
            jax.ShapeDtypeStruct((nb, nq, ATTN_HEADS, VT_ROWS, ATTN_TQ), BF16),
        ],
        scratch_shapes=[pltpu.VMEM((1, LANES), F32)],
        compiler_params=_params("parallel", "arbitrary"),
        name="gate_prep",
    )(rest, bf, qkv)


def _attn_kernel(qa_ref, ka_ref, vt_ref, o_ref, acc_ref, st_ref, m_ref):
    i = pl.program_id(1)
    key = lax.broadcasted_iota(jnp.int32, (ATTN_TQ, ATTN_TQ), 0)
    qry = lax.broadcasted_iota(jnp.int32, (ATTN_TQ, ATTN_TQ), 1)
    acc_ref[...] = jnp.zeros_like(acc_ref)
    m_ref[...] = jnp.full(m_ref.shape, NEG_BIG, F32)

    def step(start, nkeys, vt_chunks, mask=None):
        def qk(h):
            ka = ka_ref[0, h, pl.ds(start, nkeys), :]
            qa = qa_ref[0, h]
            st_ref[h, 0:nkeys, :] = lax.dot_general(ka, qa, (((1,), (1,)), ((), ())),
                                                    preferred_element_type=F32)

        def softmax_pv(h):
            st = st_ref[h, 0:nkeys, :]
            if mask is not None:
                st = jnp.where(mask, st, NEG_BIG)
            m_old = m_ref[h]
            m_new = jnp.maximum(m_old, jnp.max(st, axis=0, keepdims=True))
            m_ref[h] = m_new
            p = jnp.exp2(st - m_new).astype(BF16)
            acc = jnp.exp2(m_old - m_new) * acc_ref[h]
            r = 0
            for vt in vt_chunks(h):
                n = vt.shape[1]
                acc += jnp.dot(vt, p[r:r + n, :], preferred_element_type=F32)
                r += n
            acc_ref[h] = acc

        for h in range(ATTN_HEADS + ATTN_QK_LEAD):
            if h < ATTN_HEADS:
                qk(h)
            if h >= ATTN_QK_LEAD:
                softmax_pv(h - ATTN_QK_LEAD)

    def quad_step(s, _):
        step(pl.multiple_of(s * 2 * ATTN_TQ, 2 * ATTN_TQ), 2 * ATTN_TQ,
             lambda h: (vt_ref[0, 2 * s, h], vt_ref[0, 2 * s + 1, h]))
        return 0

    lax.fori_loop(0, i // 2, quad_step, 0)

    @pl.when(i % 2 == 1)
    def _():
        step(pl.multiple_of((i - 1) * ATTN_TQ, ATTN_TQ), ATTN_TQ, lambda h: (vt_ref[0, i - 1, h],))

    step(pl.multiple_of(i * ATTN_TQ, ATTN_TQ), ATTN_TQ, lambda h: (vt_ref[0, i, h],), key <= qry)

    for pair in range(ATTN_HEADS // 2):
        halves = []
        for sub in range(2):
            a = acc_ref[2 * pair + sub]
            halves.append(a[0:ATTN_HEAD_DIM, :] * (1.0 / a[ATTN_HEAD_DIM:ATTN_HEAD_DIM + 1, :]))
        o_ref[0, :, pair * LANES:(pair + 1) * LANES] = (
            jnp.concatenate(halves, axis=0).T.astype(BF16))


def _attention(qa, ka, vt):
    nb, _, lq, _ = qa.shape
    nq = lq // ATTN_TQ
    return pl.pallas_call(
        _attn_kernel,
        grid=(nb, nq),
        in_specs=[
            pl.BlockSpec((1, ATTN_HEADS, ATTN_TQ, LANES), lambda b, i: (b, 0, i, 0)),
            pl.BlockSpec((1, ATTN_HEADS, lq, LANES), lambda b, i: (b, 0, 0, 0)),
            pl.BlockSpec((1, nq, ATTN_HEADS, VT_ROWS, ATTN_TQ), lambda b, i: (b, 0, 0, 0, 0)),
        ],
        out_specs=pl.BlockSpec((1, ATTN_TQ, D_ATTN), lambda b, i: (b, i, 0)),
        out_shape=jax.ShapeDtypeStruct((nb, lq, D_ATTN), BF16),
        scratch_shapes=[pltpu.VMEM((ATTN_HEADS, VT_ROWS, ATTN_TQ), F32),
                        pltpu.VMEM((ATTN_HEADS, 2 * ATTN_TQ, ATTN_TQ), F32),
                        pltpu.VMEM((ATTN_HEADS, 1, ATTN_TQ), F32)],
        compiler_params=_params("parallel", "arbitrary"),
        name="fox_attention",
    )(qa, ka, vt)


def _pool_kernel(u_ref, pw_ref, ps_ref, o_ref, buf_ref, *, tp):
    t = pl.program_id(1)

    @pl.when(t == 0)
    def _():
        buf_ref[0:POOL_HALO, :] = jnp.zeros((POOL_HALO, D_POOL), F32)

    @pl.when(t > 0)
    def _():
        buf_ref[0:POOL_HALO, :] = buf_ref[tp:tp + POOL_HALO, :]

    buf_ref[POOL_HALO:POOL_HALO + tp, :] = u_ref[...]
    pos = t * tp + lax.broadcasted_iota(jnp.int32, (tp, POOL_GROUP_DIM), 0)
    for g, w in enumerate(POOL_WINDOWS):
        lo, hi = g * POOL_GROUP_DIM, (g + 1) * POOL_GROUP_DIM
        u = buf_ref[POOL_HALO:POOL_HALO + tp, lo:hi]
        win = u
        for j in range(1, w):
            win = win + buf_ref[POOL_HALO - j:POOL_HALO - j + tp, lo:hi]
        cnt = jnp.minimum(pos + 1, w).astype(F32)
        diff = (win / cnt - u).astype(BF16)
        mixed = jnp.dot(diff, pw_ref[g], preferred_element_type=F32)
        o_ref[:, lo:hi] = (mixed * ps_ref[:, lo:hi]).astype(BF16)


def _pool(rest, pw, ps, nb, nblk):
    t = rest.shape[0]
    lp = nblk * CHUNK
    tp = _pick_tile(lp, (384, 256, 128))
    nt = lp // tp
    col_blk = REST_POOL // D_POOL
    return pl.pallas_call(
        functools.partial(_pool_kernel, tp=tp),
        grid=(nb, nt),
        in_specs=[
            pl.BlockSpec((tp, D_POOL), lambda b, i: (b * nt + i, col_blk)),
            pl.BlockSpec((POOL_GROUPS, POOL_GROUP_DIM, POOL_GROUP_DIM), lambda b, i: (0, 0, 0)),
            pl.BlockSpec((1, D_POOL), lambda b, i: (0, 0)),
        ],
        out_specs=pl.BlockSpec((tp, D_POOL), lambda b, i: (b * nt + i, 0)),
        out_shape=jax.ShapeDtypeStruct((t, D_POOL), BF16),
        scratch_shapes=[pltpu.VMEM((POOL_HALO + tp, D_POOL), F32)],
        compiler_params=_params("parallel", "arbitrary"),
        name="pool_mixer",
    )(rest, pw, ps)


def _expand_heads(v):
    r = v.shape[0]
    lane = lax.broadcasted_iota(jnp.int32, (r, LANES), 1)
    parts = []
    for k in range(SSD_HEADS // 2):
        l0 = MISC_DT0 + 2 * k
        a = jnp.broadcast_to(v[:, l0:l0 + 1], (r, LANES))
        b = jnp.broadcast_to(v[:, l0 + 1:l0 + 2], (r, LANES))
        parts.append(jnp.where(lane < SSD_HEAD_DIM, a, b))
    return jnp.concatenate(parts, axis=1)


def _ssd_kernel(xbc_ref, z_ref, misc_ref, cw_ref, cb_ref, dtb_ref, a_ref, dsk_ref, nw_ref,
                o_ref, state_ref, ext_ref):
    c = pl.program_id(1)

    @pl.when(c == 0)
    def _():
        state_ref[...] = jnp.zeros_like(state_ref)
        ext_ref[0:CONV_HALO, :] = jnp.zeros((CONV_HALO, D_CONV), F32)

    @pl.when(c > 0)
    def _():
        ext_ref[0:CONV_HALO, :] = ext_ref[CHUNK:CHUNK + CONV_HALO, :]

    ext_ref[CONV_HALO:CONV_HALO + CHUNK, :] = xbc_ref[...]

    conv = cb_ref[...]
    for k in range(CONV_K):
        off = CONV_HALO - (CONV_K - 1) + k
        conv = conv + cw_ref[k:k + 1, :] * ext_ref[off:off + CHUNK, :]
    xc = _silu(conv)
    xs = xc[:, :D_SSD]
    gn = SSD_GROUPS * SSD_STATE
    bm = xc[:, D_SSD:D_SSD + gn]
    cm = xc[:, D_SSD + gn:D_SSD + 2 * gn]

    dt = _softplus(misc_ref[...] + dtb_ref[...])
    a_cs = _cumsum_rows(dt * a_ref[...])
    a_cs_t = a_cs.T
    a_last = a_cs[CHUNK - 1:CHUNK, :]
    x_dt = xs * _expand_heads(dt)
    x_dt_b = x_dt.astype(BF16)
    decay_out = _expand_heads(jnp.exp(a_cs))
    x_state = (x_dt * _expand_heads(jnp.exp(a_last - a_cs))).astype(BF16)
    chunk_decay = _expand_heads(jnp.exp(a_last))

    row = lax.broadcasted_iota(jnp.int32, (CHUNK, CHUNK), 0)
    col = lax.broadcasted_iota(jnp.int32, (CHUNK, CHUNK), 1)
    causal = col <= row
    hpg = SSD_HEADS // SSD_GROUPS
    gw = hpg * SSD_HEAD_DIM
    for g in range(SSD_GROUPS):
        n0, n1 = g * SSD_STATE, (g + 1) * SSD_STATE
        cm_g = cm[:, n0:n1].astype(BF16)
        bm_g = bm[:, n0:n1]
        cb = lax.dot_general(cm_g, bm_g.astype(BF16), (((1,), (1,)), ((), ())),
                             preferred_element_type=F32)
        st = state_ref[:, g * gw:(g + 1) * gw]
        y_off = jnp.dot(cm_g, st.astype(BF16), preferred_element_type=F32)
        y_g = y_off * decay_out[:, g * gw:(g + 1) * gw]
        diag = []
        for r in range(hpg):
            h = g * hpg + r
            lane = MISC_DT0 + h
            seg = a_cs[:, lane:lane + 1] - a_cs_t[lane:lane + 1, :]
            m = (cb * jnp.exp(jnp.where(causal, seg, NEG_BIG))).astype(BF16)
            diag.append(jnp.dot(m, x_dt_b[:, h * SSD_HEAD_DIM:(h + 1) * SSD_HEAD_DIM],
                                preferred_element_type=F32))
        y_g = y_g + jnp.concatenate(diag, axis=1)
        new = jnp.dot(bm_g.T.astype(BF16), x_state[:, g * gw:(g + 1) * gw],
                      preferred_element_type=F32)
        state_ref[:, g * gw:(g + 1) * gw] = chunk_decay[:, g * gw:(g + 1) * gw] * st + new

        sl = slice(g * gw, (g + 1) * gw)
        y_g = y_g + xs[:, sl] * dsk_ref[:, sl]
        gy = y_g * _silu(z_ref[:, sl])
        ms = jnp.mean(gy * gy, axis=-1, keepdims=True)
        o_ref[:, sl] = (gy * lax.rsqrt(ms + RMS_EPS) * nw_ref[:, sl]).astype(BF16)


def _ssd(rest, cw, cb, dtb, a_neg, dsk, nw, nb, nblk):
    t = rest.shape[0]
    vec = lambda n: pl.BlockSpec((1, n), lambda b, c: (0, 0))
    return pl.pallas_call(
        _ssd_kernel,
        grid=(nb, nblk),
        in_specs=[
            pl.BlockSpec((CHUNK, D_CONV), lambda b, c: (b * nblk + c, REST_XBC // D_CONV)),
            pl.BlockSpec((CHUNK, D_SSD), lambda b, c: (b * nblk + c, REST_Z // D_SSD)),
            pl.BlockSpec((CHUNK, LANES), lambda b, c: (b * nblk + c, REST_MISC // LANES)),
            pl.BlockSpec((CONV_K, D_CONV), lambda b, c: (0, 0)),
            vec(D_CONV), vec(LANES), vec(LANES), vec(D_SSD), vec(D_SSD),
        ],
        out_specs=pl.BlockSpec((CHUNK, D_SSD), lambda b, c: (b * nblk + c, 0)),
        out_shape=jax.ShapeDtypeStruct((t, D_SSD), BF16),
        scratch_shapes=[pltpu.VMEM((SSD_STATE, D_SSD), F32),
                        pltpu.VMEM((CONV_HALO + CHUNK, D_CONV), F32)],
        compiler_params=_params("parallel", "arbitrary"),
        name="ssd_mixer",
    )(rest, rest, rest, cw, cb, dtb, a_neg, dsk, nw)


def _out_ln_kernel(ya_ref, yb_ref, yc_ref, h_ref, w_ref, g_ref, b_ref, o_ref):
    tm = o_ref.shape[0]
    for r in range(0, tm, tm // 2):
        rows = slice(r, r + tm // 2)
        acc = jnp.dot(ya_ref[rows, :], w_ref[0:D_ATTN, :], preferred_element_type=F32)
        acc += jnp.dot(yb_ref[rows, :], w_ref[D_ATTN:D_ATTN + D_POOL, :],
                       preferred_element_type=F32)
        acc += jnp.dot(yc_ref[rows, :], w_ref[D_ATTN + D_POOL:, :], preferred_element_type=F32)
        o_ref[rows, :] = _layer_norm(ALPHA * h_ref[rows, :] + acc, g_ref[...], b_ref[...])


def _out_ln(ya, yb, yc, h, w, g, b, layer):
    t, d = h.shape
    tm = _pick_tile(t, (512, 384, 256, 128))
    rows = lambda n: pl.BlockSpec((tm, n), lambda i: (i, 0))
    return pl.pallas_call(
        _out_ln_kernel,
        grid=(t // tm,),
        in_specs=[
            rows(D_ATTN), rows(D_POOL), rows(D_SSD), rows(d),
            pl.BlockSpec((None, d, d), lambda i: (layer, 0, 0)),
            pl.BlockSpec((1, d), lambda i: (0, 0)),
            pl.BlockSpec((1, d), lambda i: (0, 0)),
        ],
        out_specs=rows(d),
        out_shape=jax.ShapeDtypeStruct((t, d), F32),
        compiler_params=_params("parallel"),
        name="out_proj_ln",
    )(ya, yb, yc, h, w, g, b)


def _lane_row(vals, offset):
    return jnp.zeros((1, LANES), F32).at[0, offset:offset + vals.shape[0]].set(vals.astype(F32))


def _proj_weight(w_in):
    c = D_ATTN
    q_k_v = jnp.concatenate([w_in[..., :c] * Q_SCALE, w_in[..., c:3 * c]], axis=-1)
    f = w_in[..., 3 * c:3 * c + ATTN_HEADS]
    o = 3 * c + ATTN_HEADS
    pool = w_in[..., o:o + D_POOL]
    z = w_in[..., o + D_POOL:o + D_POOL + D_SSD]
    xbc = w_in[..., o + D_POOL + D_SSD:o + D_POOL + D_SSD + D_CONV]
    dt = w_in[..., o + D_POOL + D_SSD + D_CONV:]
    pad = jnp.zeros(w_in.shape[:-1] + (PROJ_TILE - ATTN_HEADS - SSD_HEADS,), w_in.dtype)
    return jnp.concatenate([q_k_v, xbc, pool, z, f, dt, pad], axis=-1).astype(BF16)


def _mixer(h, nb, nblk, layer, w_proj, b_fgate, pool_w, pool_scale, conv_w, conv_b, dt_bias,
           a_log, d_skip, ssd_norm_w, w_out, ln_g, ln_b):
    qkv, rest = _proj(h, w_proj, layer)
    qa, ka, vt = _gate_prep(rest, qkv, _lane_row(b_fgate, MISC_F0), nb, nblk)
    ya = _attention(qa, ka, vt)[:, :nblk * CHUNK].reshape(nb * nblk * CHUNK, D_ATTN)
    yb = _pool(rest, pool_w.astype(BF16), pool_scale.reshape(1, D_POOL), nb, nblk)
    yc = _ssd(rest, conv_w, conv_b.reshape(1, D_CONV), _lane_row(dt_bias, MISC_DT0),
              _lane_row(-jnp.exp(a_log.astype(F32)), MISC_DT0),
              jnp.repeat(d_skip, SSD_HEAD_DIM).reshape(1, D_SSD),
              ssd_norm_w.reshape(1, D_SSD), nb, nblk)
    return _out_ln(ya, yb, yc, h, w_out, ln_g, ln_b, layer)


def kernel(x, meta, f1_gate, f1_up, f1_down, ln1_g, ln1_b, w_in, b_fgate, pool_w, pool_scale,
           conv_w, conv_b, dt_bias, a_log, d_skip, ssd_norm_w, w_out, ln2_g, ln2_b, f2_gate,
           f2_up, f2_down, ln3_g, ln3_b):
    nb, seq, d = x.shape
    assert d == D_MODEL and meta.shape == (N_META, D_MODEL)
    l = N_META + seq
    nblk = pl.cdiv(l, CHUNK)
    lp = nblk * CHUNK
    h = jnp.concatenate([jnp.broadcast_to(meta[None].astype(x.dtype), (nb, N_META, d)), x,
                         jnp.zeros((nb, lp - l, d), x.dtype)], axis=1).reshape(nb * lp, d)
    row = lambda v: v.reshape(1, d)
    bf = lambda w: w.astype(BF16)
    f1 = (bf(f1_gate), bf(f1_up), bf(f1_down))
    f2 = (bf(f2_gate), bf(f2_up), bf(f2_down))
    w_proj, w_o = _proj_weight(w_in), bf(w_out)
    for i in range(f1_gate.shape[0]):
        h = _ffn_ln(h, *f1, row(ln1_g[i]), row(ln1_b[i]), i)
        h = _mixer(h, nb, nblk, i, w_proj, b_fgate[i], pool_w[i], pool_scale[i], conv_w[i],
                   conv_b[i], dt_bias[i], a_log[i], d_skip[i], ssd_norm_w[i], w_o,
                   row(ln2_g[i]), row(ln2_b[i]))
        h = _ffn_ln(h, *f2, row(ln3_g[i]), row(ln3_b[i]), i)
    return h.reshape(nb, lp, d)[:, N_META:l]
```

```python
import functools

import jax
import jax.numpy as jnp
from jax import lax
from jax.experimental import pallas as pl
from jax.experimental.pallas import tpu as pltpu

F32 = jnp.float32
BF16 = jnp.bfloat16

D_MODEL = 2048
N_META = 16
CHUNK = 128
PAD = CHUNK - N_META
LANES = 128

ATTN_HEADS = 8
ATTN_HEAD_DIM = 64
D_ATTN = ATTN_HEADS * ATTN_HEAD_DIM
POOL_WINDOWS = (2, 4, 8, 16)
POOL_GROUPS = 4
D_POOL = 512
POOL_GROUP_DIM = D_POOL // POOL_GROUPS
POOL_HALO = 16
D_SSD = 1024
SSD_HEAD_DIM = 64
SSD_HEADS = D_SSD // SSD_HEAD_DIM
SSD_GROUPS = 2
SSD_STATE = 128
CONV_K = 4
CONV_HALO = 8
D_CONV = D_SSD + 2 * SSD_GROUPS * SSD_STATE
D_FF_TILE = 512
ROW_TILES = (512, 384, 256, 128)
DEPTH = 2
ALPHA = (2 * DEPTH) ** 0.25
LN_EPS = 1e-5
RMS_EPS = 1e-5
NEG_BIG = -1e30

PROJ_TILE = 512
REST_XBC, REST_POOL, REST_Z, REST_MISC = 0, D_CONV, D_CONV + D_POOL, D_CONV + D_POOL + D_SSD
D_REST = REST_MISC + PROJ_TILE
MISC_F0 = 0
MISC_DT0 = ATTN_HEADS

VMEM_LIMIT = 56 * 1024 * 1024


def _params(*sem):
    return pltpu.CompilerParams(dimension_semantics=sem, vmem_limit_bytes=VMEM_LIMIT)


def _pick_tile(candidates, *sizes):
    for c in candidates:
        if all(n % c == 0 for n in sizes):
            return c
    raise ValueError(f"no tile in {candidates} divides {sizes}")


def _phys_block(b, j, nb, nx):
    return jnp.where(j == 0, nb * nx + b, b * nx + j - 1)


def _layer_norm(y, g, b):
    mu = jnp.mean(y, axis=-1, keepdims=True)
    yc = y - mu
    var = jnp.mean(yc * yc, axis=-1, keepdims=True)
    return yc * lax.rsqrt(var + LN_EPS) * g + b


def _silu(x):
    return x * (1.0 / (1.0 + jnp.exp(-x)))


def _softplus(x):
    return jnp.maximum(x, 0.0) + jnp.log1p(jnp.exp(-jnp.abs(x)))


def _cumsum_rows(x):
    n = x.shape[0]
    row = lax.broadcasted_iota(jnp.int32, x.shape, 0)
    d = 1
    while d < n:
        x = x + jnp.where(row >= d, pltpu.roll(x, d, axis=0), 0.0)
        d *= 2
    return x


def _ffn_ln_kernel(*refs, n_main_tiles):
    if n_main_tiles is None:
        x_ref, wg_ref, wu_ref, wd_ref, g_ref, b_ref, o_ref, xb_ref, acc_ref = refs
        read_x = lambda: x_ref[...]
    else:
        x_ref, tail_ref, wg_ref, wu_ref, wd_ref, g_ref, b_ref, o_ref, xb_ref, acc_ref = refs
        in_main = pl.program_id(0) < n_main_tiles
        read_x = lambda: jnp.where(in_main, x_ref[...], tail_ref[...])
    f = pl.program_id(1)

    @pl.when(f == 0)
    def _():
        xb_ref[...] = read_x().astype(BF16)
        acc_ref[...] = jnp.zeros_like(acc_ref)

    xb = xb_ref[...]
    gate = jnp.dot(xb, wg_ref[...], preferred_element_type=F32)
    up = jnp.dot(xb, wu_ref[...], preferred_element_type=F32)
    act = (_silu(gate) * up).astype(BF16)
    acc_ref[...] += jnp.dot(act, wd_ref[...], preferred_element_type=F32)

    @pl.when(f == pl.num_programs(1) - 1)
    def _():
        y = ALPHA * read_x() + 0.5 * acc_ref[...]
        o_ref[...] = _layer_norm(y, g_ref[...], b_ref[...])


def _ffn_ln(h, wg, wu, wd, g, b, layer, *, tail=None, rows=None):
    d = h.shape[1]
    ff = wg.shape[2]
    t = h.shape[0] + (0 if tail is None else tail.shape[0])
    rows = t if rows is None else rows
    tm = _pick_tile(ROW_TILES, rows, *((h.shape[0], tail.shape[0]) if tail is not None else ()))
    tf = _pick_tile((D_FF_TILE, 256, 128), ff)
    x_specs = [pl.BlockSpec((tm, d), lambda i, f: (i, 0))]
    operands = [h]
    n_main = None
    if tail is not None:
        n_main = h.shape[0] // tm
        x_specs = [pl.BlockSpec((tm, d), lambda i, f: (jnp.minimum(i, n_main - 1), 0)),
                   pl.BlockSpec((tm, d), lambda i, f: (jnp.maximum(i - n_main, 0), 0))]
        operands = [h, tail]
    return pl.pallas_call(
        functools.partial(_ffn_ln_kernel, n_main_tiles=n_main),
        grid=(rows // tm, ff // tf),
        in_specs=x_specs + [
            pl.BlockSpec((None, d, tf), lambda i, f: (layer, 0, f)),
            pl.BlockSpec((None, d, tf), lambda i, f: (layer, 0, f)),
            pl.BlockSpec((None, tf, d), lambda i, f: (layer, f, 0)),
            pl.BlockSpec((1, d), lambda i, f: (0, 0)),
            pl.BlockSpec((1, d), lambda i, f: (0, 0)),
        ],
        out_specs=pl.BlockSpec((tm, d), lambda i, f: (i, 0)),
        out_shape=jax.ShapeDtypeStruct((rows, d), F32),
        scratch_shapes=[pltpu.VMEM((tm, d), BF16), pltpu.VMEM((tm, d), F32)],
        compiler_params=_params("parallel", "arbitrary"),
        name="ffn_ln",
    )(*operands, wg, wu, wd, g, b)


N_QKV_TILES = 3 * D_ATTN // PROJ_TILE


def _proj_kernel(x_ref, w_ref, qkv_ref, rest_ref):
    xb = x_ref[...].astype(BF16)
    for j in range(w_ref.shape[1] // PROJ_TILE):
        r = jnp.dot(xb, w_ref[:, j * PROJ_TILE:(j + 1) * PROJ_TILE], preferred_element_type=F32)
        if j < N_QKV_TILES:
            qkv_ref[:, j * PROJ_TILE:(j + 1) * PROJ_TILE] = r.astype(BF16)
        else:
            k = j - N_QKV_TILES
            rest_ref[:, k * PROJ_TILE:(k + 1) * PROJ_TILE] = r


def _proj(h, w, layer):
    t, d = h.shape
    n = w.shape[2]
    tm = _pick_tile(ROW_TILES, t)
    return pl.pallas_call(
        _proj_kernel,
        grid=(t // tm,),
        in_specs=[
            pl.BlockSpec((tm, d), lambda i: (i, 0)),
            pl.BlockSpec((None, d, n), lambda i: (layer, 0, 0), pipeline_mode=pl.Buffered(1)),
        ],
        out_specs=[
            pl.BlockSpec((tm, 3 * D_ATTN), lambda i: (i, 0)),
            pl.BlockSpec((tm, D_REST), lambda i: (i, 0)),
        ],
        out_shape=[
            jax.ShapeDtypeStruct((t, 3 * D_ATTN), BF16),
            jax.ShapeDtypeStruct((t, D_REST), F32),
        ],
        compiler_params=_params("parallel"),
        name="in_proj",
    )(h, w)


LOG2_E = 1.4426950408889634
Q_SCALE = LOG2_E * ATTN_HEAD_DIM ** -0.5
AUG_Q = ATTN_HEAD_DIM
VT_ROWS = ATTN_HEAD_DIM + 16
ATTN_TQ = 2 * CHUNK
ATTN_QK_LEAD = ATTN_HEADS


def _split3(c):
    hi = c.astype(BF16).astype(F32)
    mid = (c - hi).astype(BF16).astype(F32)
    lo = (c - hi - mid).astype(BF16).astype(F32)
    return hi, mid, lo


def _gate_kernel(m_ref, bf_ref, qkv_ref, qa_ref, ka_ref, vt_ref, carry_ref):
    first = pl.program_id(1) == 0

    @pl.when(first)
    def _():
        carry_ref[...] = jnp.zeros_like(carry_ref)

    row = lax.broadcasted_iota(jnp.int32, (CHUNK, LANES), 0)
    lane = lax.broadcasted_iota(jnp.int32, (CHUNK, LANES), 1)
    is_pad = row < jnp.where(first, PAD, 0)

    x = m_ref[...] + bf_ref[...]
    log_f = jnp.minimum(x, 0.0) - jnp.log1p(jnp.exp(-jnp.abs(x)))
    c = _cumsum_rows(jnp.where(is_pad, 0.0, log_f)) + carry_ref[...]
    carry_ref[...] = c[CHUNK - 1:CHUNK, :]
    parts = _split3(c * LOG2_E)

    ones_row = jnp.where(lax.broadcasted_iota(jnp.int32, (VT_ROWS - ATTN_HEAD_DIM, CHUNK), 0) == 0,
                         1.0, 0.0).astype(BF16)
    k_pad = jnp.where(lane == AUG_Q + 3, NEG_BIG, 0.0)
    for pair in range(ATTN_HEADS // 2):
        q2 = qkv_ref[:, pair * LANES:(pair + 1) * LANES].astype(F32)
        k2 = qkv_ref[:, D_ATTN + pair * LANES:D_ATTN + (pair + 1) * LANES].astype(F32)
        v2t = qkv_ref[:, 2 * D_ATTN + pair * LANES:2 * D_ATTN + (pair + 1) * LANES].astype(F32).T
        for sub in range(2):
            h = 2 * pair + sub
            qh = q2 if sub == 0 else pltpu.roll(q2, ATTN_HEAD_DIM, axis=1)
            kh = k2 if sub == 0 else pltpu.roll(k2, ATTN_HEAD_DIM, axis=1)
            q_extra = jnp.where((lane >= AUG_Q + 3) & (lane < AUG_Q + 6), 1.0, 0.0)
            k_extra = jnp.where((lane >= AUG_Q) & (lane < AUG_Q + 3), 1.0, 0.0)
            for n, part in enumerate(parts):
                col = jnp.broadcast_to(part[:, h:h + 1], (CHUNK, LANES))
                q_extra = jnp.where(lane == AUG_Q + n, col, q_extra)
                k_extra = jnp.where(lane == AUG_Q + 3 + n, -col, k_extra)
            k_aug = jnp.where(is_pad, k_pad, jnp.where(lane < AUG_Q, kh, k_extra))
            qa_ref[0, h] = jnp.where(lane < AUG_Q, qh, q_extra).astype(BF16)
            ka_ref[0, h] = k_aug.astype(BF16)
            vt_ref[0, 0, h, 0:ATTN_HEAD_DIM, :] = (
                v2t[sub * ATTN_HEAD_DIM:(sub + 1) * ATTN_HEAD_DIM, :].astype(BF16))
            vt_ref[0, 0, h, ATTN_HEAD_DIM:VT_ROWS, :] = ones_row


def _gate_prep(rest, qkv, bf, nb, nx):
    nblk = nx + 1
    misc_blk = REST_MISC // LANES
    nq = pl.cdiv(nblk, 2)
    src = lambda b, j: _phys_block(b, jnp.minimum(j, nblk - 1), nb, nx)
    return pl.pallas_call(
        _gate_kernel,
        grid=(nb, 2 * nq),
        in_specs=[
            pl.BlockSpec((CHUNK, LANES), lambda b, j: (src(b, j), misc_blk)),
            pl.BlockSpec((1, LANES), lambda b, j: (0, 0)),
            pl.BlockSpec((CHUNK, 3 * D_ATTN), lambda b, j: (src(b, j), 0)),
        ],
        out_specs=[
            pl.BlockSpec((1, ATTN_HEADS, CHUNK, LANES), lambda b, j: (b, 0, j, 0)),
            pl.BlockSpec((1, ATTN_HEADS, CHUNK, LANES), lambda b, j: (b, 0, j, 0)),
            pl.BlockSpec((1, 1, ATTN_HEADS, VT_ROWS, CHUNK), lambda b, j: (b, j // 2, 0, 0, j % 2)),
        ],
        out_shape=[
            jax.ShapeDtypeStruct((nb, ATTN_HEADS, nq * ATTN_TQ, LANES), BF16),
            jax.ShapeDtypeStruct((nb, ATTN_HEADS, nq * ATTN_TQ, LANES), BF16),
            jax.ShapeDtypeStruct((nb, nq, ATTN_HEADS, VT_ROWS, ATTN_TQ), BF16),
        ],
        scratch_shapes=[pltpu.VMEM((1, LANES), F32)],
        compiler_params=_params("parallel", "arbitrary"),
        name="gate_prep",
    )(rest, bf, qkv)


def _attn_kernel(qa_ref, ka_ref, vt_ref, o_ref, acc_ref, st_ref, m_ref):
    i = pl.program_id(1)
    key = lax.broadcasted_iota(jnp.int32, (ATTN_TQ, ATTN_TQ), 0)
    qry = lax.broadcasted_iota(jnp.int32, (ATTN_TQ, ATTN_TQ), 1)
    acc_ref[...] = jnp.zeros_like(acc_ref)
    m_ref[...] = jnp.full(m_ref.shape, NEG_BIG, F32)

    def step(start, nkeys, vt_chunks, mask=None):
        def qk(h):
            ka = ka_ref[0, h, pl.ds(start, nkeys), :]
            qa = qa_ref[0, h]
            st_ref[h, 0:nkeys, :] = lax.dot_general(ka, qa, (((1,), (1,)), ((), ())),
                                                    preferred_element_type=F32)

        def softmax_pv(h):
            st = st_ref[h, 0:nkeys, :]
            if mask is not None:
                st = jnp.where(mask, st, NEG_BIG)
            m_old = m_ref[h]
            m_new = jnp.maximum(m_old, jnp.max(st, axis=0, keepdims=True))
            m_ref[h] = m_new
            p = jnp.exp2(st - m_new).astype(BF16)
            acc = jnp.exp2(m_old - m_new) * acc_ref[h]
            r = 0
            for vt in vt_chunks(h):
                n = vt.shape[1]
                acc += jnp.dot(vt, p[r:r + n, :], preferred_element_type=F32)
                r += n
            acc_ref[h] = acc

        for h in range(ATTN_HEADS + ATTN_QK_LEAD):
            if h < ATTN_HEADS:
                qk(h)
            if h >= ATTN_QK_LEAD:
                softmax_pv(h - ATTN_QK_LEAD)

    def quad_step(s, _):
        step(pl.multiple_of(s * 2 * ATTN_TQ, 2 * ATTN_TQ), 2 * ATTN_TQ,
             lambda h: (vt_ref[0, 2 * s, h], vt_ref[0, 2 * s + 1, h]))
        return 0

    lax.fori_loop(0, i // 2, quad_step, 0)

    @pl.when(i % 2 == 1)
    def _():
        step(pl.multiple_of((i - 1) * ATTN_TQ, ATTN_TQ), ATTN_TQ, lambda h: (vt_ref[0, i - 1, h],))

    step(pl.multiple_of(i * ATTN_TQ, ATTN_TQ), ATTN_TQ, lambda h: (vt_ref[0, i, h],), key <= qry)

    for pair in range(ATTN_HEADS // 2):
        halves = []
        for sub in range(2):
            a = acc_ref[2 * pair + sub]
            halves.append(a[0:ATTN_HEAD_DIM, :] * (1.0 / a[ATTN_HEAD_DIM:ATTN_HEAD_DIM + 1, :]))
        o_ref[0, :, pair * LANES:(pair + 1) * LANES] = (
            jnp.concatenate(halves, axis=0).T.astype(BF16))


def _attention(qa, ka, vt):
    nb, _, lq, _ = qa.shape
    nq = lq // ATTN_TQ
    return pl.pallas_call(
        _attn_kernel,
        grid=(nb, nq),
        in_specs=[
            pl.BlockSpec((1, ATTN_HEADS, ATTN_TQ, LANES), lambda b, i: (b, 0, i, 0)),
            pl.BlockSpec((1, ATTN_HEADS, lq, LANES), lambda b, i: (b, 0, 0, 0)),
            pl.BlockSpec((1, nq, ATTN_HEADS, VT_ROWS, ATTN_TQ), lambda b, i: (b, 0, 0, 0, 0)),
        ],
        out_specs=pl.BlockSpec((1, ATTN_TQ, D_ATTN), lambda b, i: (b, i, 0)),
        out_shape=jax.ShapeDtypeStruct((nb, lq, D_ATTN), BF16),
        scratch_shapes=[pltpu.VMEM((ATTN_HEADS, VT_ROWS, ATTN_TQ), F32),
                        pltpu.VMEM((ATTN_HEADS, 2 * ATTN_TQ, ATTN_TQ), F32),
                        pltpu.VMEM((ATTN_HEADS, 1, ATTN_TQ), F32)],
        compiler_params=_params("parallel", "arbitrary"),
        name="fox_attention",
    )(qa, ka, vt)


def _expand_heads(v):
    r = v.shape[0]
    lane = lax.broadcasted_iota(jnp.int32, (r, LANES), 1)
    parts = []
    for k in range(SSD_HEADS // 2):
        l0 = MISC_DT0 + 2 * k
        a = jnp.broadcast_to(v[:, l0:l0 + 1], (r, LANES))
        b = jnp.broadcast_to(v[:, l0 + 1:l0 + 2], (r, LANES))
        parts.append(jnp.where(lane < SSD_HEAD_DIM, a, b))
    return jnp.concatenate(parts, axis=1)


def _pool_chunk(c, u_ref, pw_ref, ps_ref, yb_ref, buf_ref):
    @pl.when(c == 0)
    def _():
        buf_ref[0:POOL_HALO, :] = jnp.zeros((POOL_HALO, D_POOL), F32)

    @pl.when(c > 0)
    def _():
        buf_ref[0:POOL_HALO, :] = buf_ref[CHUNK:CHUNK + POOL_HALO, :]

    buf_ref[POOL_HALO:POOL_HALO + CHUNK, :] = u_ref[...]

    @pl.when(c == 0)
    def _():
        buf_ref[POOL_HALO:POOL_HALO + PAD, :] = jnp.zeros((PAD, D_POOL), F32)

    seen = c * CHUNK - PAD + 1 + lax.broadcasted_iota(jnp.int32, (CHUNK, POOL_GROUP_DIM), 0)
    for g, w in enumerate(POOL_WINDOWS):
        lo, hi = g * POOL_GROUP_DIM, (g + 1) * POOL_GROUP_DIM
        u = buf_ref[POOL_HALO:POOL_HALO + CHUNK, lo:hi]
        win = u
        for j in range(1, w):
            win = win + buf_ref[POOL_HALO - j:POOL_HALO - j + CHUNK, lo:hi]
        cnt = jnp.clip(seen, 1, w).astype(F32)
        diff = (win / cnt - u).astype(BF16)
        mixed = jnp.dot(diff, pw_ref[g], preferred_element_type=F32)
        yb_ref[:, lo:hi] = (mixed * ps_ref[:, lo:hi]).astype(BF16)


def _seq_mix_kernel(xbc_ref, z_ref, misc_ref, u_ref, cw_ref, cb_ref, dtb_ref, a_ref, dsk_ref,
                    nw_ref, pw_ref, ps_ref, yc_ref, yb_ref, state_ref, ext_ref, buf_ref):
    c = pl.program_id(1)
    _pool_chunk(c, u_ref, pw_ref, ps_ref, yb_ref, buf_ref)

    @pl.when(c == 0)
    def _():
        state_ref[...] = jnp.zeros_like(state_ref)
        ext_ref[0:CONV_HALO, :] = jnp.zeros((CONV_HALO, D_CONV), F32)

    @pl.when(c > 0)
    def _():
        ext_ref[0:CONV_HALO, :] = ext_ref[CHUNK:CHUNK + CONV_HALO, :]

    ext_ref[CONV_HALO:CONV_HALO + CHUNK, :] = xbc_ref[...]

    @pl.when(c == 0)
    def _():
        ext_ref[CONV_HALO:CONV_HALO + PAD, :] = jnp.zeros((PAD, D_CONV), F32)

    conv = cb_ref[...]
    for k in range(CONV_K):
        off = CONV_HALO - (CONV_K - 1) + k
        conv = conv + cw_ref[k:k + 1, :] * ext_ref[off:off + CHUNK, :]
    xc = _silu(conv)
    xs = xc[:, :D_SSD]
    gn = SSD_GROUPS * SSD_STATE
    bm = xc[:, D_SSD:D_SSD + gn]
    cm = xc[:, D_SSD + gn:D_SSD + 2 * gn]

    row = lax.broadcasted_iota(jnp.int32, (CHUNK, CHUNK), 0)
    col = lax.broadcasted_iota(jnp.int32, (CHUNK, CHUNK), 1)
    dt = _softplus(misc_ref[...] + dtb_ref[...])
    dt = jnp.where(row >= jnp.where(c == 0, PAD, 0), dt, 0.0)
    a_cs = _cumsum_rows(dt * a_ref[...])
    a_cs_t = a_cs.T
    a_last = a_cs[CHUNK - 1:CHUNK, :]
    x_dt = xs * _expand_heads(dt)
    x_dt_b = x_dt.astype(BF16)
    decay_out = _expand_heads(jnp.exp(a_cs))
    x_state = (x_dt * _expand_heads(jnp.exp(a_last - a_cs))).astype(BF16)
    chunk_decay = _expand_heads(jnp.exp(a_last))

    causal = col <= row
    hpg = SSD_HEADS // SSD_GROUPS
    gw = hpg * SSD_HEAD_DIM
    for g in range(SSD_GROUPS):
        n0, n1 = g * SSD_STATE, (g + 1) * SSD_STATE
        cm_g = cm[:, n0:n1].astype(BF16)
        bm_g = bm[:, n0:n1]
        cb = lax.dot_general(cm_g, bm_g.astype(BF16), (((1,), (1,)), ((), ())),
                             preferred_element_type=F32)
        st = state_ref[:, g * gw:(g + 1) * gw]
        y_off = jnp.dot(cm_g, st.astype(BF16), preferred_element_type=F32)
        y_g = y_off * decay_out[:, g * gw:(g + 1) * gw]
        diag = []
        for r in range(hpg):
            h = g * hpg + r
            lane = MISC_DT0 + h
            seg = a_cs[:, lane:lane + 1] - a_cs_t[lane:lane + 1, :]
            m = (cb * jnp.exp(jnp.where(causal, seg, NEG_BIG))).astype(BF16)
            diag.append(jnp.dot(m, x_dt_b[:, h * SSD_HEAD_DIM:(h + 1) * SSD_HEAD_DIM],
                                preferred_element_type=F32))
        y_g = y_g + jnp.concatenate(diag, axis=1)
        new = jnp.dot(bm_g.T.astype(BF16), x_state[:, g * gw:(g + 1) * gw],
                      preferred_element_type=F32)
        state_ref[:, g * gw:(g + 1) * gw] = chunk_decay[:, g * gw:(g + 1) * gw] * st + new

        sl = slice(g * gw, (g + 1) * gw)
        y_g = y_g + xs[:, sl] * dsk_ref[:, sl]
        gy = y_g * _silu(z_ref[:, sl])
        ms = jnp.mean(gy * gy, axis=-1, keepdims=True)
        yc_ref[:, sl] = (gy * lax.rsqrt(ms + RMS_EPS) * nw_ref[:, sl]).astype(BF16)


def _seq_mix(rest, cw, cb, dtb, a_neg, dsk, nw, pw, ps, nb, nx):
    t = rest.shape[0]
    vec = lambda n: pl.BlockSpec((1, n), lambda b, c: (0, 0))
    blk = lambda b, c: _phys_block(b, c, nb, nx)
    rows = lambda n, col0: pl.BlockSpec((CHUNK, n), lambda b, c: (blk(b, c), col0 // n))
    return pl.pallas_call(
        _seq_mix_kernel,
        grid=(nb, nx + 1),
        in_specs=[
            rows(D_CONV, REST_XBC), rows(D_SSD, REST_Z), rows(LANES, REST_MISC),
            rows(D_POOL, REST_POOL),
            pl.BlockSpec((CONV_K, D_CONV), lambda b, c: (0, 0)),
            vec(D_CONV), vec(LANES), vec(LANES), vec(D_SSD), vec(D_SSD),
            pl.BlockSpec((POOL_GROUPS, POOL_GROUP_DIM, POOL_GROUP_DIM), lambda b, c: (0, 0, 0)),
            vec(D_POOL),
        ],
        out_specs=[rows(D_SSD, 0), rows(D_POOL, 0)],
        out_shape=[jax.ShapeDtypeStruct((t, D_SSD), BF16),
                   jax.ShapeDtypeStruct((t, D_POOL), BF16)],
        scratch_shapes=[pltpu.VMEM((SSD_STATE, D_SSD), F32),
                        pltpu.VMEM((CONV_HALO + CHUNK, D_CONV), F32),
                        pltpu.VMEM((POOL_HALO + CHUNK, D_POOL), F32)],
        compiler_params=_params("parallel", "arbitrary"),
        name="seq_mix",
    )(rest, rest, rest, rest, cw, cb, dtb, a_neg, dsk, nw, pw, ps)


def _out_ln_kernel(ya_ref, yb_ref, yc_ref, h_ref, w_ref, g_ref, b_ref, o_ref):
    tm = o_ref.shape[0]
    for r in range(0, tm, tm // 2):
        rows = slice(r, r + tm // 2)
        acc = jnp.dot(ya_ref[rows, :], w_ref[0:D_ATTN, :], preferred_element_type=F32)
        acc += jnp.dot(yb_ref[rows, :], w_ref[D_ATTN:D_ATTN + D_POOL, :],
                       preferred_element_type=F32)
        acc += jnp.dot(yc_ref[rows, :], w_ref[D_ATTN + D_POOL:, :], preferred_element_type=F32)
        o_ref[rows, :] = _layer_norm(ALPHA * h_ref[rows, :] + acc, g_ref[...], b_ref[...])


def _out_ln(ya, yb, yc, h, w, g, b, layer):
    t, d = h.shape
    tm = _pick_tile(ROW_TILES, t)
    rows = lambda n: pl.BlockSpec((tm, n), lambda i: (i, 0))
    return pl.pallas_call(
        _out_ln_kernel,
        grid=(t // tm,),
        in_specs=[
            rows(D_ATTN), rows(D_POOL), rows(D_SSD), rows(d),
            pl.BlockSpec((None, d, d), lambda i: (layer, 0, 0)),
            pl.BlockSpec((1, d), lambda i: (0, 0)),
            pl.BlockSpec((1, d), lambda i: (0, 0)),
        ],
        out_specs=rows(d),
        out_shape=jax.ShapeDtypeStruct((t, d), F32),
        compiler_params=_params("parallel"),
        name="out_proj_ln",
    )(ya, yb, yc, h, w, g, b)


def _lane_row(vals, offset):
    return jnp.zeros((1, LANES), F32).at[0, offset:offset + vals.shape[0]].set(vals.astype(F32))


def _proj_weight(w_in):
    c = D_ATTN
    q_k_v = jnp.concatenate([w_in[..., :c] * Q_SCALE, w_in[..., c:3 * c]], axis=-1)
    f = w_in[..., 3 * c:3 * c + ATTN_HEADS]
    o = 3 * c + ATTN_HEADS
    pool = w_in[..., o:o + D_POOL]
    z = w_in[..., o + D_POOL:o + D_POOL + D_SSD]
    xbc = w_in[..., o + D_POOL + D_SSD:o + D_POOL + D_SSD + D_CONV]
    dt = w_in[..., o + D_POOL + D_SSD + D_CONV:]
    pad = jnp.zeros(w_in.shape[:-1] + (PROJ_TILE - ATTN_HEADS - SSD_HEADS,), w_in.dtype)
    return jnp.concatenate([q_k_v, xbc, pool, z, f, dt, pad], axis=-1).astype(BF16)


def _mixer(h, nb, nx, layer, w_proj, b_fgate, pool_w, pool_scale, conv_w, conv_b, dt_bias,
           a_log, d_skip, ssd_norm_w, w_out, ln_g, ln_b):
    qkv, rest = _proj(h, w_proj, layer)
    qa, ka, vt = _gate_prep(rest, qkv, _lane_row(b_fgate, MISC_F0), nb, nx)
    ya = _attention(qa, ka, vt)
    ya = jnp.concatenate([ya[:, CHUNK:(nx + 1) * CHUNK].reshape(nb * nx * CHUNK, D_ATTN),
                          ya[:, :CHUNK].reshape(nb * CHUNK, D_ATTN)], axis=0)
    yc, yb = _seq_mix(rest, conv_w, conv_b.reshape(1, D_CONV), _lane_row(dt_bias, MISC_DT0),
                      _lane_row(-jnp.exp(a_log.astype(F32)), MISC_DT0),
                      jnp.repeat(d_skip, SSD_HEAD_DIM).reshape(1, D_SSD),
                      ssd_norm_w.reshape(1, D_SSD), pool_w.astype(BF16),
                      pool_scale.reshape(1, D_POOL), nb, nx)
    return _out_ln(ya, yb, yc, h, w_out, ln_g, ln_b, layer)


def kernel(x, meta, f1_gate, f1_up, f1_down, ln1_g, ln1_b, w_in, b_fgate, pool_w, pool_scale,
           conv_w, conv_b, dt_bias, a_log, d_skip, ssd_norm_w, w_out, ln2_g, ln2_b, f2_gate,
           f2_up, f2_down, ln3_g, ln3_b):
    nb, seq, d = x.shape
    assert d == D_MODEL and meta.shape == (N_META, D_MODEL) and seq % CHUNK == 0
    nx = seq // CHUNK
    depth = f1_gate.shape[0]
    head = jnp.concatenate([jnp.zeros((PAD, d), x.dtype), meta.astype(x.dtype)], axis=0)
    heads = jnp.broadcast_to(head[None], (nb, CHUNK, d)).reshape(nb * CHUNK, d)
    row = lambda v: v.reshape(1, d)
    bf = lambda w: w.astype(BF16)
    f1 = (bf(f1_gate), bf(f1_up), bf(f1_down))
    f2 = (bf(f2_gate), bf(f2_up), bf(f2_down))
    w_proj, w_o = _proj_weight(w_in), bf(w_out)
    h = x.reshape(nb * seq, d)
    for i in range(depth):
        h = _ffn_ln(h, *f1, row(ln1_g[i]), row(ln1_b[i]), i, tail=heads if i == 0 else None)
        h = _mixer(h, nb, nx, i, w_proj, b_fgate[i], pool_w[i], pool_scale[i], conv_w[i],
                   conv_b[i], dt_bias[i], a_log[i], d_skip[i], ssd_norm_w[i], w_o,
                   row(ln2_g[i]), row(ln2_b[i]))
        h = _ffn_ln(h, *f2, row(ln3_g[i]), row(ln3_b[i]), i,
                    rows=nb * seq if i == depth - 1 else None)
    return h.reshape(nb, seq, d)
```

```python
import functools

import jax
import jax.numpy as jnp
from jax import lax
from jax.experimental import pallas as pl
from jax.experimental.pallas import tpu as pltpu

F32 = jnp.float32
BF16 = jnp.bfloat16

D_MODEL = 2048
N_META = 16
CHUNK = 128
PAD = CHUNK - N_META
LANES = 128

ATTN_HEADS = 8
ATTN_HEAD_DIM = 64
D_ATTN = ATTN_HEADS * ATTN_HEAD_DIM
POOL_WINDOWS = (2, 4, 8, 16)
POOL_GROUPS = 4
D_POOL = 512
POOL_GROUP_DIM = D_POOL // POOL_GROUPS
POOL_HALO = 16
D_SSD = 1024
SSD_HEAD_DIM = 64
SSD_HEADS = D_SSD // SSD_HEAD_DIM
SSD_GROUPS = 2
SSD_STATE = 128
CONV_K = 4
CONV_HALO = 8
D_CONV = D_SSD + 2 * SSD_GROUPS * SSD_STATE
D_FF_TILE = 512
ROW_TILES = (512, 384, 256, 128)
DEPTH = 2
ALPHA = (2 * DEPTH) ** 0.25
LN_EPS = 1e-5
RMS_EPS = 1e-5
NEG_BIG = -1e30

PROJ_TILE = 512
REST_XBC, REST_POOL, REST_Z, REST_MISC = 0, D_CONV, D_CONV + D_POOL, D_CONV + D_POOL + D_SSD
D_REST = REST_MISC + LANES
MISC_F0 = 0
MISC_DT0 = ATTN_HEADS

VMEM_LIMIT = 56 * 1024 * 1024


def _params(*sem):
    return pltpu.CompilerParams(dimension_semantics=sem, vmem_limit_bytes=VMEM_LIMIT)


def _pick_tile(candidates, *sizes):
    for c in candidates:
        if all(n % c == 0 for n in sizes):
            return c
    raise ValueError(f"no tile in {candidates} divides {sizes}")


def _phys_block(b, j, nb, nx):
    return jnp.where(j == 0, nb * nx + b, b * nx + j - 1)


def _layer_norm(y, g, b):
    mu = jnp.mean(y, axis=-1, keepdims=True)
    yc = y - mu
    var = jnp.mean(yc * yc, axis=-1, keepdims=True)
    return yc * lax.rsqrt(var + LN_EPS) * g + b


def _silu(x):
    half = 0.5 * x
    return half + half * jnp.tanh(half)


def _softplus(x):
    return jnp.maximum(x, 0.0) + jnp.log1p(jnp.exp(-jnp.abs(x)))


def _cumsum_rows(x):
    n = x.shape[0]
    row = lax.broadcasted_iota(jnp.int32, x.shape, 0)
    d = 1
    while d < n:
        x = x + jnp.where(row >= d, pltpu.roll(x, d, axis=0), 0.0)
        d *= 2
    return x


def _ffn_ln_kernel(*refs, n_main_tiles):
    if n_main_tiles is None:
        x_ref, wg_ref, wu_ref, wd_ref, g_ref, b_ref, o_ref, xb_ref, acc_ref, y_ref = refs
        read_x = lambda: x_ref[...]
    else:
        x_ref, tail_ref, wg_ref, wu_ref, wd_ref, g_ref, b_ref, o_ref, xb_ref, acc_ref, y_ref = refs
        in_main = pl.program_id(0) < n_main_tiles
        read_x = lambda: jnp.where(in_main, x_ref[...], tail_ref[...])
    i, f = pl.program_id(0), pl.program_id(1)
    n_tiles, last = pl.num_programs(0) - 1, pl.num_programs(1) - 1

    def down_proj(xb):
        gate = jnp.dot(xb, wg_ref[...], preferred_element_type=F32)
        up = jnp.dot(xb, wu_ref[...], preferred_element_type=F32)
        act = (_silu(gate) * up).astype(BF16)
        return jnp.dot(act, wd_ref[...], preferred_element_type=F32)

    def normalize_previous():
        o_ref[...] = _layer_norm(y_ref[...], g_ref[...], b_ref[...])

    @pl.when((i == 0) & (f == 0))
    def _():
        y_ref[...] = jnp.zeros_like(y_ref)

    @pl.when((f == 0) & (i < n_tiles))
    def _():
        normalize_previous()
        xb = read_x().astype(BF16)
        xb_ref[...] = xb
        acc_ref[...] = down_proj(xb)

    @pl.when((f == 0) & (i == n_tiles))
    def _():
        normalize_previous()

    @pl.when((f > 0) & (f < last) & (i < n_tiles))
    def _():
        acc_ref[...] += down_proj(xb_ref[...])

    @pl.when((f == last) & (i < n_tiles))
    def _():
        y_ref[...] = ALPHA * read_x() + 0.5 * (acc_ref[...] + down_proj(xb_ref[...]))


def _ffn_ln(h, wg, wu, wd, g, b, layer, *, tail=None, rows=None):
    d = h.shape[1]
    ff = wg.shape[2]
    t = h.shape[0] + (0 if tail is None else tail.shape[0])
    rows = t if rows is None else rows
    tm = _pick_tile(ROW_TILES, rows, *((h.shape[0], tail.shape[0]) if tail is not None else ()))
    tf = _pick_tile((D_FF_TILE, 256, 128), ff)
    n_tiles, nf = rows // tm, ff // tf
    assert nf >= 2
    tile = lambda i: jnp.minimum(i, n_tiles - 1)
    col = lambda i, f: jnp.where(i == n_tiles, nf - 1, f)
    x_specs = [pl.BlockSpec((tm, d), lambda i, f: (tile(i), 0))]
    operands = [h]
    n_main = None
    if tail is not None:
        n_main = h.shape[0] // tm
        x_specs = [pl.BlockSpec((tm, d), lambda i, f: (jnp.minimum(i, n_main - 1), 0)),
                   pl.BlockSpec((tm, d), lambda i, f: (jnp.maximum(tile(i) - n_main, 0), 0))]
        operands = [h, tail]
    return pl.pallas_call(
        functools.partial(_ffn_ln_kernel, n_main_tiles=n_main),
        grid=(n_tiles + 1, nf),
        in_specs=x_specs + [
            pl.BlockSpec((None, d, tf), lambda i, f: (layer, 0, col(i, f))),
            pl.BlockSpec((None, d, tf), lambda i, f: (layer, 0, col(i, f))),
            pl.BlockSpec((None, tf, d), lambda i, f: (layer, col(i, f), 0)),
            pl.BlockSpec((1, d), lambda i, f: (0, 0)),
            pl.BlockSpec((1, d), lambda i, f: (0, 0)),
        ],
        out_specs=pl.BlockSpec((tm, d), lambda i, f: (jnp.maximum(i - 1, 0), 0)),
        out_shape=jax.ShapeDtypeStruct((rows, d), F32),
        scratch_shapes=[pltpu.VMEM((tm, d), BF16), pltpu.VMEM((tm, d), F32),
                        pltpu.VMEM((tm, d), F32)],
        compiler_params=_params("arbitrary", "arbitrary"),
        name="ffn_ln",
    )(*operands, wg, wu, wd, g, b)


N_QKV_SEGMENTS = 2


def _proj_kernel(x_ref, *refs):
    *w_refs, qkv_ref, rest_ref = refs
    xb = x_ref[...].astype(BF16)
    next_col = [0, 0]
    for s, w_ref in enumerate(w_refs):
        k = 0 if s < N_QKV_SEGMENTS else 1
        out_ref = (qkv_ref, rest_ref)[k]
        for c0 in range(0, w_ref.shape[1], PROJ_TILE):
            n = min(PROJ_TILE, w_ref.shape[1] - c0)
            r = jnp.dot(xb, w_ref[:, c0:c0 + n], preferred_element_type=F32)
            out_ref[:, next_col[k]:next_col[k] + n] = r.astype(out_ref.dtype)
            next_col[k] += n


def _proj(h, ws, layer):
    t, d = h.shape
    tm = _pick_tile(ROW_TILES, t)
    resident = lambda w: pl.BlockSpec((None, d, w.shape[2]), lambda i: (layer, 0, 0),
                                      pipeline_mode=pl.Buffered(1))
    return pl.pallas_call(
        _proj_kernel,
        grid=(t // tm,),
        in_specs=[pl.BlockSpec((tm, d), lambda i: (i, 0))] + [resident(w) for w in ws],
        out_specs=[
            pl.BlockSpec((tm, 3 * D_ATTN), lambda i: (i, 0)),
            pl.BlockSpec((tm, D_REST), lambda i: (i, 0)),
        ],
        out_shape=[
            jax.ShapeDtypeStruct((t, 3 * D_ATTN), BF16),
            jax.ShapeDtypeStruct((t, D_REST), F32),
        ],
        compiler_params=_params("parallel"),
        name="in_proj",
    )(h, *ws)


LOG2_E = 1.4426950408889634
Q_SCALE = LOG2_E * ATTN_HEAD_DIM ** -0.5
AUG_Q = ATTN_HEAD_DIM
VT_ROWS = ATTN_HEAD_DIM + 16
ATTN_TQ = 2 * CHUNK
ATTN_QK_LEAD = ATTN_HEADS


def _split3(c):
    hi = c.astype(BF16).astype(F32)
    mid = (c - hi).astype(BF16).astype(F32)
    lo = (c - hi - mid).astype(BF16).astype(F32)
    return hi, mid, lo


def _gate_kernel(m_ref, bf_ref, qkv_ref, qa_ref, ka_ref, vt_ref, carry_ref):
    first = pl.program_id(1) == 0

    @pl.when(first)
    def _():
        carry_ref[...] = jnp.zeros_like(carry_ref)

    row = lax.broadcasted_iota(jnp.int32, (CHUNK, LANES), 0)
    lane = lax.broadcasted_iota(jnp.int32, (CHUNK, LANES), 1)
    is_pad = row < jnp.where(first, PAD, 0)

    x = m_ref[...] + bf_ref[...]
    log_f = jnp.minimum(x, 0.0) - jnp.log1p(jnp.exp(-jnp.abs(x)))
    c = _cumsum_rows(jnp.where(is_pad, 0.0, log_f)) + carry_ref[...]
    carry_ref[...] = c[CHUNK - 1:CHUNK, :]
    parts = _split3(c * LOG2_E)

    ones_row = jnp.where(lax.broadcasted_iota(jnp.int32, (VT_ROWS - ATTN_HEAD_DIM, CHUNK), 0) == 0,
                         1.0, 0.0).astype(BF16)
    k_pad = jnp.where(lane == AUG_Q + 3, NEG_BIG, 0.0)
    for pair in range(ATTN_HEADS // 2):
        q2 = qkv_ref[:, pair * LANES:(pair + 1) * LANES].astype(F32)
        k2 = qkv_ref[:, D_ATTN + pair * LANES:D_ATTN + (pair + 1) * LANES].astype(F32)
        v2t = qkv_ref[:, 2 * D_ATTN + pair * LANES:2 * D_ATTN + (pair + 1) * LANES].astype(F32).T
        for sub in range(2):
            h = 2 * pair + sub
            qh = q2 if sub == 0 else pltpu.roll(q2, ATTN_HEAD_DIM, axis=1)
            kh = k2 if sub == 0 else pltpu.roll(k2, ATTN_HEAD_DIM, axis=1)
            q_extra = jnp.where((lane >= AUG_Q + 3) & (lane < AUG_Q + 6), 1.0, 0.0)
            k_extra = jnp.where((lane >= AUG_Q) & (lane < AUG_Q + 3), 1.0, 0.0)
            for n, part in enumerate(parts):
                col = jnp.broadcast_to(part[:, h:h + 1], (CHUNK, LANES))
                q_extra = jnp.where(lane == AUG_Q + n, col, q_extra)
                k_extra = jnp.where(lane == AUG_Q + 3 + n, -col, k_extra)
            k_aug = jnp.where(is_pad, k_pad, jnp.where(lane < AUG_Q, kh, k_extra))
            qa_ref[0, h] = jnp.where(lane < AUG_Q, qh, q_extra).astype(BF16)
            ka_ref[0, h] = k_aug.astype(BF16)
            vt_ref[0, 0, h, 0:ATTN_HEAD_DIM, :] = (
                v2t[sub * ATTN_HEAD_DIM:(sub + 1) * ATTN_HEAD_DIM, :].astype(BF16))
            vt_ref[0, 0, h, ATTN_HEAD_DIM:VT_ROWS, :] = ones_row


def _gate_prep(rest, qkv, bf, nb, nx):
    nblk = nx + 1
    misc_blk = REST_MISC // LANES
    nq = pl.cdiv(nblk, 2)
    src = lambda b, j: _phys_block(b, jnp.minimum(j, nblk - 1), nb, nx)
    return pl.pallas_call(
        _gate_kernel,
        grid=(nb, 2 * nq),
        in_specs=[
            pl.BlockSpec((CHUNK, LANES), lambda b, j: (src(b, j), misc_blk)),
            pl.BlockSpec((1, LANES), lambda b, j: (0, 0)),
            pl.BlockSpec((CHUNK, 3 * D_ATTN), lambda b, j: (src(b, j), 0)),
        ],
        out_specs=[
            pl.BlockSpec((1, ATTN_HEADS, CHUNK, LANES), lambda b, j: (b, 0, j, 0)),
            pl.BlockSpec((1, ATTN_HEADS, CHUNK, LANES), lambda b, j: (b, 0, j, 0)),
            pl.BlockSpec((1, 1, ATTN_HEADS, VT_ROWS, CHUNK), lambda b, j: (b, j // 2, 0, 0, j % 2)),
        ],
        out_shape=[
            jax.ShapeDtypeStruct((nb, ATTN_HEADS, nq * ATTN_TQ, LANES), BF16),
            jax.ShapeDtypeStruct((nb, ATTN_HEADS, nq * ATTN_TQ, LANES), BF16),
            jax.ShapeDtypeStruct((nb, nq, ATTN_HEADS, VT_ROWS, ATTN_TQ), BF16),
        ],
        scratch_shapes=[pltpu.VMEM((1, LANES), F32)],
        compiler_params=_params("parallel", "arbitrary"),
        name="gate_prep",
    )(rest, bf, qkv)


def _attn_kernel(qa_ref, ka_ref, vt_ref, o_ref, acc_ref, st_ref, m_ref):
    i = pl.program_id(1)
    key = lax.broadcasted_iota(jnp.int32, (ATTN_TQ, ATTN_TQ), 0)
    qry = lax.broadcasted_iota(jnp.int32, (ATTN_TQ, ATTN_TQ), 1)
    acc_ref[...] = jnp.zeros_like(acc_ref)
    m_ref[...] = jnp.full(m_ref.shape, NEG_BIG, F32)

    def step(start, nkeys, vt_chunks, mask=None):
        def qk(h):
            ka = ka_ref[0, h, pl.ds(start, nkeys), :]
            qa = qa_ref[0, h]
            st_ref[h, 0:nkeys, :] = lax.dot_general(ka, qa, (((1,), (1,)), ((), ())),
                                                    preferred_element_type=F32)

        def softmax_pv(h):
            st = st_ref[h, 0:nkeys, :]
            if mask is not None:
                st = jnp.where(mask, st, NEG_BIG)
            m_old = m_ref[h]
            m_new = jnp.maximum(m_old, jnp.max(st, axis=0, keepdims=True))
            m_ref[h] = m_new
            p = jnp.exp2(st - m_new).astype(BF16)
            acc = jnp.exp2(m_old - m_new) * acc_ref[h]
            r = 0
            for vt in vt_chunks(h):
                n = vt.shape[1]
                acc += jnp.dot(vt, p[r:r + n, :], preferred_element_type=F32)
                r += n
            acc_ref[h] = acc

        for h in range(ATTN_HEADS + ATTN_QK_LEAD):
            if h < ATTN_HEADS:
                qk(h)
            if h >= ATTN_QK_LEAD:
                softmax_pv(h - ATTN_QK_LEAD)

    def quad_step(s, _):
        step(pl.multiple_of(s * 2 * ATTN_TQ, 2 * ATTN_TQ), 2 * ATTN_TQ,
             lambda h: (vt_ref[0, 2 * s, h], vt_ref[0, 2 * s + 1, h]))
        return 0

    lax.fori_loop(0, i // 2, quad_step, 0)

    @pl.when(i % 2 == 1)
    def _():
        step(pl.multiple_of((i - 1) * ATTN_TQ, ATTN_TQ), ATTN_TQ, lambda h: (vt_ref[0, i - 1, h],))

    step(pl.multiple_of(i * ATTN_TQ, ATTN_TQ), ATTN_TQ, lambda h: (vt_ref[0, i, h],), key <= qry)

    for pair in range(ATTN_HEADS // 2):
        halves = []
        for sub in range(2):
            a = acc_ref[2 * pair + sub]
            halves.append(a[0:ATTN_HEAD_DIM, :] * (1.0 / a[ATTN_HEAD_DIM:ATTN_HEAD_DIM + 1, :]))
        o_ref[0, :, pair * LANES:(pair + 1) * LANES] = (
            jnp.concatenate(halves, axis=0).T.astype(BF16))


def _attention(qa, ka, vt):
    nb, _, lq, _ = qa.shape
    nq = lq // ATTN_TQ
    return pl.pallas_call(
        _attn_kernel,
        grid=(nb, nq),
        in_specs=[
            pl.BlockSpec((1, ATTN_HEADS, ATTN_TQ, LANES), lambda b, i: (b, 0, i, 0)),
            pl.BlockSpec((1, ATTN_HEADS, lq, LANES), lambda b, i: (b, 0, 0, 0)),
            pl.BlockSpec((1, nq, ATTN_HEADS, VT_ROWS, ATTN_TQ), lambda b, i: (b, 0, 0, 0, 0)),
        ],
        out_specs=pl.BlockSpec((1, ATTN_TQ, D_ATTN), lambda b, i: (b, i, 0)),
        out_shape=jax.ShapeDtypeStruct((nb, lq, D_ATTN), BF16),
        scratch_shapes=[pltpu.VMEM((ATTN_HEADS, VT_ROWS, ATTN_TQ), F32),
                        pltpu.VMEM((ATTN_HEADS, 2 * ATTN_TQ, ATTN_TQ), F32),
                        pltpu.VMEM((ATTN_HEADS, 1, ATTN_TQ), F32)],
        compiler_params=_params("parallel", "arbitrary"),
        name="fox_attention",
    )(qa, ka, vt)


def _expand_heads(v):
    r = v.shape[0]
    lane = lax.broadcasted_iota(jnp.int32, (r, LANES), 1)
    parts = []
    for k in range(SSD_HEADS // 2):
        l0 = MISC_DT0 + 2 * k
        a = jnp.broadcast_to(v[:, l0:l0 + 1], (r, LANES))
        b = jnp.broadcast_to(v[:, l0 + 1:l0 + 2], (r, LANES))
        parts.append(jnp.where(lane < SSD_HEAD_DIM, a, b))
    return jnp.concatenate(parts, axis=1)


def _pool_chunk(c, u_ref, pw_ref, ps_ref, yb_ref, buf_ref):
    @pl.when(c == 0)
    def _():
        buf_ref[0:POOL_HALO, :] = jnp.zeros((POOL_HALO, D_POOL), F32)

    @pl.when(c > 0)
    def _():
        buf_ref[0:POOL_HALO, :] = buf_ref[CHUNK:CHUNK + POOL_HALO, :]

    buf_ref[POOL_HALO:POOL_HALO + CHUNK, :] = u_ref[...]

    @pl.when(c == 0)
    def _():
        buf_ref[POOL_HALO:POOL_HALO + PAD, :] = jnp.zeros((PAD, D_POOL), F32)

    seen = c * CHUNK - PAD + 1 + lax.broadcasted_iota(jnp.int32, (CHUNK, POOL_GROUP_DIM), 0)
    for g, w in enumerate(POOL_WINDOWS):
        lo, hi = g * POOL_GROUP_DIM, (g + 1) * POOL_GROUP_DIM
        assert w & (w - 1) == 0 and w <= POOL_HALO
        win = buf_ref[:, lo:hi]
        s = 1
        while s < w:
            win = win + pltpu.roll(win, s, axis=0)
            s *= 2
        win = win[POOL_HALO:, :]
        u = buf_ref[POOL_HALO:POOL_HALO + CHUNK, lo:hi]
        cnt = jnp.clip(seen, 1, w).astype(F32)
        diff = (win / cnt - u).astype(BF16)
        mixed = jnp.dot(diff, pw_ref[g], preferred_element_type=F32)
        yb_ref[:, lo:hi] = (mixed * ps_ref[:, lo:hi]).astype(BF16)


def _seq_mix_kernel(xbc_ref, z_ref, misc_ref, u_ref, cw_ref, cb_ref, dtb_ref, a_ref, dsk_ref,
                    nw_ref, pw_ref, ps_ref, yc_ref, yb_ref, state_ref, ext_ref, buf_ref):
    c = pl.program_id(1)
    _pool_chunk(c, u_ref, pw_ref, ps_ref, yb_ref, buf_ref)

    @pl.when(c == 0)
    def _():
        state_ref[...] = jnp.zeros_like(state_ref)
        ext_ref[0:CONV_HALO, :] = jnp.zeros((CONV_HALO, D_CONV), F32)

    @pl.when(c > 0)
    def _():
        ext_ref[0:CONV_HALO, :] = ext_ref[CHUNK:CHUNK + CONV_HALO, :]

    ext_ref[CONV_HALO:CONV_HALO + CHUNK, :] = xbc_ref[...]

    @pl.when(c == 0)
    def _():
        ext_ref[CONV_HALO:CONV_HALO + PAD, :] = jnp.zeros((PAD, D_CONV), F32)

    conv = cb_ref[...]
    for k in range(CONV_K):
        off = CONV_HALO - (CONV_K - 1) + k
        conv = conv + cw_ref[k:k + 1, :] * ext_ref[off:off + CHUNK, :]
    xc = _silu(conv)
    xs = xc[:, :D_SSD]
    gn = SSD_GROUPS * SSD_STATE
    bm = xc[:, D_SSD:D_SSD + gn]
    cm = xc[:, D_SSD + gn:D_SSD + 2 * gn]

    row = lax.broadcasted_iota(jnp.int32, (CHUNK, CHUNK), 0)
    col = lax.broadcasted_iota(jnp.int32, (CHUNK, CHUNK), 1)
    dt = _softplus(misc_ref[...] + dtb_ref[...])
    dt = jnp.where(row >= jnp.where(c == 0, PAD, 0), dt, 0.0)
    a_cs = _cumsum_rows(dt * a_ref[...])
    a_cs_t = a_cs.T
    a_last = a_cs[CHUNK - 1:CHUNK, :]
    x_dt = xs * _expand_heads(dt)
    x_dt_b = x_dt.astype(BF16)
    decay_out = _expand_heads(jnp.exp2(a_cs))
    x_state = (x_dt * _expand_heads(jnp.exp2(a_last - a_cs))).astype(BF16)
    chunk_decay = _expand_heads(jnp.exp2(a_last))

    causal = col <= row
    hpg = SSD_HEADS // SSD_GROUPS
    gw = hpg * SSD_HEAD_DIM
    for g in range(SSD_GROUPS):
        n0, n1 = g * SSD_STATE, (g + 1) * SSD_STATE
        cm_g = cm[:, n0:n1].astype(BF16)
        bm_g = bm[:, n0:n1]
        cb = lax.dot_general(cm_g, bm_g.astype(BF16), (((1,), (1,)), ((), ())),
                             preferred_element_type=F32)
        st = state_ref[:, g * gw:(g + 1) * gw]
        y_off = jnp.dot(cm_g, st.astype(BF16), preferred_element_type=F32)
        y_g = y_off * decay_out[:, g * gw:(g + 1) * gw]
        diag = []
        for r in range(hpg):
            h = g * hpg + r
            lane = MISC_DT0 + h
            seg = a_cs[:, lane:lane + 1] - a_cs_t[lane:lane + 1, :]
            m = (cb * jnp.exp2(jnp.where(causal, seg, NEG_BIG))).astype(BF16)
            diag.append(jnp.dot(m, x_dt_b[:, h * SSD_HEAD_DIM:(h + 1) * SSD_HEAD_DIM],
                                preferred_element_type=F32))
        y_g = y_g + jnp.concatenate(diag, axis=1)
        new = jnp.dot(bm_g.T.astype(BF16), x_state[:, g * gw:(g + 1) * gw],
                      preferred_element_type=F32)
        state_ref[:, g * gw:(g + 1) * gw] = chunk_decay[:, g * gw:(g + 1) * gw] * st + new

        sl = slice(g * gw, (g + 1) * gw)
        y_g = y_g + xs[:, sl] * dsk_ref[:, sl]
        gy = y_g * _silu(z_ref[:, sl])
        ms = jnp.mean(gy * gy, axis=-1, keepdims=True)
        yc_ref[:, sl] = (gy * lax.rsqrt(ms + RMS_EPS) * nw_ref[:, sl]).astype(BF16)


def _seq_mix(rest, cw, cb, dtb, a_neg, dsk, nw, pw, ps, nb, nx):
    t = rest.shape[0]
    vec = lambda n: pl.BlockSpec((1, n), lambda b, c: (0, 0))
    blk = lambda b, c: _phys_block(b, c, nb, nx)
    rows = lambda n, col0: pl.BlockSpec((CHUNK, n), lambda b, c: (blk(b, c), col0 // n))
    return pl.pallas_call(
        _seq_mix_kernel,
        grid=(nb, nx + 1),
        in_specs=[
            rows(D_CONV, REST_XBC), rows(D_SSD, REST_Z), rows(LANES, REST_MISC),
            rows(D_POOL, REST_POOL),
            pl.BlockSpec((CONV_K, D_CONV), lambda b, c: (0, 0)),
            vec(D_CONV), vec(LANES), vec(LANES), vec(D_SSD), vec(D_SSD),
            pl.BlockSpec((POOL_GROUPS, POOL_GROUP_DIM, POOL_GROUP_DIM), lambda b, c: (0, 0, 0)),
            vec(D_POOL),
        ],
        out_specs=[rows(D_SSD, 0), rows(D_POOL, 0)],
        out_shape=[jax.ShapeDtypeStruct((t, D_SSD), BF16),
                   jax.ShapeDtypeStruct((t, D_POOL), BF16)],
        scratch_shapes=[pltpu.VMEM((SSD_STATE, D_SSD), F32),
                        pltpu.VMEM((CONV_HALO + CHUNK, D_CONV), F32),
                        pltpu.VMEM((POOL_HALO + CHUNK, D_POOL), F32)],
        compiler_params=_params("parallel", "arbitrary"),
        name="seq_mix",
    )(rest, rest, rest, rest, cw, cb, dtb, a_neg, dsk, nw, pw, ps)


def _out_ln_kernel(ya_ref, yb_ref, yc_ref, h_ref, w_ref, g_ref, b_ref, o_ref):
    tm = o_ref.shape[0]
    for r in range(0, tm, tm // 2):
        rows = slice(r, r + tm // 2)
        acc = jnp.dot(ya_ref[rows, :], w_ref[0:D_ATTN, :], preferred_element_type=F32)
        acc += jnp.dot(yb_ref[rows, :], w_ref[D_ATTN:D_ATTN + D_POOL, :],
                       preferred_element_type=F32)
        acc += jnp.dot(yc_ref[rows, :], w_ref[D_ATTN + D_POOL:, :], preferred_element_type=F32)
        o_ref[rows, :] = _layer_norm(ALPHA * h_ref[rows, :] + acc, g_ref[...], b_ref[...])


def _out_ln(ya, yb, yc, h, w, g, b, layer):
    t, d = h.shape
    tm = _pick_tile(ROW_TILES, t)
    rows = lambda n: pl.BlockSpec((tm, n), lambda i: (i, 0))
    return pl.pallas_call(
        _out_ln_kernel,
        grid=(t // tm,),
        in_specs=[
            rows(D_ATTN), rows(D_POOL), rows(D_SSD), rows(d),
            pl.BlockSpec((None, d, d), lambda i: (layer, 0, 0)),
            pl.BlockSpec((1, d), lambda i: (0, 0)),
            pl.BlockSpec((1, d), lambda i: (0, 0)),
        ],
        out_specs=rows(d),
        out_shape=jax.ShapeDtypeStruct((t, d), F32),
        compiler_params=_params("parallel"),
        name="out_proj_ln",
    )(ya, yb, yc, h, w, g, b)


def _lane_row(vals, offset):
    return jnp.zeros((1, LANES), F32).at[0, offset:offset + vals.shape[0]].set(vals.astype(F32))


def _proj_weight(w_in):
    c = D_ATTN
    o = 3 * c + ATTN_HEADS
    f = w_in[..., 3 * c:o]
    dt = w_in[..., o + D_POOL + D_SSD + D_CONV:]
    pad = jnp.zeros(w_in.shape[:-1] + (LANES - ATTN_HEADS - SSD_HEADS,), w_in.dtype)
    segments = (w_in[..., :c] * Q_SCALE,
                w_in[..., c:3 * c],
                w_in[..., o + D_POOL + D_SSD:o + D_POOL + D_SSD + D_CONV],
                w_in[..., o:o + D_POOL],
                w_in[..., o + D_POOL:o + D_POOL + D_SSD],
                jnp.concatenate([f, dt, pad], axis=-1))
    return tuple(w.astype(BF16) for w in segments)


def _mixer(h, nb, nx, layer, w_proj, b_fgate, pool_w, pool_scale, conv_w, conv_b, dt_bias,
           a_log, d_skip, ssd_norm_w, w_out, ln_g, ln_b):
    qkv, rest = _proj(h, w_proj, layer)
    qa, ka, vt = _gate_prep(rest, qkv, _lane_row(b_fgate, MISC_F0), nb, nx)
    ya = _attention(qa, ka, vt)
    ya = jnp.concatenate([ya[:, CHUNK:(nx + 1) * CHUNK].reshape(nb * nx * CHUNK, D_ATTN),
                          ya[:, :CHUNK].reshape(nb * CHUNK, D_ATTN)], axis=0)
    yc, yb = _seq_mix(rest, conv_w, conv_b.reshape(1, D_CONV), _lane_row(dt_bias, MISC_DT0),
                      _lane_row(-LOG2_E * jnp.exp(a_log.astype(F32)), MISC_DT0),
                      jnp.repeat(d_skip, SSD_HEAD_DIM).reshape(1, D_SSD),
                      ssd_norm_w.reshape(1, D_SSD), pool_w.astype(BF16),
                      pool_scale.reshape(1, D_POOL), nb, nx)
    return _out_ln(ya, yb, yc, h, w_out, ln_g, ln_b, layer)


def kernel(x, meta, f1_gate, f1_up, f1_down, ln1_g, ln1_b, w_in, b_fgate, pool_w, pool_scale,
           conv_w, conv_b, dt_bias, a_log, d_skip, ssd_norm_w, w_out, ln2_g, ln2_b, f2_gate,
           f2_up, f2_down, ln3_g, ln3_b):
    nb, seq, d = x.shape
    assert d == D_MODEL and meta.shape == (N_META, D_MODEL) and seq % CHUNK == 0
    nx = seq // CHUNK
    depth = f1_gate.shape[0]
    head = jnp.concatenate([jnp.zeros((PAD, d), x.dtype), meta.astype(x.dtype)], axis=0)
    heads = jnp.broadcast_to(head[None], (nb, CHUNK, d)).reshape(nb * CHUNK, d)
    row = lambda v: v.reshape(1, d)
    bf = lambda w: w.astype(BF16)
    f1 = (bf(f1_gate), bf(f1_up), bf(f1_down))
    f2 = (bf(f2_gate), bf(f2_up), bf(f2_down))
    w_proj, w_o = _proj_weight(w_in), bf(w_out)
    h = x.reshape(nb * seq, d)
    for i in range(depth):
        h = _ffn_ln(h, *f1, row(ln1_g[i]), row(ln1_b[i]), i, tail=heads if i == 0 else None)
        h = _mixer(h, nb, nx, i, w_proj, b_fgate[i], pool_w[i], pool_scale[i], conv_w[i],
                   conv_b[i], dt_bias[i], a_log[i], d_skip[i], ssd_norm_w[i], w_o,
                   row(ln2_g[i]), row(ln2_b[i]))
        h = _ffn_ln(h, *f2, row(ln3_g[i]), row(ln3_b[i]), i,
                    rows=nb * seq if i == depth - 1 else None)
    return h.reshape(nb, seq, d)
```

```python
import functools

import jax
import jax.numpy as jnp
from jax import lax
from jax.experimental import pallas as pl
from jax.experimental.pallas import tpu as pltpu

F32 = jnp.float32
BF16 = jnp.bfloat16

D_MODEL = 2048
N_META = 16
CHUNK = 128
PAD = CHUNK - N_META
LANES = 128

ATTN_HEADS = 8
ATTN_HEAD_DIM = 64
D_ATTN = ATTN_HEADS * ATTN_HEAD_DIM
POOL_WINDOWS = (2, 4, 8, 16)
POOL_GROUPS = 4
D_POOL = 512
POOL_GROUP_DIM = D_POOL // POOL_GROUPS
POOL_HALO = 16
D_SSD = 1024
SSD_HEAD_DIM = 64
SSD_HEADS = D_SSD // SSD_HEAD_DIM
SSD_GROUPS = 2
SSD_STATE = 128
CONV_K = 4
CONV_HALO = 8
D_CONV = D_SSD + 2 * SSD_GROUPS * SSD_STATE
D_FF_TILE = 512
ROW_TILES = (512, 384, 256, 128)
DEPTH = 2
ALPHA = (2 * DEPTH) ** 0.25
LN_EPS = 1e-5
RMS_EPS = 1e-5
NEG_BIG = -1e30

PROJ_TILE = 512
REST_XBC, REST_POOL, REST_Z, REST_MISC = 0, D_CONV, D_CONV + D_POOL, D_CONV + D_POOL + D_SSD
D_REST = REST_MISC + LANES
MISC_F0 = 0
MISC_DT0 = ATTN_HEADS

VMEM_LIMIT = 56 * 1024 * 1024


def _params(*sem):
    return pltpu.CompilerParams(dimension_semantics=sem, vmem_limit_bytes=VMEM_LIMIT)


def _pick_tile(candidates, *sizes):
    for c in candidates:
        if all(n % c == 0 for n in sizes):
            return c
    raise ValueError(f"no tile in {candidates} divides {sizes}")


def _phys_block(b, j, nb, nx):
    return jnp.where(j == 0, nb * nx + b, b * nx + j - 1)


def _layer_norm(y, g, b):
    mu = jnp.mean(y, axis=-1, keepdims=True)
    yc = y - mu
    var = jnp.mean(yc * yc, axis=-1, keepdims=True)
    return yc * lax.rsqrt(var + LN_EPS) * g + b


def _silu(x):
    half = 0.5 * x
    return half + half * jnp.tanh(half)


def _softplus(x):
    return jnp.maximum(x, 0.0) + jnp.log1p(jnp.exp(-jnp.abs(x)))


def _cumsum_rows(x):
    n = x.shape[0]
    row = lax.broadcasted_iota(jnp.int32, x.shape, 0)
    d = 1
    while d < n:
        x = x + jnp.where(row >= d, pltpu.roll(x, d, axis=0), 0.0)
        d *= 2
    return x


def _ffn_ln_kernel(*refs, n_main_tiles):
    if n_main_tiles is None:
        x_ref, wg_ref, wu_ref, wd_ref, g_ref, b_ref, o_ref, xb_ref, acc_ref, y_ref = refs
        read_x = lambda: x_ref[...]
    else:
        x_ref, tail_ref, wg_ref, wu_ref, wd_ref, g_ref, b_ref, o_ref, xb_ref, acc_ref, y_ref = refs
        in_main = pl.program_id(0) < n_main_tiles
        read_x = lambda: jnp.where(in_main, x_ref[...], tail_ref[...])
    i, f = pl.program_id(0), pl.program_id(1)
    n_tiles, last = pl.num_programs(0) - 1, pl.num_programs(1) - 1

    def down_proj(xb):
        gate = jnp.dot(xb, wg_ref[...], preferred_element_type=F32)
        up = jnp.dot(xb, wu_ref[...], preferred_element_type=F32)
        act = (_silu(gate) * up).astype(BF16)
        return jnp.dot(act, wd_ref[...], preferred_element_type=F32)

    def normalize_previous():
        o_ref[...] = _layer_norm(y_ref[...], g_ref[...], b_ref[...])

    @pl.when((i == 0) & (f == 0))
    def _():
        y_ref[...] = jnp.zeros_like(y_ref)

    @pl.when((f == 0) & (i < n_tiles))
    def _():
        normalize_previous()
        xb = read_x().astype(BF16)
        xb_ref[...] = xb
        acc_ref[...] = down_proj(xb)

    @pl.when((f == 0) & (i == n_tiles))
    def _():
        normalize_previous()

    @pl.when((f > 0) & (f < last) & (i < n_tiles))
    def _():
        acc_ref[...] += down_proj(xb_ref[...])

    @pl.when((f == last) & (i < n_tiles))
    def _():
        y_ref[...] = ALPHA * read_x() + 0.5 * (acc_ref[...] + down_proj(xb_ref[...]))


def _ffn_ln(h, wg, wu, wd, g, b, layer, *, tail=None, rows=None):
    d = h.shape[1]
    ff = wg.shape[2]
    t = h.shape[0] + (0 if tail is None else tail.shape[0])
    rows = t if rows is None else rows
    tm = _pick_tile(ROW_TILES, rows, *((h.shape[0], tail.shape[0]) if tail is not None else ()))
    tf = _pick_tile((D_FF_TILE, 256, 128), ff)
    n_tiles, nf = rows // tm, ff // tf
    assert nf >= 2
    tile = lambda i: jnp.minimum(i, n_tiles - 1)
    col = lambda i, f: jnp.where(i == n_tiles, nf - 1, f)
    x_specs = [pl.BlockSpec((tm, d), lambda i, f: (tile(i), 0))]
    operands = [h]
    n_main = None
    if tail is not None:
        n_main = h.shape[0] // tm
        x_specs = [pl.BlockSpec((tm, d), lambda i, f: (jnp.minimum(i, n_main - 1), 0)),
                   pl.BlockSpec((tm, d), lambda i, f: (jnp.maximum(tile(i) - n_main, 0), 0))]
        operands = [h, tail]
    return pl.pallas_call(
        functools.partial(_ffn_ln_kernel, n_main_tiles=n_main),
        grid=(n_tiles + 1, nf),
        in_specs=x_specs + [
            pl.BlockSpec((None, d, tf), lambda i, f: (layer, 0, col(i, f))),
            pl.BlockSpec((None, d, tf), lambda i, f: (layer, 0, col(i, f))),
            pl.BlockSpec((None, tf, d), lambda i, f: (layer, col(i, f), 0)),
            pl.BlockSpec((1, d), lambda i, f: (0, 0)),
            pl.BlockSpec((1, d), lambda i, f: (0, 0)),
        ],
        out_specs=pl.BlockSpec((tm, d), lambda i, f: (jnp.maximum(i - 1, 0), 0)),
        out_shape=jax.ShapeDtypeStruct((rows, d), F32),
        scratch_shapes=[pltpu.VMEM((tm, d), BF16), pltpu.VMEM((tm, d), F32),
                        pltpu.VMEM((tm, d), F32)],
        compiler_params=_params("arbitrary", "arbitrary"),
        name="ffn_ln",
    )(*operands, wg, wu, wd, g, b)


N_QKV_SEGMENTS = 2


def _proj_kernel(x_ref, *refs):
    *w_refs, qkv_ref, rest_ref = refs
    xb = x_ref[...].astype(BF16)
    next_col = [0, 0]
    for s, w_ref in enumerate(w_refs):
        k = 0 if s < N_QKV_SEGMENTS else 1
        out_ref = (qkv_ref, rest_ref)[k]
        for c0 in range(0, w_ref.shape[0], PROJ_TILE):
            n = min(PROJ_TILE, w_ref.shape[0] - c0)
            r = lax.dot_general(xb, w_ref[c0:c0 + n, :], (((1,), (1,)), ((), ())),
                                preferred_element_type=F32)
            out_ref[:, next_col[k]:next_col[k] + n] = r.astype(out_ref.dtype)
            next_col[k] += n


def _proj(h, ws, layer):
    t, d = h.shape
    tm = _pick_tile(ROW_TILES, t)
    resident = lambda w: pl.BlockSpec((None, w.shape[1], d), lambda i: (layer, 0, 0),
                                      pipeline_mode=pl.Buffered(1))
    return pl.pallas_call(
        _proj_kernel,
        grid=(t // tm,),
        in_specs=[pl.BlockSpec((tm, d), lambda i: (i, 0))] + [resident(w) for w in ws],
        out_specs=[
            pl.BlockSpec((tm, 3 * D_ATTN), lambda i: (i, 0)),
            pl.BlockSpec((tm, D_REST), lambda i: (i, 0)),
        ],
        out_shape=[
            jax.ShapeDtypeStruct((t, 3 * D_ATTN), BF16),
            jax.ShapeDtypeStruct((t, D_REST), F32),
        ],
        compiler_params=_params("parallel"),
        name="in_proj",
    )(h, *ws)


LOG2_E = 1.4426950408889634
Q_SCALE = LOG2_E * ATTN_HEAD_DIM ** -0.5
AUG_Q = ATTN_HEAD_DIM
VT_ROWS = ATTN_HEAD_DIM + 16
ATTN_TQ = 2 * CHUNK
ATTN_QK_LEAD = ATTN_HEADS


def _split3(c):
    hi = c.astype(BF16).astype(F32)
    mid = (c - hi).astype(BF16).astype(F32)
    lo = (c - hi - mid).astype(BF16).astype(F32)
    return hi, mid, lo


def _gate_kernel(m_ref, bf_ref, qkv_ref, qa_ref, ka_ref, vt_ref, carry_ref):
    first = pl.program_id(1) == 0

    @pl.when(first)
    def _():
        carry_ref[...] = jnp.zeros_like(carry_ref)

    row = lax.broadcasted_iota(jnp.int32, (CHUNK, LANES), 0)
    lane = lax.broadcasted_iota(jnp.int32, (CHUNK, LANES), 1)
    is_pad = row < jnp.where(first, PAD, 0)

    x = m_ref[...] + bf_ref[...]
    log_f = jnp.minimum(x, 0.0) - jnp.log1p(jnp.exp(-jnp.abs(x)))
    c = _cumsum_rows(jnp.where(is_pad, 0.0, log_f)) + carry_ref[...]
    carry_ref[...] = c[CHUNK - 1:CHUNK, :]
    parts = _split3(c * LOG2_E)

    ones_row = jnp.where(lax.broadcasted_iota(jnp.int32, (VT_ROWS - ATTN_HEAD_DIM, CHUNK), 0) == 0,
                         1.0, 0.0).astype(BF16)
    k_pad = jnp.where(lane == AUG_Q + 3, NEG_BIG, 0.0)
    for pair in range(ATTN_HEADS // 2):
        q2 = qkv_ref[:, pair * LANES:(pair + 1) * LANES].astype(F32)
        k2 = qkv_ref[:, D_ATTN + pair * LANES:D_ATTN + (pair + 1) * LANES].astype(F32)
        v2t = qkv_ref[:, 2 * D_ATTN + pair * LANES:2 * D_ATTN + (pair + 1) * LANES].astype(F32).T
        for sub in range(2):
            h = 2 * pair + sub
            qh = q2 if sub == 0 else pltpu.roll(q2, ATTN_HEAD_DIM, axis=1)
            kh = k2 if sub == 0 else pltpu.roll(k2, ATTN_HEAD_DIM, axis=1)
            q_extra = jnp.where((lane >= AUG_Q + 3) & (lane < AUG_Q + 6), 1.0, 0.0)
            k_extra = jnp.where((lane >= AUG_Q) & (lane < AUG_Q + 3), 1.0, 0.0)
            for n, part in enumerate(parts):
                col = jnp.broadcast_to(part[:, h:h + 1], (CHUNK, LANES))
                q_extra = jnp.where(lane == AUG_Q + n, col, q_extra)
                k_extra = jnp.where(lane == AUG_Q + 3 + n, -col, k_extra)
            k_aug = jnp.where(is_pad, k_pad, jnp.where(lane < AUG_Q, kh, k_extra))
            qa_ref[0, h] = jnp.where(lane < AUG_Q, qh, q_extra).astype(BF16)
            ka_ref[0, h] = k_aug.astype(BF16)
            vt_ref[0, 0, h, 0:ATTN_HEAD_DIM, :] = (
                v2t[sub * ATTN_HEAD_DIM:(sub + 1) * ATTN_HEAD_DIM, :].astype(BF16))
            vt_ref[0, 0, h, ATTN_HEAD_DIM:VT_ROWS, :] = ones_row


def _gate_prep(rest, qkv, bf, nb, nx):
    nblk = nx + 1
    misc_blk = REST_MISC // LANES
    nq = pl.cdiv(nblk, 2)
    src = lambda b, j: _phys_block(b, jnp.minimum(j, nblk - 1), nb, nx)
    return pl.pallas_call(
        _gate_kernel,
        grid=(nb, 2 * nq),
        in_specs=[
            pl.BlockSpec((CHUNK, LANES), lambda b, j: (src(b, j), misc_blk)),
            pl.BlockSpec((1, LANES), lambda b, j: (0, 0)),
            pl.BlockSpec((CHUNK, 3 * D_ATTN), lambda b, j: (src(b, j), 0)),
        ],
        out_specs=[
            pl.BlockSpec((1, ATTN_HEADS, CHUNK, LANES), lambda b, j: (b, 0, j, 0)),
            pl.BlockSpec((1, ATTN_HEADS, CHUNK, LANES), lambda b, j: (b, 0, j, 0)),
            pl.BlockSpec((1, 1, ATTN_HEADS, VT_ROWS, CHUNK), lambda b, j: (b, j // 2, 0, 0, j % 2)),
        ],
        out_shape=[
            jax.ShapeDtypeStruct((nb, ATTN_HEADS, nq * ATTN_TQ, LANES), BF16),
            jax.ShapeDtypeStruct((nb, ATTN_HEADS, nq * ATTN_TQ, LANES), BF16),
            jax.ShapeDtypeStruct((nb, nq, ATTN_HEADS, VT_ROWS, ATTN_TQ), BF16),
        ],
        scratch_shapes=[pltpu.VMEM((1, LANES), F32)],
        compiler_params=_params("parallel", "arbitrary"),
        name="gate_prep",
    )(rest, bf, qkv)


def _attn_kernel(qa_ref, ka_ref, vt_ref, o_ref, acc_ref, st_ref, m_ref):
    i = pl.program_id(1)
    key = lax.broadcasted_iota(jnp.int32, (ATTN_TQ, ATTN_TQ), 0)
    qry = lax.broadcasted_iota(jnp.int32, (ATTN_TQ, ATTN_TQ), 1)
    acc_ref[...] = jnp.zeros_like(acc_ref)
    m_ref[...] = jnp.full(m_ref.shape, NEG_BIG, F32)

    def step(start, nkeys, vt_chunks, mask=None):
        def qk(h):
            ka = ka_ref[0, h, pl.ds(start, nkeys), :]
            qa = qa_ref[0, h]
            st_ref[h, 0:nkeys, :] = lax.dot_general(ka, qa, (((1,), (1,)), ((), ())),
                                                    preferred_element_type=F32)

        def softmax_pv(h):
            st = st_ref[h, 0:nkeys, :]
            if mask is not None:
                st = jnp.where(mask, st, NEG_BIG)
            m_old = m_ref[h]
            m_new = jnp.maximum(m_old, jnp.max(st, axis=0, keepdims=True))
            m_ref[h] = m_new
            p = jnp.exp2(st - m_new).astype(BF16)
            acc = jnp.exp2(m_old - m_new) * acc_ref[h]
            r = 0
            for vt in vt_chunks(h):
                n = vt.shape[1]
                acc += jnp.dot(vt, p[r:r + n, :], preferred_element_type=F32)
                r += n
            acc_ref[h] = acc

        for h in range(ATTN_HEADS + ATTN_QK_LEAD):
            if h < ATTN_HEADS:
                qk(h)
            if h >= ATTN_QK_LEAD:
                softmax_pv(h - ATTN_QK_LEAD)

    def quad_step(s, _):
        step(pl.multiple_of(s * 2 * ATTN_TQ, 2 * ATTN_TQ), 2 * ATTN_TQ,
             lambda h: (vt_ref[0, 2 * s, h], vt_ref[0, 2 * s + 1, h]))
        return 0

    lax.fori_loop(0, i // 2, quad_step, 0)

    @pl.when(i % 2 == 1)
    def _():
        step(pl.multiple_of((i - 1) * ATTN_TQ, ATTN_TQ), ATTN_TQ, lambda h: (vt_ref[0, i - 1, h],))

    step(pl.multiple_of(i * ATTN_TQ, ATTN_TQ), ATTN_TQ, lambda h: (vt_ref[0, i, h],), key <= qry)

    for pair in range(ATTN_HEADS // 2):
        halves = []
        for sub in range(2):
            a = acc_ref[2 * pair + sub]
            halves.append(a[0:ATTN_HEAD_DIM, :] * (1.0 / a[ATTN_HEAD_DIM:ATTN_HEAD_DIM + 1, :]))
        o_ref[0, :, pair * LANES:(pair + 1) * LANES] = (
            jnp.concatenate(halves, axis=0).T.astype(BF16))


def _attention(qa, ka, vt):
    nb, _, lq, _ = qa.shape
    nq = lq // ATTN_TQ
    return pl.pallas_call(
        _attn_kernel,
        grid=(nb, nq),
        in_specs=[
            pl.BlockSpec((1, ATTN_HEADS, ATTN_TQ, LANES), lambda b, i: (b, 0, i, 0)),
            pl.BlockSpec((1, ATTN_HEADS, lq, LANES), lambda b, i: (b, 0, 0, 0)),
            pl.BlockSpec((1, nq, ATTN_HEADS, VT_ROWS, ATTN_TQ), lambda b, i: (b, 0, 0, 0, 0)),
        ],
        out_specs=pl.BlockSpec((1, ATTN_TQ, D_ATTN), lambda b, i: (b, i, 0)),
        out_shape=jax.ShapeDtypeStruct((nb, lq, D_ATTN), BF16),
        scratch_shapes=[pltpu.VMEM((ATTN_HEADS, VT_ROWS, ATTN_TQ), F32),
                        pltpu.VMEM((ATTN_HEADS, 2 * ATTN_TQ, ATTN_TQ), F32),
                        pltpu.VMEM((ATTN_HEADS, 1, ATTN_TQ), F32)],
        compiler_params=_params("parallel", "arbitrary"),
        name="fox_attention",
    )(qa, ka, vt)


def _expand_heads(v):
    r = v.shape[0]
    lane = lax.broadcasted_iota(jnp.int32, (r, LANES), 1)
    parts = []
    for k in range(SSD_HEADS // 2):
        l0 = MISC_DT0 + 2 * k
        a = jnp.broadcast_to(v[:, l0:l0 + 1], (r, LANES))
        b = jnp.broadcast_to(v[:, l0 + 1:l0 + 2], (r, LANES))
        parts.append(jnp.where(lane < SSD_HEAD_DIM, a, b))
    return jnp.concatenate(parts, axis=1)


def _pool_chunk(c, u_ref, pw_ref, ps_ref, yb_ref, buf_ref):
    @pl.when(c == 0)
    def _():
        buf_ref[0:POOL_HALO, :] = jnp.zeros((POOL_HALO, D_POOL), F32)

    @pl.when(c > 0)
    def _():
        buf_ref[0:POOL_HALO, :] = buf_ref[CHUNK:CHUNK + POOL_HALO, :]

    buf_ref[POOL_HALO:POOL_HALO + CHUNK, :] = u_ref[...]

    @pl.when(c == 0)
    def _():
        buf_ref[POOL_HALO:POOL_HALO + PAD, :] = jnp.zeros((PAD, D_POOL), F32)

    seen = c * CHUNK - PAD + 1 + lax.broadcasted_iota(jnp.int32, (CHUNK, POOL_GROUP_DIM), 0)
    for g, w in enumerate(POOL_WINDOWS):
        lo, hi = g * POOL_GROUP_DIM, (g + 1) * POOL_GROUP_DIM
        assert w & (w - 1) == 0 and w <= POOL_HALO
        win = buf_ref[:, lo:hi]
        s = 1
        while s < w:
            win = win + pltpu.roll(win, s, axis=0)
            s *= 2
        win = win[POOL_HALO:, :]
        u = buf_ref[POOL_HALO:POOL_HALO + CHUNK, lo:hi]
        cnt = jnp.clip(seen, 1, w).astype(F32)
        diff = (win / cnt - u).astype(BF16)
        mixed = jnp.dot(diff, pw_ref[g], preferred_element_type=F32)
        yb_ref[:, lo:hi] = (mixed * ps_ref[:, lo:hi]).astype(BF16)


def _seq_mix_kernel(xbc_ref, z_ref, misc_ref, u_ref, cw_ref, cb_ref, dtb_ref, a_ref, dsk_ref,
                    nw_ref, pw_ref, ps_ref, yc_ref, yb_ref, state_ref, ext_ref, buf_ref):
    c = pl.program_id(1)
    _pool_chunk(c, u_ref, pw_ref, ps_ref, yb_ref, buf_ref)

    @pl.when(c == 0)
    def _():
        state_ref[...] = jnp.zeros_like(state_ref)
        ext_ref[0:CONV_HALO, :] = jnp.zeros((CONV_HALO, D_CONV), F32)

    @pl.when(c > 0)
    def _():
        ext_ref[0:CONV_HALO, :] = ext_ref[CHUNK:CHUNK + CONV_HALO, :]

    ext_ref[CONV_HALO:CONV_HALO + CHUNK, :] = xbc_ref[...]

    @pl.when(c == 0)
    def _():
        ext_ref[CONV_HALO:CONV_HALO + PAD, :] = jnp.zeros((PAD, D_CONV), F32)

    conv = cb_ref[...]
    for k in range(CONV_K):
        off = CONV_HALO - (CONV_K - 1) + k
        conv = conv + cw_ref[k:k + 1, :] * ext_ref[off:off + CHUNK, :]
    xc = _silu(conv)
    xs = xc[:, :D_SSD]
    gn = SSD_GROUPS * SSD_STATE
    bm = xc[:, D_SSD:D_SSD + gn]
    cm = xc[:, D_SSD + gn:D_SSD + 2 * gn]

    row = lax.broadcasted_iota(jnp.int32, (CHUNK, CHUNK), 0)
    col = lax.broadcasted_iota(jnp.int32, (CHUNK, CHUNK), 1)
    dt = _softplus(misc_ref[...] + dtb_ref[...])
    dt = jnp.where(row >= jnp.where(c == 0, PAD, 0), dt, 0.0)
    a_cs = _cumsum_rows(dt * a_ref[...])
    a_cs_t = a_cs.T
    a_last = a_cs[CHUNK - 1:CHUNK, :]
    x_dt = xs * _expand_heads(dt)
    x_dt_b = x_dt.astype(BF16)
    decay_out = _expand_heads(jnp.exp2(a_cs))
    x_state = (x_dt * _expand_heads(jnp.exp2(a_last - a_cs))).astype(BF16)
    chunk_decay = _expand_heads(jnp.exp2(a_last))

    causal = col <= row
    hpg = SSD_HEADS // SSD_GROUPS
    gw = hpg * SSD_HEAD_DIM
    for g in range(SSD_GROUPS):
        n0, n1 = g * SSD_STATE, (g + 1) * SSD_STATE
        cm_g = cm[:, n0:n1].astype(BF16)
        bm_g = bm[:, n0:n1]
        cb = lax.dot_general(cm_g, bm_g.astype(BF16), (((1,), (1,)), ((), ())),
                             preferred_element_type=F32)
        st = state_ref[:, g * gw:(g + 1) * gw]
        y_off = jnp.dot(cm_g, st.astype(BF16), preferred_element_type=F32)
        y_g = y_off * decay_out[:, g * gw:(g + 1) * gw]
        diag = []
        for r in range(hpg):
            h = g * hpg + r
            lane = MISC_DT0 + h
            seg = a_cs[:, lane:lane + 1] - a_cs_t[lane:lane + 1, :]
            m = (cb * jnp.exp2(jnp.where(causal, seg, NEG_BIG))).astype(BF16)
            diag.append(jnp.dot(m, x_dt_b[:, h * SSD_HEAD_DIM:(h + 1) * SSD_HEAD_DIM],
                                preferred_element_type=F32))
        y_g = y_g + jnp.concatenate(diag, axis=1)
        new = jnp.dot(bm_g.T.astype(BF16), x_state[:, g * gw:(g + 1) * gw],
                      preferred_element_type=F32)
        state_ref[:, g * gw:(g + 1) * gw] = chunk_decay[:, g * gw:(g + 1) * gw] * st + new

        sl = slice(g * gw, (g + 1) * gw)
        y_g = y_g + xs[:, sl] * dsk_ref[:, sl]
        gy = y_g * _silu(z_ref[:, sl])
        ms = jnp.mean(gy * gy, axis=-1, keepdims=True)
        yc_ref[:, sl] = (gy * lax.rsqrt(ms + RMS_EPS) * nw_ref[:, sl]).astype(BF16)


def _seq_mix(rest, cw, cb, dtb, a_neg, dsk, nw, pw, ps, nb, nx):
    t = rest.shape[0]
    vec = lambda n: pl.BlockSpec((1, n), lambda b, c: (0, 0))
    blk = lambda b, c: _phys_block(b, c, nb, nx)
    rows = lambda n, col0: pl.BlockSpec((CHUNK, n), lambda b, c: (blk(b, c), col0 // n))
    return pl.pallas_call(
        _seq_mix_kernel,
        grid=(nb, nx + 1),
        in_specs=[
            rows(D_CONV, REST_XBC), rows(D_SSD, REST_Z), rows(LANES, REST_MISC),
            rows(D_POOL, REST_POOL),
            pl.BlockSpec((CONV_K, D_CONV), lambda b, c: (0, 0)),
            vec(D_CONV), vec(LANES), vec(LANES), vec(D_SSD), vec(D_SSD),
            pl.BlockSpec((POOL_GROUPS, POOL_GROUP_DIM, POOL_GROUP_DIM), lambda b, c: (0, 0, 0)),
            vec(D_POOL),
        ],
        out_specs=[rows(D_SSD, 0), rows(D_POOL, 0)],
        out_shape=[jax.ShapeDtypeStruct((t, D_SSD), BF16),
                   jax.ShapeDtypeStruct((t, D_POOL), BF16)],
        scratch_shapes=[pltpu.VMEM((SSD_STATE, D_SSD), F32),
                        pltpu.VMEM((CONV_HALO + CHUNK, D_CONV), F32),
                        pltpu.VMEM((POOL_HALO + CHUNK, D_POOL), F32)],
        compiler_params=_params("parallel", "arbitrary"),
        name="seq_mix",
    )(rest, rest, rest, rest, cw, cb, dtb, a_neg, dsk, nw, pw, ps)


def _out_ln_kernel(ya_ref, yb_ref, yc_ref, h_ref, w_ref, g_ref, b_ref, o_ref):
    tm = o_ref.shape[0]
    for r in range(0, tm, tm // 2):
        rows = slice(r, r + tm // 2)
        acc = jnp.dot(ya_ref[rows, :], w_ref[0:D_ATTN, :], preferred_element_type=F32)
        acc += jnp.dot(yb_ref[rows, :], w_ref[D_ATTN:D_ATTN + D_POOL, :],
                       preferred_element_type=F32)
        acc += jnp.dot(yc_ref[rows, :], w_ref[D_ATTN + D_POOL:, :], preferred_element_type=F32)
        o_ref[rows, :] = _layer_norm(ALPHA * h_ref[rows, :] + acc, g_ref[...], b_ref[...])


def _out_ln(ya, yb, yc, h, w, g, b, layer):
    t, d = h.shape
    tm = _pick_tile(ROW_TILES, t)
    rows = lambda n: pl.BlockSpec((tm, n), lambda i: (i, 0))
    return pl.pallas_call(
        _out_ln_kernel,
        grid=(t // tm,),
        in_specs=[
            rows(D_ATTN), rows(D_POOL), rows(D_SSD), rows(d),
            pl.BlockSpec((None, d, d), lambda i: (layer, 0, 0)),
            pl.BlockSpec((1, d), lambda i: (0, 0)),
            pl.BlockSpec((1, d), lambda i: (0, 0)),
        ],
        out_specs=rows(d),
        out_shape=jax.ShapeDtypeStruct((t, d), F32),
        compiler_params=_params("parallel"),
        name="out_proj_ln",
    )(ya, yb, yc, h, w, g, b)


def _lane_row(vals, offset):
    return jnp.zeros((1, LANES), F32).at[0, offset:offset + vals.shape[0]].set(vals.astype(F32))


def _proj_weight(w_in):
    wt = jnp.swapaxes(w_in, 1, 2)
    c = D_ATTN
    o = 3 * c + ATTN_HEADS
    f = wt[:, 3 * c:o]
    dt = wt[:, o + D_POOL + D_SSD + D_CONV:]
    pad = jnp.zeros((wt.shape[0], LANES - ATTN_HEADS - SSD_HEADS, wt.shape[2]), wt.dtype)
    segments = (wt[:, :c] * Q_SCALE,
                wt[:, c:3 * c],
                wt[:, o + D_POOL + D_SSD:o + D_POOL + D_SSD + D_CONV],
                wt[:, o:o + D_POOL],
                wt[:, o + D_POOL:o + D_POOL + D_SSD],
                jnp.concatenate([f, dt, pad], axis=1))
    return tuple(w.astype(BF16) for w in segments)


def _mixer(h, nb, nx, layer, w_proj, b_fgate, pool_w, pool_scale, conv_w, conv_b, dt_bias,
           a_log, d_skip, ssd_norm_w, w_out, ln_g, ln_b):
    qkv, rest = _proj(h, w_proj, layer)
    qa, ka, vt = _gate_prep(rest, qkv, _lane_row(b_fgate, MISC_F0), nb, nx)
    ya = _attention(qa, ka, vt)
    ya = jnp.concatenate([ya[:, CHUNK:(nx + 1) * CHUNK].reshape(nb * nx * CHUNK, D_ATTN),
                          ya[:, :CHUNK].reshape(nb * CHUNK, D_ATTN)], axis=0)
    yc, yb = _seq_mix(rest, conv_w, conv_b.reshape(1, D_CONV), _lane_row(dt_bias, MISC_DT0),
                      _lane_row(-LOG2_E * jnp.exp(a_log.astype(F32)), MISC_DT0),
                      jnp.repeat(d_skip, SSD_HEAD_DIM).reshape(1, D_SSD),
                      ssd_norm_w.reshape(1, D_SSD), pool_w.astype(BF16),
                      pool_scale.reshape(1, D_POOL), nb, nx)
    return _out_ln(ya, yb, yc, h, w_out, ln_g, ln_b, layer)


def kernel(x, meta, f1_gate, f1_up, f1_down, ln1_g, ln1_b, w_in, b_fgate, pool_w, pool_scale,
           conv_w, conv_b, dt_bias, a_log, d_skip, ssd_norm_w, w_out, ln2_g, ln2_b, f2_gate,
           f2_up, f2_down, ln3_g, ln3_b):
    nb, seq, d = x.shape
    assert d == D_MODEL and meta.shape == (N_META, D_MODEL) and seq % CHUNK == 0
    nx = seq // CHUNK
    depth = f1_gate.shape[0]
    head = jnp.concatenate([jnp.zeros((PAD, d), x.dtype), meta.astype(x.dtype)], axis=0)
    heads = jnp.broadcast_to(head[None], (nb, CHUNK, d)).reshape(nb * CHUNK, d)
    row = lambda v: v.reshape(1, d)
    bf = lambda w: w.astype(BF16)
    f1 = (bf(f1_gate), bf(f1_up), bf(f1_down))
    f2 = (bf(f2_gate), bf(f2_up), bf(f2_down))
    w_proj, w_o = _proj_weight(w_in), bf(w_out)
    h = x.reshape(nb * seq, d)
    for i in range(depth):
        h = _ffn_ln(h, *f1, row(ln1_g[i]), row(ln1_b[i]), i, tail=heads if i == 0 else None)
        h = _mixer(h, nb, nx, i, w_proj, b_fgate[i], pool_w[i], pool_scale[i], conv_w[i],
                   conv_b[i], dt_bias[i], a_log[i], d_skip[i], ssd_norm_w[i], w_o,
                   row(ln2_g[i]), row(ln2_b[i]))
        h = _ffn_ln(h, *f2, row(ln3_g[i]), row(ln3_b[i]), i,
                    rows=nb * seq if i == depth - 1 else None)
    return h.reshape(nb, seq, d)
```

```python
import functools

import jax
import jax.numpy as jnp
from jax import lax
from jax.experimental import pallas as pl
from jax.experimental.pallas import tpu as pltpu

F32 = jnp.float32
BF16 = jnp.bfloat16

D_MODEL = 2048
N_META = 16
CHUNK = 128
PAD = CHUNK - N_META
LANES = 128

ATTN_HEADS = 8
ATTN_HEAD_DIM = 64
D_ATTN = ATTN_HEADS * ATTN_HEAD_DIM
POOL_WINDOWS = (2, 4, 8, 16)
POOL_GROUPS = 4
D_POOL = 512
POOL_GROUP_DIM = D_POOL // POOL_GROUPS
POOL_HALO = 16
D_SSD = 1024
SSD_HEAD_DIM = 64
SSD_HEADS = D_SSD // SSD_HEAD_DIM
SSD_GROUPS = 2
SSD_STATE = 128
CONV_K = 4
CONV_HALO = 8
D_CONV = D_SSD + 2 * SSD_GROUPS * SSD_STATE
D_FF_TILE = 512
ROW_TILES = (512, 384, 256, 128)
DEPTH = 2
ALPHA = (2 * DEPTH) ** 0.25
LN_EPS = 1e-5
RMS_EPS = 1e-5
NEG_BIG = -1e30

PROJ_TILE = 512
REST_XBC, REST_POOL, REST_Z, REST_MISC = 0, D_CONV, D_CONV + D_POOL, D_CONV + D_POOL + D_SSD
D_REST = REST_MISC + LANES
MISC_F0 = 0
MISC_DT0 = ATTN_HEADS

VMEM_LIMIT = 56 * 1024 * 1024


def _params(*sem):
    return pltpu.CompilerParams(dimension_semantics=sem, vmem_limit_bytes=VMEM_LIMIT)


def _pick_tile(candidates, *sizes):
    for c in candidates:
        if all(n % c == 0 for n in sizes):
            return c
    raise ValueError(f"no tile in {candidates} divides {sizes}")


def _phys_block(b, j, nb, nx):
    return jnp.where(j == 0, nb * nx + b, b * nx + j - 1)


def _layer_norm(y, g, b):
    mu = jnp.mean(y, axis=-1, keepdims=True)
    yc = y - mu
    var = jnp.mean(yc * yc, axis=-1, keepdims=True)
    return yc * lax.rsqrt(var + LN_EPS) * g + b


def _silu(x):
    half = 0.5 * x
    return half + half * jnp.tanh(half)


def _softplus(x):
    return jnp.maximum(x, 0.0) + jnp.log1p(jnp.exp(-jnp.abs(x)))


def _cumsum_rows(x):
    n = x.shape[0]
    row = lax.broadcasted_iota(jnp.int32, x.shape, 0)
    d = 1
    while d < n:
        x = x + jnp.where(row >= d, pltpu.roll(x, d, axis=0), 0.0)
        d *= 2
    return x


def _ffn_ln_kernel(*refs, n_main_tiles):
    if n_main_tiles is None:
        x_ref, wg_ref, wu_ref, wd_ref, g_ref, b_ref, o_ref, xb_ref, acc_ref, y_ref = refs
        read_x = lambda: x_ref[...]
    else:
        x_ref, tail_ref, wg_ref, wu_ref, wd_ref, g_ref, b_ref, o_ref, xb_ref, acc_ref, y_ref = refs
        in_main = pl.program_id(0) < n_main_tiles
        read_x = lambda: jnp.where(in_main, x_ref[...], tail_ref[...])
    i, f = pl.program_id(0), pl.program_id(1)
    n_tiles, last = pl.num_programs(0) - 1, pl.num_programs(1) - 1

    def down_proj(xb):
        gate = jnp.dot(xb, wg_ref[...], preferred_element_type=F32)
        up = jnp.dot(xb, wu_ref[...], preferred_element_type=F32)
        act = (_silu(gate) * up).astype(BF16)
        return jnp.dot(act, wd_ref[...], preferred_element_type=F32)

    def normalize_previous():
        o_ref[...] = _layer_norm(y_ref[...], g_ref[...], b_ref[...])

    @pl.when((i == 0) & (f == 0))
    def _():
        y_ref[...] = jnp.zeros_like(y_ref)

    @pl.when((f == 0) & (i < n_tiles))
    def _():
        normalize_previous()
        xb = read_x().astype(BF16)
        xb_ref[...] = xb
        acc_ref[...] = down_proj(xb)

    @pl.when((f == 0) & (i == n_tiles))
    def _():
        normalize_previous()

    @pl.when((f > 0) & (f < last) & (i < n_tiles))
    def _():
        acc_ref[...] += down_proj(xb_ref[...])

    @pl.when((f == last) & (i < n_tiles))
    def _():
        y_ref[...] = ALPHA * read_x() + 0.5 * (acc_ref[...] + down_proj(xb_ref[...]))


def _ffn_ln(h, wg, wu, wd, g, b, layer, *, tail=None, rows=None):
    d = h.shape[1]
    ff = wg.shape[2]
    t = h.shape[0] + (0 if tail is None else tail.shape[0])
    rows = t if rows is None else rows
    tm = _pick_tile(ROW_TILES, rows, *((h.shape[0], tail.shape[0]) if tail is not None else ()))
    tf = _pick_tile((D_FF_TILE, 256, 128), ff)
    n_tiles, nf = rows // tm, ff // tf
    assert nf >= 2
    tile = lambda i: jnp.minimum(i, n_tiles - 1)
    col = lambda i, f: jnp.where(i == n_tiles, nf - 1, f)
    x_specs = [pl.BlockSpec((tm, d), lambda i, f: (tile(i), 0))]
    operands = [h]
    n_main = None
    if tail is not None:
        n_main = h.shape[0] // tm
        x_specs = [pl.BlockSpec((tm, d), lambda i, f: (jnp.minimum(i, n_main - 1), 0)),
                   pl.BlockSpec((tm, d), lambda i, f: (jnp.maximum(tile(i) - n_main, 0), 0))]
        operands = [h, tail]
    return pl.pallas_call(
        functools.partial(_ffn_ln_kernel, n_main_tiles=n_main),
        grid=(n_tiles + 1, nf),
        in_specs=x_specs + [
            pl.BlockSpec((None, d, tf), lambda i, f: (layer, 0, col(i, f))),
            pl.BlockSpec((None, d, tf), lambda i, f: (layer, 0, col(i, f))),
            pl.BlockSpec((None, tf, d), lambda i, f: (layer, col(i, f), 0)),
            pl.BlockSpec((1, d), lambda i, f: (0, 0)),
            pl.BlockSpec((1, d), lambda i, f: (0, 0)),
        ],
        out_specs=pl.BlockSpec((tm, d), lambda i, f: (jnp.maximum(i - 1, 0), 0)),
        out_shape=jax.ShapeDtypeStruct((rows, d), F32),
        scratch_shapes=[pltpu.VMEM((tm, d), BF16), pltpu.VMEM((tm, d), F32),
                        pltpu.VMEM((tm, d), F32)],
        compiler_params=_params("arbitrary", "arbitrary"),
        name="ffn_ln",
    )(*operands, wg, wu, wd, g, b)


LOG2_E = 1.4426950408889634
Q_SCALE = LOG2_E * ATTN_HEAD_DIM ** -0.5
AUG_Q = ATTN_HEAD_DIM
VT_ROWS = ATTN_HEAD_DIM + 16
ATTN_TQ = 2 * CHUNK
ATTN_QK_LEAD = ATTN_HEADS
NT_DIMS = (((1,), (1,)), ((), ()))


def _split3(c):
    hi = c.astype(BF16).astype(F32)
    mid = (c - hi).astype(BF16).astype(F32)
    lo = (c - hi - mid).astype(BF16).astype(F32)
    return hi, mid, lo


def _attn_operands(misc, q, k, v, bias, carry, n_pad):
    row = lax.broadcasted_iota(jnp.int32, (CHUNK, LANES), 0)
    lane = lax.broadcasted_iota(jnp.int32, (CHUNK, LANES), 1)
    is_pad = row < n_pad
    x = misc + bias
    log_f = jnp.minimum(x, 0.0) - jnp.log1p(jnp.exp(-jnp.abs(x)))
    c = _cumsum_rows(jnp.where(is_pad, 0.0, log_f)) + carry
    parts = _split3(c * LOG2_E)
    k_pad = jnp.where(lane == AUG_Q + 3, NEG_BIG, 0.0)
    q_aug, k_aug, v_t = [], [], []
    for pair in range(ATTN_HEADS // 2):
        q2 = q[:, pair * LANES:(pair + 1) * LANES]
        k2 = k[:, pair * LANES:(pair + 1) * LANES]
        v2t = v[:, pair * LANES:(pair + 1) * LANES].T
        for sub in range(2):
            h = 2 * pair + sub
            qh = q2 if sub == 0 else pltpu.roll(q2, ATTN_HEAD_DIM, axis=1)
            kh = k2 if sub == 0 else pltpu.roll(k2, ATTN_HEAD_DIM, axis=1)
            q_extra = jnp.where((lane >= AUG_Q + 3) & (lane < AUG_Q + 6), 1.0, 0.0)
            k_extra = jnp.where((lane >= AUG_Q) & (lane < AUG_Q + 3), 1.0, 0.0)
            for n, part in enumerate(parts):
                col = jnp.broadcast_to(part[:, h:h + 1], (CHUNK, LANES))
                q_extra = jnp.where(lane == AUG_Q + n, col, q_extra)
                k_extra = jnp.where(lane == AUG_Q + 3 + n, -col, k_extra)
            q_aug.append(jnp.where(lane < AUG_Q, qh, q_extra).astype(BF16))
            k_aug.append(jnp.where(is_pad, k_pad, jnp.where(lane < AUG_Q, kh, k_extra)).astype(BF16))
            v_t.append(v2t[sub * ATTN_HEAD_DIM:(sub + 1) * ATTN_HEAD_DIM, :].astype(BF16))
    return q_aug, k_aug, v_t, c[CHUNK - 1:CHUNK, :]


def _proj_kernel(x_ref, wq_ref, wkv_ref, wxbc_ref, wpool_ref, wz_ref, wmisc_ref, bf_ref,
                 rest_ref, qa_ref, ka_ref, vt_ref, qah_ref, kah_ref, vth_ref,
                 carry_ref, head_carry_ref, *, tiles_per_batch):
    g = pl.program_id(0)
    is_head = g == 0
    blocks = x_ref.shape[0] // CHUNK

    @pl.when(is_head)
    def _():
        carry_ref[...] = jnp.zeros_like(carry_ref)
        head_carry_ref[...] = jnp.zeros_like(head_carry_ref)

    xb = x_ref[...].astype(BF16)
    proj = lambda w: lax.dot_general(xb, w, NT_DIMS, preferred_element_type=F32)
    q = proj(wq_ref[...])
    k = proj(wkv_ref[0:D_ATTN, :])
    v = proj(wkv_ref[D_ATTN:2 * D_ATTN, :])
    misc = proj(wmisc_ref[...])
    rest_ref[:, REST_MISC:REST_MISC + LANES] = misc

    tile = jnp.maximum(g - 1, 0)
    first_of_row = tile % tiles_per_batch == 0
    carry = jnp.where(first_of_row, head_carry_ref[pl.ds(tile // tiles_per_batch, 1), :],
                      carry_ref[...])
    n_pad = jnp.where(is_head, PAD, 0)
    ones_row = jnp.where(lax.broadcasted_iota(jnp.int32, (VT_ROWS - ATTN_HEAD_DIM, CHUNK), 0) == 0,
                         1.0, 0.0).astype(BF16)
    block_sums = []
    for r in range(blocks):
        rows = slice(r * CHUNK, (r + 1) * CHUNK)
        half = slice((r % 2) * CHUNK, (r % 2 + 1) * CHUNK)
        q_aug, k_aug, v_t, carry = _attn_operands(
            misc[rows], q[rows], k[rows], v[rows], bf_ref[...],
            jnp.where(is_head, 0.0, carry), n_pad)
        block_sums.append(carry)
        for h in range(ATTN_HEADS):
            qa_ref[0, h, rows, :] = q_aug[h]
            ka_ref[0, h, rows, :] = k_aug[h]
            vt_ref[0, r // 2, h, 0:ATTN_HEAD_DIM, half] = v_t[h]
            vt_ref[0, r // 2, h, ATTN_HEAD_DIM:VT_ROWS, half] = ones_row
    carry_ref[...] = carry

    for w_ref, col0 in ((wxbc_ref, REST_XBC), (wpool_ref, REST_POOL), (wz_ref, REST_Z)):
        for c0 in range(0, w_ref.shape[0], PROJ_TILE):
            rest_ref[:, col0 + c0:col0 + c0 + PROJ_TILE] = proj(w_ref[c0:c0 + PROJ_TILE, :])

    @pl.when(is_head)
    def _():
        for r in range(blocks):
            rows = slice(r * CHUNK, (r + 1) * CHUNK)
            half = slice((r % 2) * CHUNK, (r % 2 + 1) * CHUNK)
            head_carry_ref[r:r + 1, :] = block_sums[r]
            for h in range(ATTN_HEADS):
                qah_ref[r, h] = qa_ref[0, h, rows, :]
                kah_ref[r, h] = ka_ref[0, h, rows, :]
                vth_ref[r, h] = vt_ref[0, r // 2, h, :, half]


def _proj(h, ws, bias, layer, nb, nx):
    t, d = h.shape
    seq = nx * CHUNK
    tm = nb * CHUNK
    assert seq % tm == 0 and tm % ATTN_TQ == 0 and t == nb * seq + tm
    tpb = seq // tm
    n_main = nb * tpb
    x_tile = lambda g: jnp.maximum(g - 1, 0)
    resident = lambda w: pl.BlockSpec((None, w.shape[1], d), lambda g: (layer, 0, 0),
                                      pipeline_mode=pl.Buffered(1))
    whole = lambda shape: pl.BlockSpec(shape, lambda g: (0,) * len(shape))
    rows_of = lambda g: jnp.where(g == 0, n_main, g - 1)
    head_qk = (nb, ATTN_HEADS, CHUNK, LANES)
    head_vt = (nb, ATTN_HEADS, VT_ROWS, CHUNK)
    return pl.pallas_call(
        functools.partial(_proj_kernel, tiles_per_batch=tpb),
        grid=(n_main + 1,),
        in_specs=[pl.BlockSpec((tm, d), lambda g: (rows_of(g), 0))] + [resident(w) for w in ws]
        + [pl.BlockSpec((1, LANES), lambda g: (0, 0))],
        out_specs=[
            pl.BlockSpec((tm, D_REST), lambda g: (rows_of(g), 0)),
            pl.BlockSpec((1, ATTN_HEADS, tm, LANES),
                         lambda g: (x_tile(g) // tpb, 0, x_tile(g) % tpb, 0)),
            pl.BlockSpec((1, ATTN_HEADS, tm, LANES),
                         lambda g: (x_tile(g) // tpb, 0, x_tile(g) % tpb, 0)),
            pl.BlockSpec((1, tm // ATTN_TQ, ATTN_HEADS, VT_ROWS, ATTN_TQ),
                         lambda g: (x_tile(g) // tpb, x_tile(g) % tpb, 0, 0, 0)),
            whole(head_qk), whole(head_qk), whole(head_vt),
        ],
        out_shape=[
            jax.ShapeDtypeStruct((t, D_REST), F32),
            jax.ShapeDtypeStruct((nb, ATTN_HEADS, seq, LANES), BF16),
            jax.ShapeDtypeStruct((nb, ATTN_HEADS, seq, LANES), BF16),
            jax.ShapeDtypeStruct((nb, seq // ATTN_TQ, ATTN_HEADS, VT_ROWS, ATTN_TQ), BF16),
            jax.ShapeDtypeStruct(head_qk, BF16),
            jax.ShapeDtypeStruct(head_qk, BF16),
            jax.ShapeDtypeStruct(head_vt, BF16),
        ],
        scratch_shapes=[pltpu.VMEM((1, LANES), F32), pltpu.VMEM((nb, LANES), F32)],
        compiler_params=_params("arbitrary"),
        name="in_proj",
    )(h, *ws, bias)


def _softmax_step(st, m_old, mask):
    if mask is not None:
        st = jnp.where(mask, st, NEG_BIG)
    m_new = jnp.maximum(m_old, jnp.max(st, axis=0, keepdims=True))
    return m_new, jnp.exp2(m_old - m_new), jnp.exp2(st - m_new).astype(BF16)


def _write_heads(o_ref, accs):
    for pair in range(ATTN_HEADS // 2):
        halves = []
        for a in accs[2 * pair:2 * pair + 2]:
            halves.append(a[0:ATTN_HEAD_DIM, :] * (1.0 / a[ATTN_HEAD_DIM:ATTN_HEAD_DIM + 1, :]))
        o_ref[:, pair * LANES:(pair + 1) * LANES] = jnp.concatenate(halves, axis=0).T.astype(BF16)


def _attn_kernel(qa_ref, ka_ref, vt_ref, kah_ref, vth_ref, o_ref, acc_ref, st_ref, m_ref):
    i = pl.program_id(1)
    key = lax.broadcasted_iota(jnp.int32, (ATTN_TQ, ATTN_TQ), 0)
    qry = lax.broadcasted_iota(jnp.int32, (ATTN_TQ, ATTN_TQ), 1)
    acc_ref[...] = jnp.zeros_like(acc_ref)
    m_ref[...] = jnp.full(m_ref.shape, NEG_BIG, F32)

    def step(load_keys, nkeys, vt_chunks, mask=None):
        def qk(h):
            st_ref[h, 0:nkeys, :] = lax.dot_general(load_keys(h), qa_ref[0, h], NT_DIMS,
                                                    preferred_element_type=F32)

        def softmax_pv(h):
            m_new, rescale, p = _softmax_step(st_ref[h, 0:nkeys, :], m_ref[h], mask)
            m_ref[h] = m_new
            acc = rescale * acc_ref[h]
            r = 0
            for vt in vt_chunks(h):
                n = vt.shape[1]
                acc += jnp.dot(vt, p[r:r + n, :], preferred_element_type=F32)
                r += n
            acc_ref[h] = acc

        for h in range(ATTN_HEADS + ATTN_QK_LEAD):
            if h < ATTN_HEADS:
                qk(h)
            if h >= ATTN_QK_LEAD:
                softmax_pv(h - ATTN_QK_LEAD)

    def x_keys(start, nkeys):
        return lambda h: ka_ref[0, h, pl.ds(start, nkeys), :]

    step(lambda h: kah_ref[0, h], CHUNK, lambda h: (vth_ref[0, h],))

    def quad_step(s, _):
        step(x_keys(pl.multiple_of(s * 2 * ATTN_TQ, 2 * ATTN_TQ), 2 * ATTN_TQ), 2 * ATTN_TQ,
             lambda h: (vt_ref[0, 2 * s, h], vt_ref[0, 2 * s + 1, h]))
        return 0

    lax.fori_loop(0, i // 2, quad_step, 0)

    @pl.when(i % 2 == 1)
    def _():
        step(x_keys(pl.multiple_of((i - 1) * ATTN_TQ, ATTN_TQ), ATTN_TQ), ATTN_TQ,
             lambda h: (vt_ref[0, i - 1, h],))

    step(x_keys(pl.multiple_of(i * ATTN_TQ, ATTN_TQ), ATTN_TQ), ATTN_TQ,
         lambda h: (vt_ref[0, i, h],), key <= qry)
    _write_heads(o_ref.at[0], [acc_ref[h] for h in range(ATTN_HEADS)])


def _attention(qa, ka, vt, ka_h, vt_h):
    nb, _, seq, _ = qa.shape
    nq = seq // ATTN_TQ
    per_row = lambda shape: pl.BlockSpec((1,) + shape, lambda b, i: (b,) + (0,) * len(shape))
    return pl.pallas_call(
        _attn_kernel,
        grid=(nb, nq),
        in_specs=[
            pl.BlockSpec((1, ATTN_HEADS, ATTN_TQ, LANES), lambda b, i: (b, 0, i, 0)),
            per_row((ATTN_HEADS, seq, LANES)),
            per_row((nq, ATTN_HEADS, VT_ROWS, ATTN_TQ)),
            per_row((ATTN_HEADS, CHUNK, LANES)),
            per_row((ATTN_HEADS, VT_ROWS, CHUNK)),
        ],
        out_specs=pl.BlockSpec((1, ATTN_TQ, D_ATTN), lambda b, i: (b, i, 0)),
        out_shape=jax.ShapeDtypeStruct((nb, seq, D_ATTN), BF16),
        scratch_shapes=[pltpu.VMEM((ATTN_HEADS, VT_ROWS, ATTN_TQ), F32),
                        pltpu.VMEM((ATTN_HEADS, 2 * ATTN_TQ, ATTN_TQ), F32),
                        pltpu.VMEM((ATTN_HEADS, 1, ATTN_TQ), F32)],
        compiler_params=_params("parallel", "arbitrary"),
        name="fox_attention",
    )(qa, ka, vt, ka_h, vt_h)


def _head_attn_kernel(qah_ref, kah_ref, vth_ref, o_ref):
    key = lax.broadcasted_iota(jnp.int32, (CHUNK, CHUNK), 0)
    qry = lax.broadcasted_iota(jnp.int32, (CHUNK, CHUNK), 1)
    accs = []
    for h in range(ATTN_HEADS):
        st = lax.dot_general(kah_ref[0, h], qah_ref[0, h], NT_DIMS, preferred_element_type=F32)
        _, _, p = _softmax_step(st, jnp.full((1, CHUNK), NEG_BIG, F32), key <= qry)
        accs.append(jnp.dot(vth_ref[0, h], p, preferred_element_type=F32))
    _write_heads(o_ref, accs)


def _head_attention(qa_h, ka_h, vt_h):
    nb = qa_h.shape[0]
    per_row = lambda shape: pl.BlockSpec((1,) + shape, lambda b: (b,) + (0,) * len(shape))
    return pl.pallas_call(
        _head_attn_kernel,
        grid=(nb,),
        in_specs=[per_row((ATTN_HEADS, CHUNK, LANES)), per_row((ATTN_HEADS, CHUNK, LANES)),
                  per_row((ATTN_HEADS, VT_ROWS, CHUNK))],
        out_specs=pl.BlockSpec((CHUNK, D_ATTN), lambda b: (b, 0)),
        out_shape=jax.ShapeDtypeStruct((nb * CHUNK, D_ATTN), BF16),
        compiler_params=_params("parallel"),
        name="head_attention",
    )(qa_h, ka_h, vt_h)


def _expand_heads(v):
    r = v.shape[0]
    lane = lax.broadcasted_iota(jnp.int32, (r, LANES), 1)
    parts = []
    for k in range(SSD_HEADS // 2):
        l0 = MISC_DT0 + 2 * k
        a = jnp.broadcast_to(v[:, l0:l0 + 1], (r, LANES))
        b = jnp.broadcast_to(v[:, l0 + 1:l0 + 2], (r, LANES))
        parts.append(jnp.where(lane < SSD_HEAD_DIM, a, b))
    return jnp.concatenate(parts, axis=1)


def _pool_chunk(c, u_ref, pw_ref, ps_ref, yb_ref, buf_ref):
    @pl.when(c == 0)
    def _():
        buf_ref[0:POOL_HALO, :] = jnp.zeros((POOL_HALO, D_POOL), F32)

    @pl.when(c > 0)
    def _():
        buf_ref[0:POOL_HALO, :] = buf_ref[CHUNK:CHUNK + POOL_HALO, :]

    buf_ref[POOL_HALO:POOL_HALO + CHUNK, :] = u_ref[...]

    @pl.when(c == 0)
    def _():
        buf_ref[POOL_HALO:POOL_HALO + PAD, :] = jnp.zeros((PAD, D_POOL), F32)

    seen = c * CHUNK - PAD + 1 + lax.broadcasted_iota(jnp.int32, (CHUNK, POOL_GROUP_DIM), 0)
    for g, w in enumerate(POOL_WINDOWS):
        lo, hi = g * POOL_GROUP_DIM, (g + 1) * POOL_GROUP_DIM
        assert w & (w - 1) == 0 and w <= POOL_HALO
        win = buf_ref[:, lo:hi]
        s = 1
        while s < w:
            win = win + pltpu.roll(win, s, axis=0)
            s *= 2
        win = win[POOL_HALO:, :]
        u = buf_ref[POOL_HALO:POOL_HALO + CHUNK, lo:hi]
        cnt = jnp.clip(seen, 1, w).astype(F32)
        diff = (win / cnt - u).astype(BF16)
        mixed = jnp.dot(diff, pw_ref[g], preferred_element_type=F32)
        yb_ref[:, lo:hi] = (mixed * ps_ref[:, lo:hi]).astype(BF16)


def _seq_mix_kernel(xbc_ref, z_ref, misc_ref, u_ref, cw_ref, cb_ref, dtb_ref, a_ref, dsk_ref,
                    nw_ref, pw_ref, ps_ref, yc_ref, yb_ref, state_ref, ext_ref, buf_ref):
    c = pl.program_id(1)
    _pool_chunk(c, u_ref, pw_ref, ps_ref, yb_ref, buf_ref)

    @pl.when(c == 0)
    def _():
        state_ref[...] = jnp.zeros_like(state_ref)
        ext_ref[0:CONV_HALO, :] = jnp.zeros((CONV_HALO, D_CONV), F32)

    @pl.when(c > 0)
    def _():
        ext_ref[0:CONV_HALO, :] = ext_ref[CHUNK:CHUNK + CONV_HALO, :]

    ext_ref[CONV_HALO:CONV_HALO + CHUNK, :] = xbc_ref[...]

    @pl.when(c == 0)
    def _():
        ext_ref[CONV_HALO:CONV_HALO + PAD, :] = jnp.zeros((PAD, D_CONV), F32)

    conv = cb_ref[...]
    for k in range(CONV_K):
        off = CONV_HALO - (CONV_K - 1) + k
        conv = conv + cw_ref[k:k + 1, :] * ext_ref[off:off + CHUNK, :]
    xc = _silu(conv)
    xs = xc[:, :D_SSD]
    gn = SSD_GROUPS * SSD_STATE
    bm = xc[:, D_SSD:D_SSD + gn]
    cm = xc[:, D_SSD + gn:D_SSD + 2 * gn]

    row = lax.broadcasted_iota(jnp.int32, (CHUNK, CHUNK), 0)
    col = lax.broadcasted_iota(jnp.int32, (CHUNK, CHUNK), 1)
    dt = _softplus(misc_ref[...] + dtb_ref[...])
    dt = jnp.where(row >= jnp.where(c == 0, PAD, 0), dt, 0.0)
    a_cs = _cumsum_rows(dt * a_ref[...])
    a_cs_t = a_cs.T
    a_last = a_cs[CHUNK - 1:CHUNK, :]
    x_dt = xs * _expand_heads(dt)
    x_dt_b = x_dt.astype(BF16)
    decay_out = _expand_heads(jnp.exp2(a_cs))
    x_state = (x_dt * _expand_heads(jnp.exp2(a_last - a_cs))).astype(BF16)
    chunk_decay = _expand_heads(jnp.exp2(a_last))

    causal = col <= row
    hpg = SSD_HEADS // SSD_GROUPS
    gw = hpg * SSD_HEAD_DIM
    for g in range(SSD_GROUPS):
        n0, n1 = g * SSD_STATE, (g + 1) * SSD_STATE
        cm_g = cm[:, n0:n1].astype(BF16)
        bm_g = bm[:, n0:n1]
        cb = lax.dot_general(cm_g, bm_g.astype(BF16), NT_DIMS, preferred_element_type=F32)
        st = state_ref[:, g * gw:(g + 1) * gw]
        y_off = jnp.dot(cm_g, st.astype(BF16), preferred_element_type=F32)
        y_g = y_off * decay_out[:, g * gw:(g + 1) * gw]
        diag = []
        for r in range(hpg):
            h = g * hpg + r
            lane = MISC_DT0 + h
            seg = a_cs[:, lane:lane + 1] - a_cs_t[lane:lane + 1, :]
            m = (cb * jnp.exp2(jnp.where(causal, seg, NEG_BIG))).astype(BF16)
            diag.append(jnp.dot(m, x_dt_b[:, h * SSD_HEAD_DIM:(h + 1) * SSD_HEAD_DIM],
                                preferred_element_type=F32))
        y_g = y_g + jnp.concatenate(diag, axis=1)
        new = jnp.dot(bm_g.T.astype(BF16), x_state[:, g * gw:(g + 1) * gw],
                      preferred_element_type=F32)
        state_ref[:, g * gw:(g + 1) * gw] = chunk_decay[:, g * gw:(g + 1) * gw] * st + new

        sl = slice(g * gw, (g + 1) * gw)
        y_g = y_g + xs[:, sl] * dsk_ref[:, sl]
        gy = y_g * _silu(z_ref[:, sl])
        ms = jnp.mean(gy * gy, axis=-1, keepdims=True)
        yc_ref[:, sl] = (gy * lax.rsqrt(ms + RMS_EPS) * nw_ref[:, sl]).astype(BF16)


def _seq_mix(rest, cw, cb, dtb, a_neg, dsk, nw, pw, ps, nb, nx):
    t = rest.shape[0]
    vec = lambda n: pl.BlockSpec((1, n), lambda b, c: (0, 0))
    blk = lambda b, c: _phys_block(b, c, nb, nx)
    rows = lambda n, col0: pl.BlockSpec((CHUNK, n), lambda b, c: (blk(b, c), col0 // n))
    return pl.pallas_call(
        _seq_mix_kernel,
        grid=(nb, nx + 1),
        in_specs=[
            rows(D_CONV, REST_XBC), rows(D_SSD, REST_Z), rows(LANES, REST_MISC),
            rows(D_POOL, REST_POOL),
            pl.BlockSpec((CONV_K, D_CONV), lambda b, c: (0, 0)),
            vec(D_CONV), vec(LANES), vec(LANES), vec(D_SSD), vec(D_SSD),
            pl.BlockSpec((POOL_GROUPS, POOL_GROUP_DIM, POOL_GROUP_DIM), lambda b, c: (0, 0, 0)),
            vec(D_POOL),
        ],
        out_specs=[rows(D_SSD, 0), rows(D_POOL, 0)],
        out_shape=[jax.ShapeDtypeStruct((t, D_SSD), BF16),
                   jax.ShapeDtypeStruct((t, D_POOL), BF16)],
        scratch_shapes=[pltpu.VMEM((SSD_STATE, D_SSD), F32),
                        pltpu.VMEM((CONV_HALO + CHUNK, D_CONV), F32),
                        pltpu.VMEM((POOL_HALO + CHUNK, D_POOL), F32)],
        compiler_params=_params("parallel", "arbitrary"),
        name="seq_mix",
    )(rest, rest, rest, rest, cw, cb, dtb, a_neg, dsk, nw, pw, ps)


def _out_ln_kernel(ya_ref, yah_ref, yb_ref, yc_ref, h_ref, w_ref, g_ref, b_ref, o_ref, *,
                   n_main_tiles):
    tm = o_ref.shape[0]
    in_main = pl.program_id(0) < n_main_tiles
    for r in range(0, tm, tm // 2):
        rows = slice(r, r + tm // 2)
        ya = jnp.where(in_main, ya_ref[rows, :], yah_ref[rows, :])
        acc = jnp.dot(ya, w_ref[0:D_ATTN, :], preferred_element_type=F32)
        acc += jnp.dot(yb_ref[rows, :], w_ref[D_ATTN:D_ATTN + D_POOL, :],
                       preferred_element_type=F32)
        acc += jnp.dot(yc_ref[rows, :], w_ref[D_ATTN + D_POOL:, :], preferred_element_type=F32)
        o_ref[rows, :] = _layer_norm(ALPHA * h_ref[rows, :] + acc, g_ref[...], b_ref[...])


def _out_ln(ya, ya_h, yb, yc, h, w, g, b, layer):
    t, d = h.shape
    tm = _pick_tile(ROW_TILES, ya.shape[0], ya_h.shape[0])
    n_main = ya.shape[0] // tm
    rows = lambda n: pl.BlockSpec((tm, n), lambda i: (i, 0))
    return pl.pallas_call(
        functools.partial(_out_ln_kernel, n_main_tiles=n_main),
        grid=(t // tm,),
        in_specs=[
            pl.BlockSpec((tm, D_ATTN), lambda i: (jnp.minimum(i, n_main - 1), 0)),
            pl.BlockSpec((tm, D_ATTN), lambda i: (jnp.maximum(i - n_main, 0), 0)),
            rows(D_POOL), rows(D_SSD), rows(d),
            pl.BlockSpec((None, d, d), lambda i: (layer, 0, 0)),
            pl.BlockSpec((1, d), lambda i: (0, 0)),
            pl.BlockSpec((1, d), lambda i: (0, 0)),
        ],
        out_specs=rows(d),
        out_shape=jax.ShapeDtypeStruct((t, d), F32),
        compiler_params=_params("parallel"),
        name="out_proj_ln",
    )(ya, ya_h, yb, yc, h, w, g, b)


def _lane_row(vals, offset):
    return jnp.zeros((1, LANES), F32).at[0, offset:offset + vals.shape[0]].set(vals.astype(F32))


def _proj_weight(w_in):
    wt = jnp.swapaxes(w_in, 1, 2)
    c = D_ATTN
    o = 3 * c + ATTN_HEADS
    f = wt[:, 3 * c:o]
    dt = wt[:, o + D_POOL + D_SSD + D_CONV:]
    pad = jnp.zeros((wt.shape[0], LANES - ATTN_HEADS - SSD_HEADS, wt.shape[2]), wt.dtype)
    segments = (wt[:, :c] * Q_SCALE,
                wt[:, c:3 * c],
                wt[:, o + D_POOL + D_SSD:o + D_POOL + D_SSD + D_CONV],
                wt[:, o:o + D_POOL],
                wt[:, o + D_POOL:o + D_POOL + D_SSD],
                jnp.concatenate([f, dt, pad], axis=1))
    return tuple(w.astype(BF16) for w in segments)


def _mixer(h, nb, nx, layer, w_proj, b_fgate, pool_w, pool_scale, conv_w, conv_b, dt_bias,
           a_log, d_skip, ssd_norm_w, w_out, ln_g, ln_b):
    rest, qa, ka, vt, qa_h, ka_h, vt_h = _proj(h, w_proj, _lane_row(b_fgate, MISC_F0), layer,
                                               nb, nx)
    ya = _attention(qa, ka, vt, ka_h, vt_h).reshape(nb * nx * CHUNK, D_ATTN)
    ya_h = _head_attention(qa_h, ka_h, vt_h)
    yc, yb = _seq_mix(rest, conv_w, conv_b.reshape(1, D_CONV), _lane_row(dt_bias, MISC_DT0),
                      _lane_row(-LOG2_E * jnp.exp(a_log.astype(F32)), MISC_DT0),
                      jnp.repeat(d_skip, SSD_HEAD_DIM).reshape(1, D_SSD),
                      ssd_norm_w.reshape(1, D_SSD), pool_w.astype(BF16),
                      pool_scale.reshape(1, D_POOL), nb, nx)
    return _out_ln(ya, ya_h, yb, yc, h, w_out, ln_g, ln_b, layer)


def kernel(x, meta, f1_gate, f1_up, f1_down, ln1_g, ln1_b, w_in, b_fgate, pool_w, pool_scale,
           conv_w, conv_b, dt_bias, a_log, d_skip, ssd_norm_w, w_out, ln2_g, ln2_b, f2_gate,
           f2_up, f2_down, ln3_g, ln3_b):
    nb, seq, d = x.shape
    assert d == D_MODEL and meta.shape == (N_META, D_MODEL) and seq % ATTN_TQ == 0
    nx = seq // CHUNK
    depth = f1_gate.shape[0]
    head = jnp.concatenate([jnp.zeros((PAD, d), x.dtype), meta.astype(x.dtype)], axis=0)
    heads = jnp.broadcast_to(head[None], (nb, CHUNK, d)).reshape(nb * CHUNK, d)
    row = lambda v: v.reshape(1, d)
    bf = lambda w: w.astype(BF16)
    f1 = (bf(f1_gate), bf(f1_up), bf(f1_down))
    f2 = (bf(f2_gate), bf(f2_up), bf(f2_down))
    w_proj, w_o = _proj_weight(w_in), bf(w_out)
    h = x.reshape(nb * seq, d)
    for i in range(depth):
        h = _ffn_ln(h, *f1, row(ln1_g[i]), row(ln1_b[i]), i, tail=heads if i == 0 else None)
        h = _mixer(h, nb, nx, i, w_proj, b_fgate[i], pool_w[i], pool_scale[i], conv_w[i],
                   conv_b[i], dt_bias[i], a_log[i], d_skip[i], ssd_norm_w[i], w_o,
                   row(ln2_g[i]), row(ln2_b[i]))
        h = _ffn_ln(h, *f2, row(ln3_g[i]), row(ln3_b[i]), i,
                    rows=nb * seq if i == depth - 1 else None)
    return h.reshape(nb, seq, d)
```

```python
import functools

import jax
import jax.numpy as jnp
from jax import lax
from jax.experimental import pallas as pl
from jax.experimental.pallas import tpu as pltpu

F32 = jnp.float32
BF16 = jnp.bfloat16

D_MODEL = 2048
N_META = 16
CHUNK = 128
PAD = CHUNK - N_META
LANES = 128

ATTN_HEADS = 8
ATTN_HEAD_DIM = 64
D_ATTN = ATTN_HEADS * ATTN_HEAD_DIM
POOL_WINDOWS = (2, 4, 8, 16)
POOL_GROUPS = 4
D_POOL = 512
POOL_GROUP_DIM = D_POOL // POOL_GROUPS
POOL_HALO = 16
D_SSD = 1024
SSD_HEAD_DIM = 64
SSD_HEADS = D_SSD // SSD_HEAD_DIM
SSD_GROUPS = 2
SSD_STATE = 128
CONV_K = 4
CONV_HALO = 8
D_CONV = D_SSD + 2 * SSD_GROUPS * SSD_STATE
D_FF_TILE = 512
ROW_TILES = (512, 384, 256, 128)
DEPTH = 2
ALPHA = (2 * DEPTH) ** 0.25
LN_EPS = 1e-5
RMS_EPS = 1e-5
NEG_BIG = -1e30

PROJ_TILE = 512
REST_XBC, REST_POOL, REST_Z, REST_MISC = 0, D_CONV, D_CONV + D_POOL, D_CONV + D_POOL + D_SSD
D_REST = REST_MISC + LANES
MISC_F0 = 0
MISC_DT0 = ATTN_HEADS

VMEM_LIMIT = 56 * 1024 * 1024


def _params(*sem):
    return pltpu.CompilerParams(dimension_semantics=sem, vmem_limit_bytes=VMEM_LIMIT)


def _pick_tile(candidates, *sizes):
    for c in candidates:
        if all(n % c == 0 for n in sizes):
            return c
    raise ValueError(f"no tile in {candidates} divides {sizes}")


def _phys_block(b, j, nb, nx):
    return jnp.where(j == 0, nb * nx + b, b * nx + j - 1)


def _layer_norm(y, g, b):
    mu = jnp.mean(y, axis=-1, keepdims=True)
    yc = y - mu
    var = jnp.mean(yc * yc, axis=-1, keepdims=True)
    return yc * lax.rsqrt(var + LN_EPS) * g + b


def _silu(x):
    half = 0.5 * x
    return half + half * jnp.tanh(half)


def _softplus(x):
    return jnp.maximum(x, 0.0) + jnp.log1p(jnp.exp(-jnp.abs(x)))


def _cumsum_rows(x):
    n = x.shape[0]
    row = lax.broadcasted_iota(jnp.int32, x.shape, 0)
    d = 1
    while d < n:
        x = x + jnp.where(row >= d, pltpu.roll(x, d, axis=0), 0.0)
        d *= 2
    return x


def _ffn_ln_kernel(*refs, n_main_tiles):
    if n_main_tiles is None:
        x_ref, wg_ref, wu_ref, wd_ref, g_ref, b_ref, o_ref, xb_ref, acc_ref, y_ref = refs
        read_x = lambda: x_ref[...]
    else:
        x_ref, tail_ref, wg_ref, wu_ref, wd_ref, g_ref, b_ref, o_ref, xb_ref, acc_ref, y_ref = refs
        in_main = pl.program_id(0) < n_main_tiles
        read_x = lambda: jnp.where(in_main, x_ref[...], tail_ref[...])
    i, f = pl.program_id(0), pl.program_id(1)
    n_tiles, last = pl.num_programs(0) - 1, pl.num_programs(1) - 1

    def down_proj(xb):
        gate = jnp.dot(xb, wg_ref[...], preferred_element_type=F32)
        up = jnp.dot(xb, wu_ref[...], preferred_element_type=F32)
        act = (_silu(gate) * up).astype(BF16)
        return jnp.dot(act, wd_ref[...], preferred_element_type=F32)

    def normalize_previous():
        o_ref[...] = _layer_norm(y_ref[...], g_ref[...], b_ref[...])

    @pl.when((i == 0) & (f == 0))
    def _():
        y_ref[...] = jnp.zeros_like(y_ref)

    @pl.when((f == 0) & (i < n_tiles))
    def _():
        normalize_previous()
        xb = read_x().astype(BF16)
        xb_ref[...] = xb
        acc_ref[...] = down_proj(xb)

    @pl.when((f == 0) & (i == n_tiles))
    def _():
        normalize_previous()

    @pl.when((f > 0) & (f < last) & (i < n_tiles))
    def _():
        acc_ref[...] += down_proj(xb_ref[...])

    @pl.when((f == last) & (i < n_tiles))
    def _():
        y_ref[...] = ALPHA * read_x() + 0.5 * (acc_ref[...] + down_proj(xb_ref[...]))


def _ffn_ln(h, wg, wu, wd, g, b, *, tail=None, rows=None):
    d = h.shape[1]
    ff = wg.shape[1]
    t = h.shape[0] + (0 if tail is None else tail.shape[0])
    rows = t if rows is None else rows
    tm = _pick_tile(ROW_TILES, rows, *((h.shape[0], tail.shape[0]) if tail is not None else ()))
    tf = _pick_tile((D_FF_TILE, 256, 128), ff)
    n_tiles, nf = rows // tm, ff // tf
    assert nf >= 2
    tile = lambda i: jnp.minimum(i, n_tiles - 1)
    col = lambda i, f: jnp.where(i == n_tiles, nf - 1, f)
    x_specs = [pl.BlockSpec((tm, d), lambda i, f: (tile(i), 0))]
    operands = [h]
    n_main = None
    if tail is not None:
        n_main = h.shape[0] // tm
        x_specs = [pl.BlockSpec((tm, d), lambda i, f: (jnp.minimum(i, n_main - 1), 0)),
                   pl.BlockSpec((tm, d), lambda i, f: (jnp.maximum(tile(i) - n_main, 0), 0))]
        operands = [h, tail]
    return pl.pallas_call(
        functools.partial(_ffn_ln_kernel, n_main_tiles=n_main),
        grid=(n_tiles + 1, nf),
        in_specs=x_specs + [
            pl.BlockSpec((d, tf), lambda i, f: (0, col(i, f))),
            pl.BlockSpec((d, tf), lambda i, f: (0, col(i, f))),
            pl.BlockSpec((tf, d), lambda i, f: (col(i, f), 0)),
            pl.BlockSpec((1, d), lambda i, f: (0, 0)),
            pl.BlockSpec((1, d), lambda i, f: (0, 0)),
        ],
        out_specs=pl.BlockSpec((tm, d), lambda i, f: (jnp.maximum(i - 1, 0), 0)),
        out_shape=jax.ShapeDtypeStruct((rows, d), F32),
        scratch_shapes=[pltpu.VMEM((tm, d), BF16), pltpu.VMEM((tm, d), F32),
                        pltpu.VMEM((tm, d), F32)],
        compiler_params=_params("arbitrary", "arbitrary"),
        name="ffn_ln",
    )(*operands, wg, wu, wd, g, b)


LOG2_E = 1.4426950408889634
Q_SCALE = LOG2_E * ATTN_HEAD_DIM ** -0.5
AUG_Q = ATTN_HEAD_DIM
VT_ROWS = ATTN_HEAD_DIM + 16
ATTN_TQ = 2 * CHUNK
ATTN_QK_LEAD = ATTN_HEADS
NT_DIMS = (((1,), (1,)), ((), ()))


def _split3(c):
    hi = c.astype(BF16).astype(F32)
    mid = (c - hi).astype(BF16).astype(F32)
    lo = (c - hi - mid).astype(BF16).astype(F32)
    return hi, mid, lo


def _attn_operands(misc, q, k, v, bias, carry, n_pad):
    row = lax.broadcasted_iota(jnp.int32, (CHUNK, LANES), 0)
    lane = lax.broadcasted_iota(jnp.int32, (CHUNK, LANES), 1)
    is_pad = row < n_pad
    x = misc + bias
    log_f = jnp.minimum(x, 0.0) - jnp.log1p(jnp.exp(-jnp.abs(x)))
    c = _cumsum_rows(jnp.where(is_pad, 0.0, log_f)) + carry
    parts = _split3(c * LOG2_E)
    k_pad = jnp.where(lane == AUG_Q + 3, NEG_BIG, 0.0)
    q_aug, k_aug, v_t = [], [], []
    for pair in range(ATTN_HEADS // 2):
        q2 = q[:, pair * LANES:(pair + 1) * LANES]
        k2 = k[:, pair * LANES:(pair + 1) * LANES]
        v2t = v[:, pair * LANES:(pair + 1) * LANES].T
        for sub in range(2):
            h = 2 * pair + sub
            qh = q2 if sub == 0 else pltpu.roll(q2, ATTN_HEAD_DIM, axis=1)
            kh = k2 if sub == 0 else pltpu.roll(k2, ATTN_HEAD_DIM, axis=1)
            q_extra = jnp.where((lane >= AUG_Q + 3) & (lane < AUG_Q + 6), 1.0, 0.0)
            k_extra = jnp.where((lane >= AUG_Q) & (lane < AUG_Q + 3), 1.0, 0.0)
            for n, part in enumerate(parts):
                col = jnp.broadcast_to(part[:, h:h + 1], (CHUNK, LANES))
                q_extra = jnp.where(lane == AUG_Q + n, col, q_extra)
                k_extra = jnp.where(lane == AUG_Q + 3 + n, -col, k_extra)
            q_aug.append(jnp.where(lane < AUG_Q, qh, q_extra).astype(BF16))
            k_aug.append(jnp.where(is_pad, k_pad, jnp.where(lane < AUG_Q, kh, k_extra)).astype(BF16))
            v_t.append(v2t[sub * ATTN_HEAD_DIM:(sub + 1) * ATTN_HEAD_DIM, :].astype(BF16))
    return q_aug, k_aug, v_t, c[CHUNK - 1:CHUNK, :]


def _proj_kernel(x_ref, wq_ref, wkv_ref, wxbc_ref, wpool_ref, wz_ref, wmisc_ref, bf_ref,
                 rest_ref, qa_ref, ka_ref, vt_ref, qah_ref, kah_ref, vth_ref,
                 carry_ref, head_carry_ref, *, tiles_per_batch):
    g = pl.program_id(0)
    is_head = g == 0
    blocks = x_ref.shape[0] // CHUNK

    @pl.when(is_head)
    def _():
        carry_ref[...] = jnp.zeros_like(carry_ref)
        head_carry_ref[...] = jnp.zeros_like(head_carry_ref)

    xb = x_ref[...].astype(BF16)
    proj = lambda w: lax.dot_general(xb, w, NT_DIMS, preferred_element_type=F32)
    q = proj(wq_ref[...])
    k = proj(wkv_ref[0:D_ATTN, :])
    v = proj(wkv_ref[D_ATTN:2 * D_ATTN, :])
    misc = proj(wmisc_ref[...])
    rest_ref[:, REST_MISC:REST_MISC + LANES] = misc

    tile = jnp.maximum(g - 1, 0)
    first_of_row = tile % tiles_per_batch == 0
    carry = jnp.where(first_of_row, head_carry_ref[pl.ds(tile // tiles_per_batch, 1), :],
                      carry_ref[...])
    n_pad = jnp.where(is_head, PAD, 0)
    ones_row = jnp.where(lax.broadcasted_iota(jnp.int32, (VT_ROWS - ATTN_HEAD_DIM, CHUNK), 0) == 0,
                         1.0, 0.0).astype(BF16)
    block_sums = []
    for r in range(blocks):
        rows = slice(r * CHUNK, (r + 1) * CHUNK)
        half = slice((r % 2) * CHUNK, (r % 2 + 1) * CHUNK)
        q_aug, k_aug, v_t, carry = _attn_operands(
            misc[rows], q[rows], k[rows], v[rows], bf_ref[...],
            jnp.where(is_head, 0.0, carry), n_pad)
        block_sums.append(carry)
        for h in range(ATTN_HEADS):
            qa_ref[0, h, rows, :] = q_aug[h]
            ka_ref[0, h, rows, :] = k_aug[h]
            vt_ref[0, r // 2, h, 0:ATTN_HEAD_DIM, half] = v_t[h]
            vt_ref[0, r // 2, h, ATTN_HEAD_DIM:VT_ROWS, half] = ones_row
    carry_ref[...] = carry

    for w_ref, col0 in ((wxbc_ref, REST_XBC), (wpool_ref, REST_POOL), (wz_ref, REST_Z)):
        for c0 in range(0, w_ref.shape[0], PROJ_TILE):
            rest_ref[:, col0 + c0:col0 + c0 + PROJ_TILE] = proj(w_ref[c0:c0 + PROJ_TILE, :])

    @pl.when(is_head)
    def _():
        for r in range(blocks):
            rows = slice(r * CHUNK, (r + 1) * CHUNK)
            half = slice((r % 2) * CHUNK, (r % 2 + 1) * CHUNK)
            head_carry_ref[r:r + 1, :] = block_sums[r]
            for h in range(ATTN_HEADS):
                qah_ref[r, h] = qa_ref[0, h, rows, :]
                kah_ref[r, h] = ka_ref[0, h, rows, :]
                vth_ref[r, h] = vt_ref[0, r // 2, h, :, half]


def _proj(h, ws, bias, layer, nb, nx):
    t, d = h.shape
    seq = nx * CHUNK
    tm = nb * CHUNK
    assert seq % tm == 0 and tm % ATTN_TQ == 0 and t == nb * seq + tm
    tpb = seq // tm
    n_main = nb * tpb
    x_tile = lambda g: jnp.maximum(g - 1, 0)
    resident = lambda w: pl.BlockSpec((None, w.shape[1], d), lambda g: (layer, 0, 0),
                                      pipeline_mode=pl.Buffered(1))
    whole = lambda shape: pl.BlockSpec(shape, lambda g: (0,) * len(shape))
    rows_of = lambda g: jnp.where(g == 0, n_main, g - 1)
    head_qk = (nb, ATTN_HEADS, CHUNK, LANES)
    head_vt = (nb, ATTN_HEADS, VT_ROWS, CHUNK)
    return pl.pallas_call(
        functools.partial(_proj_kernel, tiles_per_batch=tpb),
        grid=(n_main + 1,),
        in_specs=[pl.BlockSpec((tm, d), lambda g: (rows_of(g), 0))] + [resident(w) for w in ws]
        + [pl.BlockSpec((1, LANES), lambda g: (0, 0))],
        out_specs=[
            pl.BlockSpec((tm, D_REST), lambda g: (rows_of(g), 0)),
            pl.BlockSpec((1, ATTN_HEADS, tm, LANES),
                         lambda g: (x_tile(g) // tpb, 0, x_tile(g) % tpb, 0)),
            pl.BlockSpec((1, ATTN_HEADS, tm, LANES),
                         lambda g: (x_tile(g) // tpb, 0, x_tile(g) % tpb, 0)),
            pl.BlockSpec((1, tm // ATTN_TQ, ATTN_HEADS, VT_ROWS, ATTN_TQ),
                         lambda g: (x_tile(g) // tpb, x_tile(g) % tpb, 0, 0, 0)),
            whole(head_qk), whole(head_qk), whole(head_vt),
        ],
        out_shape=[
            jax.ShapeDtypeStruct((t, D_REST), F32),
            jax.ShapeDtypeStruct((nb, ATTN_HEADS, seq, LANES), BF16),
            jax.ShapeDtypeStruct((nb, ATTN_HEADS, seq, LANES), BF16),
            jax.ShapeDtypeStruct((nb, seq // ATTN_TQ, ATTN_HEADS, VT_ROWS, ATTN_TQ), BF16),
            jax.ShapeDtypeStruct(head_qk, BF16),
            jax.ShapeDtypeStruct(head_qk, BF16),
            jax.ShapeDtypeStruct(head_vt, BF16),
        ],
        scratch_shapes=[pltpu.VMEM((1, LANES), F32), pltpu.VMEM((nb, LANES), F32)],
        compiler_params=_params("arbitrary"),
        name="in_proj",
    )(h, *ws, bias)


def _softmax_step(st, m_old, mask):
    if mask is not None:
        st = jnp.where(mask, st, NEG_BIG)
    m_new = jnp.maximum(m_old, jnp.max(st, axis=0, keepdims=True))
    return m_new, jnp.exp2(m_old - m_new), jnp.exp2(st - m_new).astype(BF16)


def _write_heads(o_ref, accs):
    for pair in range(ATTN_HEADS // 2):
        halves = []
        for a in accs[2 * pair:2 * pair + 2]:
            halves.append(a[0:ATTN_HEAD_DIM, :] * (1.0 / a[ATTN_HEAD_DIM:ATTN_HEAD_DIM + 1, :]))
        o_ref[:, pair * LANES:(pair + 1) * LANES] = jnp.concatenate(halves, axis=0).T.astype(BF16)


def _attn_kernel(qa_ref, ka_ref, vt_ref, kah_ref, vth_ref, o_ref, acc_ref, st_ref, m_ref):
    i = pl.program_id(1)
    key = lax.broadcasted_iota(jnp.int32, (ATTN_TQ, ATTN_TQ), 0)
    qry = lax.broadcasted_iota(jnp.int32, (ATTN_TQ, ATTN_TQ), 1)
    acc_ref[...] = jnp.zeros_like(acc_ref)
    m_ref[...] = jnp.full(m_ref.shape, NEG_BIG, F32)

    def step(load_keys, nkeys, vt_chunks, mask=None):
        def qk(h):
            st_ref[h, 0:nkeys, :] = lax.dot_general(load_keys(h), qa_ref[0, h], NT_DIMS,
                                                    preferred_element_type=F32)

        def softmax_pv(h):
            m_new, rescale, p = _softmax_step(st_ref[h, 0:nkeys, :], m_ref[h], mask)
            m_ref[h] = m_new
            acc = rescale * acc_ref[h]
            r = 0
            for vt in vt_chunks(h):
                n = vt.shape[1]
                acc += jnp.dot(vt, p[r:r + n, :], preferred_element_type=F32)
                r += n
            acc_ref[h] = acc

        for h in range(ATTN_HEADS + ATTN_QK_LEAD):
            if h < ATTN_HEADS:
                qk(h)
            if h >= ATTN_QK_LEAD:
                softmax_pv(h - ATTN_QK_LEAD)

    def x_keys(start, nkeys):
        return lambda h: ka_ref[0, h, pl.ds(start, nkeys), :]

    step(lambda h: kah_ref[0, h], CHUNK, lambda h: (vth_ref[0, h],))

    def quad_step(s, _):
        step(x_keys(pl.multiple_of(s * 2 * ATTN_TQ, 2 * ATTN_TQ), 2 * ATTN_TQ), 2 * ATTN_TQ,
             lambda h: (vt_ref[0, 2 * s, h], vt_ref[0, 2 * s + 1, h]))
        return 0

    lax.fori_loop(0, i // 2, quad_step, 0)

    @pl.when(i % 2 == 1)
    def _():
        step(x_keys(pl.multiple_of((i - 1) * ATTN_TQ, ATTN_TQ), ATTN_TQ), ATTN_TQ,
             lambda h: (vt_ref[0, i - 1, h],))

    step(x_keys(pl.multiple_of(i * ATTN_TQ, ATTN_TQ), ATTN_TQ), ATTN_TQ,
         lambda h: (vt_ref[0, i, h],), key <= qry)
    _write_heads(o_ref.at[0], [acc_ref[h] for h in range(ATTN_HEADS)])


def _attention(qa, ka, vt, ka_h, vt_h):
    nb, _, seq, _ = qa.shape
    nq = seq // ATTN_TQ
    per_row = lambda shape: pl.BlockSpec((1,) + shape, lambda b, i: (b,) + (0,) * len(shape))
    return pl.pallas_call(
        _attn_kernel,
        grid=(nb, nq),
        in_specs=[
            pl.BlockSpec((1, ATTN_HEADS, ATTN_TQ, LANES), lambda b, i: (b, 0, i, 0)),
            per_row((ATTN_HEADS, seq, LANES)),
            per_row((nq, ATTN_HEADS, VT_ROWS, ATTN_TQ)),
            per_row((ATTN_HEADS, CHUNK, LANES)),
            per_row((ATTN_HEADS, VT_ROWS, CHUNK)),
        ],
        out_specs=pl.BlockSpec((1, ATTN_TQ, D_ATTN), lambda b, i: (b, i, 0)),
        out_shape=jax.ShapeDtypeStruct((nb, seq, D_ATTN), BF16),
        scratch_shapes=[pltpu.VMEM((ATTN_HEADS, VT_ROWS, ATTN_TQ), F32),
                        pltpu.VMEM((ATTN_HEADS, 2 * ATTN_TQ, ATTN_TQ), F32),
                        pltpu.VMEM((ATTN_HEADS, 1, ATTN_TQ), F32)],
        compiler_params=_params("parallel", "arbitrary"),
        name="fox_attention",
    )(qa, ka, vt, ka_h, vt_h)


def _head_attn_kernel(qah_ref, kah_ref, vth_ref, o_ref):
    key = lax.broadcasted_iota(jnp.int32, (CHUNK, CHUNK), 0)
    qry = lax.broadcasted_iota(jnp.int32, (CHUNK, CHUNK), 1)
    accs = []
    for h in range(ATTN_HEADS):
        st = lax.dot_general(kah_ref[0, h], qah_ref[0, h], NT_DIMS, preferred_element_type=F32)
        _, _, p = _softmax_step(st, jnp.full((1, CHUNK), NEG_BIG, F32), key <= qry)
        accs.append(jnp.dot(vth_ref[0, h], p, preferred_element_type=F32))
    _write_heads(o_ref, accs)


def _head_attention(qa_h, ka_h, vt_h):
    nb = qa_h.shape[0]
    per_row = lambda shape: pl.BlockSpec((1,) + shape, lambda b: (b,) + (0,) * len(shape))
    return pl.pallas_call(
        _head_attn_kernel,
        grid=(nb,),
        in_specs=[per_row((ATTN_HEADS, CHUNK, LANES)), per_row((ATTN_HEADS, CHUNK, LANES)),
                  per_row((ATTN_HEADS, VT_ROWS, CHUNK))],
        out_specs=pl.BlockSpec((CHUNK, D_ATTN), lambda b: (b, 0)),
        out_shape=jax.ShapeDtypeStruct((nb * CHUNK, D_ATTN), BF16),
        compiler_params=_params("parallel"),
        name="head_attention",
    )(qa_h, ka_h, vt_h)


def _expand_heads(v):
    r = v.shape[0]
    lane = lax.broadcasted_iota(jnp.int32, (r, LANES), 1)
    parts = []
    for k in range(SSD_HEADS // 2):
        l0 = MISC_DT0 + 2 * k
        a = jnp.broadcast_to(v[:, l0:l0 + 1], (r, LANES))
        b = jnp.broadcast_to(v[:, l0 + 1:l0 + 2], (r, LANES))
        parts.append(jnp.where(lane < SSD_HEAD_DIM, a, b))
    return jnp.concatenate(parts, axis=1)


def _pool_chunk(c, u_ref, pw_ref, ps_ref, yb_ref, buf_ref):
    @pl.when(c == 0)
    def _():
        buf_ref[0:POOL_HALO, :] = jnp.zeros((POOL_HALO, D_POOL), F32)

    @pl.when(c > 0)
    def _():
        buf_ref[0:POOL_HALO, :] = buf_ref[CHUNK:CHUNK + POOL_HALO, :]

    buf_ref[POOL_HALO:POOL_HALO + CHUNK, :] = u_ref[...]

    @pl.when(c == 0)
    def _():
        buf_ref[POOL_HALO:POOL_HALO + PAD, :] = jnp.zeros((PAD, D_POOL), F32)

    seen = c * CHUNK - PAD + 1 + lax.broadcasted_iota(jnp.int32, (CHUNK, POOL_GROUP_DIM), 0)
    for g, w in enumerate(POOL_WINDOWS):
        lo, hi = g * POOL_GROUP_DIM, (g + 1) * POOL_GROUP_DIM
        assert w & (w - 1) == 0 and w <= POOL_HALO
        win = buf_ref[:, lo:hi]
        s = 1
        while s < w:
            win = win + pltpu.roll(win, s, axis=0)
            s *= 2
        win = win[POOL_HALO:, :]
        u = buf_ref[POOL_HALO:POOL_HALO + CHUNK, lo:hi]
        cnt = jnp.clip(seen, 1, w).astype(F32)
        diff = (win / cnt - u).astype(BF16)
        mixed = jnp.dot(diff, pw_ref[g], preferred_element_type=F32)
        yb_ref[:, lo:hi] = (mixed * ps_ref[:, lo:hi]).astype(BF16)


def _seq_mix_kernel(*refs, n_cast):
    (xbc_ref, z_ref, misc_ref, u_ref, cw_ref, cb_ref, dtb_ref, a_ref, dsk_ref, nw_ref, pw_ref,
     ps_ref) = refs[:12]
    cast_src = refs[12:12 + n_cast]
    yc_ref, yb_ref = refs[12 + n_cast:14 + n_cast]
    cast_dst = refs[14 + n_cast:14 + 2 * n_cast]
    state_ref, ext_ref, buf_ref = refs[14 + 2 * n_cast:]
    c = pl.program_id(1)
    for src_ref, dst_ref in zip(cast_src, cast_dst):
        dst_ref[...] = src_ref[...].astype(BF16)
    _pool_chunk(c, u_ref, pw_ref, ps_ref, yb_ref, buf_ref)

    @pl.when(c == 0)
    def _():
        state_ref[...] = jnp.zeros_like(state_ref)
        ext_ref[0:CONV_HALO, :] = jnp.zeros((CONV_HALO, D_CONV), F32)

    @pl.when(c > 0)
    def _():
        ext_ref[0:CONV_HALO, :] = ext_ref[CHUNK:CHUNK + CONV_HALO, :]

    ext_ref[CONV_HALO:CONV_HALO + CHUNK, :] = xbc_ref[...]

    @pl.when(c == 0)
    def _():
        ext_ref[CONV_HALO:CONV_HALO + PAD, :] = jnp.zeros((PAD, D_CONV), F32)

    conv = cb_ref[...]
    for k in range(CONV_K):
        off = CONV_HALO - (CONV_K - 1) + k
        conv = conv + cw_ref[k:k + 1, :] * ext_ref[off:off + CHUNK, :]
    xc = _silu(conv)
    xs = xc[:, :D_SSD]
    gn = SSD_GROUPS * SSD_STATE
    bm = xc[:, D_SSD:D_SSD + gn]
    cm = xc[:, D_SSD + gn:D_SSD + 2 * gn]

    row = lax.broadcasted_iota(jnp.int32, (CHUNK, CHUNK), 0)
    col = lax.broadcasted_iota(jnp.int32, (CHUNK, CHUNK), 1)
    dt = _softplus(misc_ref[...] + dtb_ref[...])
    dt = jnp.where(row >= jnp.where(c == 0, PAD, 0), dt, 0.0)
    a_cs = _cumsum_rows(dt * a_ref[...])
    a_cs_t = a_cs.T
    a_last = a_cs[CHUNK - 1:CHUNK, :]
    x_dt = xs * _expand_heads(dt)
    x_dt_b = x_dt.astype(BF16)
    decay_out = _expand_heads(jnp.exp2(a_cs))
    x_state = (x_dt * _expand_heads(jnp.exp2(a_last - a_cs))).astype(BF16)
    chunk_decay = _expand_heads(jnp.exp2(a_last))

    causal = col <= row
    hpg = SSD_HEADS // SSD_GROUPS
    gw = hpg * SSD_HEAD_DIM
    for g in range(SSD_GROUPS):
        n0, n1 = g * SSD_STATE, (g + 1) * SSD_STATE
        cm_g = cm[:, n0:n1].astype(BF16)
        bm_g = bm[:, n0:n1]
        cb = lax.dot_general(cm_g, bm_g.astype(BF16), NT_DIMS, preferred_element_type=F32)
        st = state_ref[:, g * gw:(g + 1) * gw]
        y_off = jnp.dot(cm_g, st.astype(BF16), preferred_element_type=F32)
        y_g = y_off * decay_out[:, g * gw:(g + 1) * gw]
        diag = []
        for r in range(hpg):
            h = g * hpg + r
            lane = MISC_DT0 + h
            seg = a_cs[:, lane:lane + 1] - a_cs_t[lane:lane + 1, :]
            m = (cb * jnp.exp2(jnp.where(causal, seg, NEG_BIG))).astype(BF16)
            diag.append(jnp.dot(m, x_dt_b[:, h * SSD_HEAD_DIM:(h + 1) * SSD_HEAD_DIM],
                                preferred_element_type=F32))
        y_g = y_g + jnp.concatenate(diag, axis=1)
        new = jnp.dot(bm_g.T.astype(BF16), x_state[:, g * gw:(g + 1) * gw],
                      preferred_element_type=F32)
        state_ref[:, g * gw:(g + 1) * gw] = chunk_decay[:, g * gw:(g + 1) * gw] * st + new

        sl = slice(g * gw, (g + 1) * gw)
        y_g = y_g + xs[:, sl] * dsk_ref[:, sl]
        gy = y_g * _silu(z_ref[:, sl])
        ms = jnp.mean(gy * gy, axis=-1, keepdims=True)
        yc_ref[:, sl] = (gy * lax.rsqrt(ms + RMS_EPS) * nw_ref[:, sl]).astype(BF16)


def _seq_mix(rest, cw, cb, dtb, a_neg, dsk, nw, pw, ps, nb, nx, casts=()):
    t = rest.shape[0]
    steps = nb * (nx + 1)
    cast_in, cast_out, cast_shapes = [], [], []
    for arr, layer, depth in casts:
        r, cols = arr.shape[0] // depth, arr.shape[1]
        rb = next(k for k in range(16, r + 1, 16) if r % k == 0 and r // k <= steps)
        nblk = r // rb
        blk_of = lambda b, c, nblk=nblk: jnp.minimum(b * (nx + 1) + c, nblk - 1)
        cast_in.append(pl.BlockSpec((rb, cols),
                                    lambda b, c, o=layer * nblk, f=blk_of: (o + f(b, c), 0)))
        cast_out.append(pl.BlockSpec((rb, cols), lambda b, c, f=blk_of: (f(b, c), 0)))
        cast_shapes.append(jax.ShapeDtypeStruct((r, cols), BF16))
    vec = lambda n: pl.BlockSpec((1, n), lambda b, c: (0, 0))
    blk = lambda b, c: _phys_block(b, c, nb, nx)
    rows = lambda n, col0: pl.BlockSpec((CHUNK, n), lambda b, c: (blk(b, c), col0 // n))
    return pl.pallas_call(
        functools.partial(_seq_mix_kernel, n_cast=len(casts)),
        grid=(nb, nx + 1),
        in_specs=[
            rows(D_CONV, REST_XBC), rows(D_SSD, REST_Z), rows(LANES, REST_MISC),
            rows(D_POOL, REST_POOL),
            pl.BlockSpec((CONV_K, D_CONV), lambda b, c: (0, 0)),
            vec(D_CONV), vec(LANES), vec(LANES), vec(D_SSD), vec(D_SSD),
            pl.BlockSpec((POOL_GROUPS, POOL_GROUP_DIM, POOL_GROUP_DIM), lambda b, c: (0, 0, 0)),
            vec(D_POOL),
        ] + cast_in,
        out_specs=[rows(D_SSD, 0), rows(D_POOL, 0)] + cast_out,
        out_shape=[jax.ShapeDtypeStruct((t, D_SSD), BF16),
                   jax.ShapeDtypeStruct((t, D_POOL), BF16)] + cast_shapes,
        scratch_shapes=[pltpu.VMEM((SSD_STATE, D_SSD), F32),
                        pltpu.VMEM((CONV_HALO + CHUNK, D_CONV), F32),
                        pltpu.VMEM((POOL_HALO + CHUNK, D_POOL), F32)],
        compiler_params=_params("parallel", "arbitrary"),
        name="seq_mix",
    )(rest, rest, rest, rest, cw, cb, dtb, a_neg, dsk, nw, pw, ps, *(a for a, _, _ in casts))


def _out_ln_kernel(ya_ref, yah_ref, yb_ref, yc_ref, h_ref, w_ref, g_ref, b_ref, o_ref, *,
                   n_main_tiles):
    tm = o_ref.shape[0]
    in_main = pl.program_id(0) < n_main_tiles
    for r in range(0, tm, tm // 2):
        rows = slice(r, r + tm // 2)
        ya = jnp.where(in_main, ya_ref[rows, :], yah_ref[rows, :])
        acc = jnp.dot(ya, w_ref[0:D_ATTN, :], preferred_element_type=F32)
        acc += jnp.dot(yb_ref[rows, :], w_ref[D_ATTN:D_ATTN + D_POOL, :],
                       preferred_element_type=F32)
        acc += jnp.dot(yc_ref[rows, :], w_ref[D_ATTN + D_POOL:, :], preferred_element_type=F32)
        o_ref[rows, :] = _layer_norm(ALPHA * h_ref[rows, :] + acc, g_ref[...], b_ref[...])


def _out_ln(ya, ya_h, yb, yc, h, w, g, b, layer):
    t, d = h.shape
    tm = _pick_tile(ROW_TILES, ya.shape[0], ya_h.shape[0])
    n_main = ya.shape[0] // tm
    rows = lambda n: pl.BlockSpec((tm, n), lambda i: (i, 0))
    return pl.pallas_call(
        functools.partial(_out_ln_kernel, n_main_tiles=n_main),
        grid=(t // tm,),
        in_specs=[
            pl.BlockSpec((tm, D_ATTN), lambda i: (jnp.minimum(i, n_main - 1), 0)),
            pl.BlockSpec((tm, D_ATTN), lambda i: (jnp.maximum(i - n_main, 0), 0)),
            rows(D_POOL), rows(D_SSD), rows(d),
            pl.BlockSpec((None, d, d), lambda i: (layer, 0, 0)),
            pl.BlockSpec((1, d), lambda i: (0, 0)),
            pl.BlockSpec((1, d), lambda i: (0, 0)),
        ],
        out_specs=rows(d),
        out_shape=jax.ShapeDtypeStruct((t, d), F32),
        compiler_params=_params("parallel"),
        name="out_proj_ln",
    )(ya, ya_h, yb, yc, h, w, g, b)


def _lane_row(vals, offset):
    return jnp.zeros((1, LANES), F32).at[0, offset:offset + vals.shape[0]].set(vals.astype(F32))


def _proj_weight(w_in):
    wt = jnp.swapaxes(w_in, 1, 2)
    c = D_ATTN
    o = 3 * c + ATTN_HEADS
    f = wt[:, 3 * c:o]
    dt = wt[:, o + D_POOL + D_SSD + D_CONV:]
    pad = jnp.zeros((wt.shape[0], LANES - ATTN_HEADS - SSD_HEADS, wt.shape[2]), wt.dtype)
    segments = (wt[:, :c] * Q_SCALE,
                wt[:, c:3 * c],
                wt[:, o + D_POOL + D_SSD:o + D_POOL + D_SSD + D_CONV],
                wt[:, o:o + D_POOL],
                wt[:, o + D_POOL:o + D_POOL + D_SSD],
                jnp.concatenate([f, dt, pad], axis=1))
    return tuple(w.astype(BF16) for w in segments)


def _mixer(h, nb, nx, layer, w_proj, b_fgate, pool_w, pool_scale, conv_w, conv_b, dt_bias,
           a_log, d_skip, ssd_norm_w, w_out, ln_g, ln_b, casts):
    rest, qa, ka, vt, qa_h, ka_h, vt_h = _proj(h, w_proj, _lane_row(b_fgate, MISC_F0), layer,
                                               nb, nx)
    ya = _attention(qa, ka, vt, ka_h, vt_h).reshape(nb * nx * CHUNK, D_ATTN)
    ya_h = _head_attention(qa_h, ka_h, vt_h)
    yc, yb, *cast = _seq_mix(rest, conv_w, conv_b.reshape(1, D_CONV),
                             _lane_row(dt_bias, MISC_DT0),
                             _lane_row(-LOG2_E * jnp.exp(a_log.astype(F32)), MISC_DT0),
                             jnp.repeat(d_skip, SSD_HEAD_DIM).reshape(1, D_SSD),
                             ssd_norm_w.reshape(1, D_SSD), pool_w.astype(BF16),
                             pool_scale.reshape(1, D_POOL), nb, nx, casts)
    return _out_ln(ya, ya_h, yb, yc, h, w_out, ln_g, ln_b, layer), cast


def kernel(x, meta, f1_gate, f1_up, f1_down, ln1_g, ln1_b, w_in, b_fgate, pool_w, pool_scale,
           conv_w, conv_b, dt_bias, a_log, d_skip, ssd_norm_w, w_out, ln2_g, ln2_b, f2_gate,
           f2_up, f2_down, ln3_g, ln3_b):
    nb, seq, d = x.shape
    assert d == D_MODEL and meta.shape == (N_META, D_MODEL) and seq % ATTN_TQ == 0
    nx = seq // CHUNK
    depth = f1_gate.shape[0]
    head = jnp.concatenate([jnp.zeros((PAD, d), x.dtype), meta.astype(x.dtype)], axis=0)
    heads = jnp.broadcast_to(head[None], (nb, CHUNK, d)).reshape(nb * CHUNK, d)
    row = lambda v: v.reshape(1, d)
    bf = lambda w: w.astype(BF16)
    ff = f1_gate.shape[2]
    w_proj, w_o = _proj_weight(w_in), bf(w_out)
    flat = lambda w: w.reshape(depth * d, ff)
    unflat = lambda ws: (ws[0], ws[1], ws[2].reshape(ff, d))
    ffn1 = (bf(f1_gate[0]), bf(f1_up[0]), bf(f1_down[0]))
    h = x.reshape(nb * seq, d)
    for i in range(depth):
        h = _ffn_ln(h, *ffn1, row(ln1_g[i]), row(ln1_b[i]), tail=heads if i == 0 else None)
        pending = [f2_gate, f2_up, f2_down] + ([f1_gate, f1_up, f1_down] if i + 1 < depth else [])
        layers = [i] * 3 + [i + 1] * 3
        h, cast = _mixer(h, nb, nx, i, w_proj, b_fgate[i], pool_w[i], pool_scale[i], conv_w[i],
                         conv_b[i], dt_bias[i], a_log[i], d_skip[i], ssd_norm_w[i], w_o,
                         row(ln2_g[i]), row(ln2_b[i]),
                         [(flat(w), l, depth) for w, l in zip(pending, layers)])
        ffn2, ffn1 = unflat(cast[:3]), (unflat(cast[3:]) if i + 1 < depth else None)
        h = _ffn_ln(h, *ffn2, row(ln3_g[i]), row(ln3_b[i]),
                    rows=nb * seq if i == depth - 1 else None)
    return h.reshape(nb, seq, d)
```

```python
import functools

import jax
import jax.numpy as jnp
from jax import lax
from jax.experimental import pallas as pl
from jax.experimental.pallas import tpu as pltpu

F32 = jnp.float32
BF16 = jnp.bfloat16

D_MODEL = 2048
N_META = 16
CHUNK = 128
PAD = CHUNK - N_META
LANES = 128

ATTN_HEADS = 8
ATTN_HEAD_DIM = 64
D_ATTN = ATTN_HEADS * ATTN_HEAD_DIM
POOL_WINDOWS = (2, 4, 8, 16)
POOL_GROUPS = 4
D_POOL = 512
POOL_GROUP_DIM = D_POOL // POOL_GROUPS
POOL_HALO = 16
D_SSD = 1024
SSD_HEAD_DIM = 64
SSD_HEADS = D_SSD // SSD_HEAD_DIM
SSD_GROUPS = 2
SSD_STATE = 128
CONV_K = 4
CONV_HALO = 8
D_CONV = D_SSD + 2 * SSD_GROUPS * SSD_STATE
D_FF_TILE = 512
ROW_TILES = (512, 384, 256, 128)
DEPTH = 2
ALPHA = (2 * DEPTH) ** 0.25
LN_EPS = 1e-5
RMS_EPS = 1e-5
NEG_BIG = -1e30

PROJ_TILE = 512
REST_XBC, REST_POOL, REST_Z, REST_MISC = 0, D_CONV, D_CONV + D_POOL, D_CONV + D_POOL + D_SSD
D_REST = REST_MISC + LANES
MISC_F0 = 0
MISC_DT0 = ATTN_HEADS

VMEM_LIMIT = 56 * 1024 * 1024


def _params(*sem):
    return pltpu.CompilerParams(dimension_semantics=sem, vmem_limit_bytes=VMEM_LIMIT)


def _pick_tile(candidates, *sizes):
    for c in candidates:
        if all(n % c == 0 for n in sizes):
            return c
    raise ValueError(f"no tile in {candidates} divides {sizes}")


def _phys_block(b, j, nb, nx):
    return jnp.where(j == 0, nb * nx + b, b * nx + j - 1)


def _layer_norm(y, g, b):
    mu = jnp.mean(y, axis=-1, keepdims=True)
    yc = y - mu
    var = jnp.mean(yc * yc, axis=-1, keepdims=True)
    return yc * lax.rsqrt(var + LN_EPS) * g + b


def _silu(x):
    half = 0.5 * x
    return half + half * jnp.tanh(half)


def _softplus(x):
    return jnp.maximum(x, 0.0) + jnp.log1p(jnp.exp(-jnp.abs(x)))


def _cumsum_rows(x):
    n = x.shape[0]
    row = lax.broadcasted_iota(jnp.int32, x.shape, 0)
    d = 1
    while d < n:
        x = x + jnp.where(row >= d, pltpu.roll(x, d, axis=0), 0.0)
        d *= 2
    return x


def _ffn_ln_kernel(*refs, n_main_tiles):
    if n_main_tiles is None:
        x_ref, wg_ref, wu_ref, wd_ref, g_ref, b_ref, o_ref, xb_ref, acc_ref, y_ref = refs
        read_x = lambda: x_ref[...]
    else:
        x_ref, tail_ref, wg_ref, wu_ref, wd_ref, g_ref, b_ref, o_ref, xb_ref, acc_ref, y_ref = refs
        in_main = pl.program_id(0) < n_main_tiles
        read_x = lambda: jnp.where(in_main, x_ref[...], tail_ref[...])
    i, f = pl.program_id(0), pl.program_id(1)
    n_tiles, last = pl.num_programs(0) - 1, pl.num_programs(1) - 1

    def down_proj(xb):
        gate = jnp.dot(xb, wg_ref[...], preferred_element_type=F32)
        up = jnp.dot(xb, wu_ref[...], preferred_element_type=F32)
        act = (_silu(gate) * up).astype(BF16)
        return jnp.dot(act, wd_ref[...], preferred_element_type=F32)

    def normalize_previous():
        o_ref[...] = _layer_norm(y_ref[...], g_ref[...], b_ref[...])

    @pl.when((i == 0) & (f == 0))
    def _():
        y_ref[...] = jnp.zeros_like(y_ref)

    @pl.when((f == 0) & (i < n_tiles))
    def _():
        normalize_previous()
        xb = read_x().astype(BF16)
        xb_ref[...] = xb
        acc_ref[...] = down_proj(xb)

    @pl.when((f == 0) & (i == n_tiles))
    def _():
        normalize_previous()

    @pl.when((f > 0) & (f < last) & (i < n_tiles))
    def _():
        acc_ref[...] += down_proj(xb_ref[...])

    @pl.when((f == last) & (i < n_tiles))
    def _():
        y_ref[...] = ALPHA * read_x() + 0.5 * (acc_ref[...] + down_proj(xb_ref[...]))


def _ffn_ln(h, wg, wu, wd, g, b, *, tail=None, rows=None):
    d = h.shape[1]
    ff = wg.shape[1]
    t = h.shape[0] + (0 if tail is None else tail.shape[0])
    rows = t if rows is None else rows
    tm = _pick_tile(ROW_TILES, rows, *((h.shape[0], tail.shape[0]) if tail is not None else ()))
    tf = _pick_tile((D_FF_TILE, 256, 128), ff)
    n_tiles, nf = rows // tm, ff // tf
    assert nf >= 2
    tile = lambda i: jnp.minimum(i, n_tiles - 1)
    col = lambda i, f: jnp.where(i == n_tiles, nf - 1, f)
    x_specs = [pl.BlockSpec((tm, d), lambda i, f: (tile(i), 0))]
    operands = [h]
    n_main = None
    if tail is not None:
        n_main = h.shape[0] // tm
        x_specs = [pl.BlockSpec((tm, d), lambda i, f: (jnp.minimum(i, n_main - 1), 0)),
                   pl.BlockSpec((tm, d), lambda i, f: (jnp.maximum(tile(i) - n_main, 0), 0))]
        operands = [h, tail]
    return pl.pallas_call(
        functools.partial(_ffn_ln_kernel, n_main_tiles=n_main),
        grid=(n_tiles + 1, nf),
        in_specs=x_specs + [
            pl.BlockSpec((d, tf), lambda i, f: (0, col(i, f))),
            pl.BlockSpec((d, tf), lambda i, f: (0, col(i, f))),
            pl.BlockSpec((tf, d), lambda i, f: (col(i, f), 0)),
            pl.BlockSpec((1, d), lambda i, f: (0, 0)),
            pl.BlockSpec((1, d), lambda i, f: (0, 0)),
        ],
        out_specs=pl.BlockSpec((tm, d), lambda i, f: (jnp.maximum(i - 1, 0), 0)),
        out_shape=jax.ShapeDtypeStruct((rows, d), F32),
        scratch_shapes=[pltpu.VMEM((tm, d), BF16), pltpu.VMEM((tm, d), F32),
                        pltpu.VMEM((tm, d), F32)],
        compiler_params=_params("arbitrary", "arbitrary"),
        name="ffn_ln",
    )(*operands, wg, wu, wd, g, b)


LOG2_E = 1.4426950408889634
Q_SCALE = LOG2_E * ATTN_HEAD_DIM ** -0.5
AUG_Q = ATTN_HEAD_DIM
VT_ROWS = ATTN_HEAD_DIM + 16
ATTN_TQ = 2 * CHUNK
ATTN_QK_LEAD = ATTN_HEADS
NT_DIMS = (((1,), (1,)), ((), ()))


def _split3(c):
    hi = c.astype(BF16).astype(F32)
    mid = (c - hi).astype(BF16).astype(F32)
    lo = (c - hi - mid).astype(BF16).astype(F32)
    return hi, mid, lo


def _attn_operands(misc, q, k, v, bias, carry, n_pad):
    row = lax.broadcasted_iota(jnp.int32, (CHUNK, LANES), 0)
    lane = lax.broadcasted_iota(jnp.int32, (CHUNK, LANES), 1)
    is_pad = row < n_pad
    x = misc + bias
    log_f = jnp.minimum(x, 0.0) - jnp.log1p(jnp.exp(-jnp.abs(x)))
    c = _cumsum_rows(jnp.where(is_pad, 0.0, log_f)) + carry
    parts = _split3(c * LOG2_E)
    k_pad = jnp.where(lane == AUG_Q + 3, NEG_BIG, 0.0)
    q_aug, k_aug, v_t = [], [], []
    for pair in range(ATTN_HEADS // 2):
        q2 = q[:, pair * LANES:(pair + 1) * LANES]
        k2 = k[:, pair * LANES:(pair + 1) * LANES]
        v2t = v[:, pair * LANES:(pair + 1) * LANES].T
        for sub in range(2):
            h = 2 * pair + sub
            qh = q2 if sub == 0 else pltpu.roll(q2, ATTN_HEAD_DIM, axis=1)
            kh = k2 if sub == 0 else pltpu.roll(k2, ATTN_HEAD_DIM, axis=1)
            q_extra = jnp.where((lane >= AUG_Q + 3) & (lane < AUG_Q + 6), 1.0, 0.0)
            k_extra = jnp.where((lane >= AUG_Q) & (lane < AUG_Q + 3), 1.0, 0.0)
            for n, part in enumerate(parts):
                col = jnp.broadcast_to(part[:, h:h + 1], (CHUNK, LANES))
                q_extra = jnp.where(lane == AUG_Q + n, col, q_extra)
                k_extra = jnp.where(lane == AUG_Q + 3 + n, -col, k_extra)
            q_aug.append(jnp.where(lane < AUG_Q, qh, q_extra).astype(BF16))
            k_aug.append(jnp.where(is_pad, k_pad, jnp.where(lane < AUG_Q, kh, k_extra)).astype(BF16))
            v_t.append(v2t[sub * ATTN_HEAD_DIM:(sub + 1) * ATTN_HEAD_DIM, :].astype(BF16))
    return q_aug, k_aug, v_t, c[CHUNK - 1:CHUNK, :]


def _proj_kernel(x_ref, wq_ref, wkv_ref, wxbc_ref, wpool_ref, wz_ref, wmisc_ref, bf_ref,
                 rest_ref, qa_ref, ka_ref, vt_ref, qah_ref, kah_ref, vth_ref,
                 carry_ref, head_carry_ref, *, tiles_per_batch):
    g = pl.program_id(0)
    is_head = g == 0
    blocks = x_ref.shape[0] // CHUNK

    @pl.when(is_head)
    def _():
        carry_ref[...] = jnp.zeros_like(carry_ref)
        head_carry_ref[...] = jnp.zeros_like(head_carry_ref)

    xb = x_ref[...].astype(BF16)
    proj = lambda w: lax.dot_general(xb, w, NT_DIMS, preferred_element_type=F32)
    q = proj(wq_ref[...])
    k = proj(wkv_ref[0:D_ATTN, :])
    v = proj(wkv_ref[D_ATTN:2 * D_ATTN, :])
    misc = proj(wmisc_ref[...])
    rest_ref[:, REST_MISC:REST_MISC + LANES] = misc

    tile = jnp.maximum(g - 1, 0)
    first_of_row = tile % tiles_per_batch == 0
    carry = jnp.where(first_of_row, head_carry_ref[pl.ds(tile // tiles_per_batch, 1), :],
                      carry_ref[...])
    n_pad = jnp.where(is_head, PAD, 0)
    ones_row = jnp.where(lax.broadcasted_iota(jnp.int32, (VT_ROWS - ATTN_HEAD_DIM, CHUNK), 0) == 0,
                         1.0, 0.0).astype(BF16)
    block_sums = []
    for r in range(blocks):
        rows = slice(r * CHUNK, (r + 1) * CHUNK)
        half = slice((r % 2) * CHUNK, (r % 2 + 1) * CHUNK)
        q_aug, k_aug, v_t, carry = _attn_operands(
            misc[rows], q[rows], k[rows], v[rows], bf_ref[...],
            jnp.where(is_head, 0.0, carry), n_pad)
        block_sums.append(carry)
        for h in range(ATTN_HEADS):
            qa_ref[0, h, rows, :] = q_aug[h]
            ka_ref[0, h, rows, :] = k_aug[h]
            vt_ref[0, r // 2, h, 0:ATTN_HEAD_DIM, half] = v_t[h]
            vt_ref[0, r // 2, h, ATTN_HEAD_DIM:VT_ROWS, half] = ones_row
    carry_ref[...] = carry

    for w_ref, col0 in ((wxbc_ref, REST_XBC), (wpool_ref, REST_POOL), (wz_ref, REST_Z)):
        for c0 in range(0, w_ref.shape[0], PROJ_TILE):
            rest_ref[:, col0 + c0:col0 + c0 + PROJ_TILE] = proj(w_ref[c0:c0 + PROJ_TILE, :])

    @pl.when(is_head)
    def _():
        for r in range(blocks):
            rows = slice(r * CHUNK, (r + 1) * CHUNK)
            half = slice((r % 2) * CHUNK, (r % 2 + 1) * CHUNK)
            head_carry_ref[r:r + 1, :] = block_sums[r]
            for h in range(ATTN_HEADS):
                qah_ref[r, h] = qa_ref[0, h, rows, :]
                kah_ref[r, h] = ka_ref[0, h, rows, :]
                vth_ref[r, h] = vt_ref[0, r // 2, h, :, half]


def _proj(h, ws, bias, layer, nb, nx):
    t, d = h.shape
    seq = nx * CHUNK
    tm = nb * CHUNK
    assert seq % tm == 0 and tm % ATTN_TQ == 0 and t == nb * seq + tm
    tpb = seq // tm
    n_main = nb * tpb
    x_tile = lambda g: jnp.maximum(g - 1, 0)
    resident = lambda w: pl.BlockSpec((None, w.shape[1], d), lambda g: (layer, 0, 0),
                                      pipeline_mode=pl.Buffered(1))
    whole = lambda shape: pl.BlockSpec(shape, lambda g: (0,) * len(shape))
    rows_of = lambda g: jnp.where(g == 0, n_main, g - 1)
    head_qk = (nb, ATTN_HEADS, CHUNK, LANES)
    head_vt = (nb, ATTN_HEADS, VT_ROWS, CHUNK)
    return pl.pallas_call(
        functools.partial(_proj_kernel, tiles_per_batch=tpb),
        grid=(n_main + 1,),
        in_specs=[pl.BlockSpec((tm, d), lambda g: (rows_of(g), 0))] + [resident(w) for w in ws]
        + [pl.BlockSpec((1, LANES), lambda g: (0, 0))],
        out_specs=[
            pl.BlockSpec((tm, D_REST), lambda g: (rows_of(g), 0)),
            pl.BlockSpec((1, ATTN_HEADS, tm, LANES),
                         lambda g: (x_tile(g) // tpb, 0, x_tile(g) % tpb, 0)),
            pl.BlockSpec((1, ATTN_HEADS, tm, LANES),
                         lambda g: (x_tile(g) // tpb, 0, x_tile(g) % tpb, 0)),
            pl.BlockSpec((1, tm // ATTN_TQ, ATTN_HEADS, VT_ROWS, ATTN_TQ),
                         lambda g: (x_tile(g) // tpb, x_tile(g) % tpb, 0, 0, 0)),
            whole(head_qk), whole(head_qk), whole(head_vt),
        ],
        out_shape=[
            jax.ShapeDtypeStruct((t, D_REST), F32),
            jax.ShapeDtypeStruct((nb, ATTN_HEADS, seq, LANES), BF16),
            jax.ShapeDtypeStruct((nb, ATTN_HEADS, seq, LANES), BF16),
            jax.ShapeDtypeStruct((nb, seq // ATTN_TQ, ATTN_HEADS, VT_ROWS, ATTN_TQ), BF16),
            jax.ShapeDtypeStruct(head_qk, BF16),
            jax.ShapeDtypeStruct(head_qk, BF16),
            jax.ShapeDtypeStruct(head_vt, BF16),
        ],
        scratch_shapes=[pltpu.VMEM((1, LANES), F32), pltpu.VMEM((nb, LANES), F32)],
        compiler_params=_params("arbitrary"),
        name="in_proj",
    )(h, *ws, bias)


def _softmax_step(st, m_old, mask):
    if mask is not None:
        st = jnp.where(mask, st, NEG_BIG)
    m_new = jnp.maximum(m_old, jnp.max(st, axis=0, keepdims=True))
    return m_new, jnp.exp2(m_old - m_new), jnp.exp2(st - m_new).astype(BF16)


def _write_heads(o_ref, accs):
    for pair in range(ATTN_HEADS // 2):
        halves = []
        for a in accs[2 * pair:2 * pair + 2]:
            halves.append(a[0:ATTN_HEAD_DIM, :] * (1.0 / a[ATTN_HEAD_DIM:ATTN_HEAD_DIM + 1, :]))
        o_ref[:, pair * LANES:(pair + 1) * LANES] = jnp.concatenate(halves, axis=0).T.astype(BF16)


def _attn_kernel(qa_ref, ka_ref, vt_ref, kah_ref, vth_ref, o_ref, acc_ref, st_ref, m_ref):
    i = pl.program_id(1)
    key = lax.broadcasted_iota(jnp.int32, (ATTN_TQ, ATTN_TQ), 0)
    qry = lax.broadcasted_iota(jnp.int32, (ATTN_TQ, ATTN_TQ), 1)
    acc_ref[...] = jnp.zeros_like(acc_ref)
    m_ref[...] = jnp.full(m_ref.shape, NEG_BIG, F32)

    def step(load_keys, nkeys, vt_chunks, mask=None):
        def qk(h):
            st_ref[h, 0:nkeys, :] = lax.dot_general(load_keys(h), qa_ref[0, h], NT_DIMS,
                                                    preferred_element_type=F32)

        def softmax_pv(h):
            m_new, rescale, p = _softmax_step(st_ref[h, 0:nkeys, :], m_ref[h], mask)
            m_ref[h] = m_new
            acc = rescale * acc_ref[h]
            r = 0
            for vt in vt_chunks(h):
                n = vt.shape[1]
                acc += jnp.dot(vt, p[r:r + n, :], preferred_element_type=F32)
                r += n
            acc_ref[h] = acc

        for h in range(ATTN_HEADS + ATTN_QK_LEAD):
            if h < ATTN_HEADS:
                qk(h)
            if h >= ATTN_QK_LEAD:
                softmax_pv(h - ATTN_QK_LEAD)

    def x_keys(start, nkeys):
        return lambda h: ka_ref[0, h, pl.ds(start, nkeys), :]

    step(lambda h: kah_ref[0, h], CHUNK, lambda h: (vth_ref[0, h],))

    def quad_step(s, _):
        step(x_keys(pl.multiple_of(s * 2 * ATTN_TQ, 2 * ATTN_TQ), 2 * ATTN_TQ), 2 * ATTN_TQ,
             lambda h: (vt_ref[0, 2 * s, h], vt_ref[0, 2 * s + 1, h]))
        return 0

    lax.fori_loop(0, i // 2, quad_step, 0)

    @pl.when(i % 2 == 1)
    def _():
        step(x_keys(pl.multiple_of((i - 1) * ATTN_TQ, ATTN_TQ), ATTN_TQ), ATTN_TQ,
             lambda h: (vt_ref[0, i - 1, h],))

    step(x_keys(pl.multiple_of(i * ATTN_TQ, ATTN_TQ), ATTN_TQ), ATTN_TQ,
         lambda h: (vt_ref[0, i, h],), key <= qry)
    _write_heads(o_ref.at[0], [acc_ref[h] for h in range(ATTN_HEADS)])


def _attention(qa, ka, vt, ka_h, vt_h):
    nb, _, seq, _ = qa.shape
    nq = seq // ATTN_TQ
    per_row = lambda shape: pl.BlockSpec((1,) + shape, lambda b, i: (b,) + (0,) * len(shape))
    return pl.pallas_call(
        _attn_kernel,
        grid=(nb, nq),
        in_specs=[
            pl.BlockSpec((1, ATTN_HEADS, ATTN_TQ, LANES), lambda b, i: (b, 0, i, 0)),
            per_row((ATTN_HEADS, seq, LANES)),
            per_row((nq, ATTN_HEADS, VT_ROWS, ATTN_TQ)),
            per_row((ATTN_HEADS, CHUNK, LANES)),
            per_row((ATTN_HEADS, VT_ROWS, CHUNK)),
        ],
        out_specs=pl.BlockSpec((1, ATTN_TQ, D_ATTN), lambda b, i: (b, i, 0)),
        out_shape=jax.ShapeDtypeStruct((nb, seq, D_ATTN), BF16),
        scratch_shapes=[pltpu.VMEM((ATTN_HEADS, VT_ROWS, ATTN_TQ), F32),
                        pltpu.VMEM((ATTN_HEADS, 2 * ATTN_TQ, ATTN_TQ), F32),
                        pltpu.VMEM((ATTN_HEADS, 1, ATTN_TQ), F32)],
        compiler_params=_params("parallel", "arbitrary"),
        name="fox_attention",
    )(qa, ka, vt, ka_h, vt_h)


def _head_attn_kernel(qah_ref, kah_ref, vth_ref, o_ref):
    key = lax.broadcasted_iota(jnp.int32, (CHUNK, CHUNK), 0)
    qry = lax.broadcasted_iota(jnp.int32, (CHUNK, CHUNK), 1)
    accs = []
    for h in range(ATTN_HEADS):
        st = lax.dot_general(kah_ref[0, h], qah_ref[0, h], NT_DIMS, preferred_element_type=F32)
        _, _, p = _softmax_step(st, jnp.full((1, CHUNK), NEG_BIG, F32), key <= qry)
        accs.append(jnp.dot(vth_ref[0, h], p, preferred_element_type=F32))
    _write_heads(o_ref, accs)


def _head_attention(qa_h, ka_h, vt_h):
    nb = qa_h.shape[0]
    per_row = lambda shape: pl.BlockSpec((1,) + shape, lambda b: (b,) + (0,) * len(shape))
    return pl.pallas_call(
        _head_attn_kernel,
        grid=(nb,),
        in_specs=[per_row((ATTN_HEADS, CHUNK, LANES)), per_row((ATTN_HEADS, CHUNK, LANES)),
                  per_row((ATTN_HEADS, VT_ROWS, CHUNK))],
        out_specs=pl.BlockSpec((CHUNK, D_ATTN), lambda b: (b, 0)),
        out_shape=jax.ShapeDtypeStruct((nb * CHUNK, D_ATTN), BF16),
        compiler_params=_params("parallel"),
        name="head_attention",
    )(qa_h, ka_h, vt_h)


def _expand_heads(v):
    r = v.shape[0]
    lane = lax.broadcasted_iota(jnp.int32, (r, LANES), 1)
    parts = []
    for k in range(SSD_HEADS // 2):
        l0 = MISC_DT0 + 2 * k
        a = jnp.broadcast_to(v[:, l0:l0 + 1], (r, LANES))
        b = jnp.broadcast_to(v[:, l0 + 1:l0 + 2], (r, LANES))
        parts.append(jnp.where(lane < SSD_HEAD_DIM, a, b))
    return jnp.concatenate(parts, axis=1)


def _pool_chunk(c, u_ref, pw_ref, ps_ref, yb_ref, buf_ref):
    @pl.when(c == 0)
    def _():
        buf_ref[0:POOL_HALO, :] = jnp.zeros((POOL_HALO, D_POOL), F32)

    @pl.when(c > 0)
    def _():
        buf_ref[0:POOL_HALO, :] = buf_ref[CHUNK:CHUNK + POOL_HALO, :]

    buf_ref[POOL_HALO:POOL_HALO + CHUNK, :] = u_ref[...]

    @pl.when(c == 0)
    def _():
        buf_ref[POOL_HALO:POOL_HALO + PAD, :] = jnp.zeros((PAD, D_POOL), F32)

    seen = c * CHUNK - PAD + 1 + lax.broadcasted_iota(jnp.int32, (CHUNK, POOL_GROUP_DIM), 0)
    for g, w in enumerate(POOL_WINDOWS):
        lo, hi = g * POOL_GROUP_DIM, (g + 1) * POOL_GROUP_DIM
        assert w & (w - 1) == 0 and w <= POOL_HALO
        win = buf_ref[:, lo:hi]
        s = 1
        while s < w:
            win = win + pltpu.roll(win, s, axis=0)
            s *= 2
        win = win[POOL_HALO:, :]
        u = buf_ref[POOL_HALO:POOL_HALO + CHUNK, lo:hi]
        cnt = jnp.clip(seen, 1, w).astype(F32)
        diff = (win / cnt - u).astype(BF16)
        mixed = jnp.dot(diff, pw_ref[g], preferred_element_type=F32)
        yb_ref[:, lo:hi] = (mixed * ps_ref[:, lo:hi]).astype(BF16)


def _seq_mix_kernel(*refs, n_cast):
    (xbc_ref, z_ref, misc_ref, u_ref, cw_ref, cb_ref, dtb_ref, a_ref, dsk_ref, nw_ref, pw_ref,
     ps_ref) = refs[:12]
    cast_src = refs[12:12 + n_cast]
    yc_ref, yb_ref = refs[12 + n_cast:14 + n_cast]
    cast_dst = refs[14 + n_cast:14 + 2 * n_cast]
    state_ref, ext_ref, buf_ref = refs[14 + 2 * n_cast:]
    c = pl.program_id(1)
    for src_ref, dst_ref in zip(cast_src, cast_dst):
        dst_ref[...] = src_ref[...].astype(BF16)
    _pool_chunk(c, u_ref, pw_ref, ps_ref, yb_ref, buf_ref)

    @pl.when(c == 0)
    def _():
        state_ref[...] = jnp.zeros_like(state_ref)
        ext_ref[0:CONV_HALO, :] = jnp.zeros((CONV_HALO, D_CONV), F32)

    @pl.when(c > 0)
    def _():
        ext_ref[0:CONV_HALO, :] = ext_ref[CHUNK:CHUNK + CONV_HALO, :]

    ext_ref[CONV_HALO:CONV_HALO + CHUNK, :] = xbc_ref[...]

    @pl.when(c == 0)
    def _():
        ext_ref[CONV_HALO:CONV_HALO + PAD, :] = jnp.zeros((PAD, D_CONV), F32)

    conv = cb_ref[...]
    for k in range(CONV_K):
        off = CONV_HALO - (CONV_K - 1) + k
        conv = conv + cw_ref[k:k + 1, :] * ext_ref[off:off + CHUNK, :]
    xc = _silu(conv)
    xs = xc[:, :D_SSD]
    gn = SSD_GROUPS * SSD_STATE
    bm = xc[:, D_SSD:D_SSD + gn]
    cm = xc[:, D_SSD + gn:D_SSD + 2 * gn]

    row = lax.broadcasted_iota(jnp.int32, (CHUNK, CHUNK), 0)
    col = lax.broadcasted_iota(jnp.int32, (CHUNK, CHUNK), 1)
    dt = _softplus(misc_ref[...] + dtb_ref[...])
    dt = jnp.where(row >= jnp.where(c == 0, PAD, 0), dt, 0.0)
    a_cs = _cumsum_rows(dt * a_ref[...])
    a_cs_t = a_cs.T
    a_last = a_cs[CHUNK - 1:CHUNK, :]
    x_dt = xs * _expand_heads(dt)
    x_dt_b = x_dt.astype(BF16)
    decay_out = _expand_heads(jnp.exp2(a_cs))
    x_state = (x_dt * _expand_heads(jnp.exp2(a_last - a_cs))).astype(BF16)
    chunk_decay = _expand_heads(jnp.exp2(a_last))

    causal = col <= row
    hpg = SSD_HEADS // SSD_GROUPS
    gw = hpg * SSD_HEAD_DIM
    for g in range(SSD_GROUPS):
        n0, n1 = g * SSD_STATE, (g + 1) * SSD_STATE
        cm_g = cm[:, n0:n1].astype(BF16)
        bm_g = bm[:, n0:n1]
        cb = lax.dot_general(cm_g, bm_g.astype(BF16), NT_DIMS, preferred_element_type=F32)
        st = state_ref[:, g * gw:(g + 1) * gw]
        y_off = jnp.dot(cm_g, st.astype(BF16), preferred_element_type=F32)
        y_g = y_off * decay_out[:, g * gw:(g + 1) * gw]
        diag = []
        for r in range(hpg):
            h = g * hpg + r
            lane = MISC_DT0 + h
            seg = a_cs[:, lane:lane + 1] - a_cs_t[lane:lane + 1, :]
            m = (cb * jnp.exp2(jnp.where(causal, seg, NEG_BIG))).astype(BF16)
            diag.append(jnp.dot(m, x_dt_b[:, h * SSD_HEAD_DIM:(h + 1) * SSD_HEAD_DIM],
                                preferred_element_type=F32))
        y_g = y_g + jnp.concatenate(diag, axis=1)
        new = jnp.dot(bm_g.T.astype(BF16), x_state[:, g * gw:(g + 1) * gw],
                      preferred_element_type=F32)
        state_ref[:, g * gw:(g + 1) * gw] = chunk_decay[:, g * gw:(g + 1) * gw] * st + new

        sl = slice(g * gw, (g + 1) * gw)
        y_g = y_g + xs[:, sl] * dsk_ref[:, sl]
        gy = y_g * _silu(z_ref[:, sl])
        ms = jnp.mean(gy * gy, axis=-1, keepdims=True)
        yc_ref[:, sl] = (gy * lax.rsqrt(ms + RMS_EPS) * nw_ref[:, sl]).astype(BF16)


def _seq_mix(rest, cw, cb, dtb, a_neg, dsk, nw, pw, ps, nb, nx, casts=()):
    t = rest.shape[0]
    steps = nb * (nx + 1)
    cast_in, cast_out, cast_shapes = [], [], []
    for arr, layer, depth in casts:
        r, cols = arr.shape[0] // depth, arr.shape[1]
        rb = next(k for k in range(16, r + 1, 16) if r % k == 0 and r // k <= steps)
        nblk = r // rb
        blk_of = lambda b, c, nblk=nblk: jnp.minimum(b * (nx + 1) + c, nblk - 1)
        cast_in.append(pl.BlockSpec((rb, cols),
                                    lambda b, c, o=layer * nblk, f=blk_of: (o + f(b, c), 0)))
        cast_out.append(pl.BlockSpec((rb, cols), lambda b, c, f=blk_of: (f(b, c), 0)))
        cast_shapes.append(jax.ShapeDtypeStruct((r, cols), BF16))
    vec = lambda n: pl.BlockSpec((1, n), lambda b, c: (0, 0))
    blk = lambda b, c: _phys_block(b, c, nb, nx)
    rows = lambda n, col0: pl.BlockSpec((CHUNK, n), lambda b, c: (blk(b, c), col0 // n))
    return pl.pallas_call(
        functools.partial(_seq_mix_kernel, n_cast=len(casts)),
        grid=(nb, nx + 1),
        in_specs=[
            rows(D_CONV, REST_XBC), rows(D_SSD, REST_Z), rows(LANES, REST_MISC),
            rows(D_POOL, REST_POOL),
            pl.BlockSpec((CONV_K, D_CONV), lambda b, c: (0, 0)),
            vec(D_CONV), vec(LANES), vec(LANES), vec(D_SSD), vec(D_SSD),
            pl.BlockSpec((POOL_GROUPS, POOL_GROUP_DIM, POOL_GROUP_DIM), lambda b, c: (0, 0, 0)),
            vec(D_POOL),
        ] + cast_in,
        out_specs=[rows(D_SSD, 0), rows(D_POOL, 0)] + cast_out,
        out_shape=[jax.ShapeDtypeStruct((t, D_SSD), BF16),
                   jax.ShapeDtypeStruct((t, D_POOL), BF16)] + cast_shapes,
        scratch_shapes=[pltpu.VMEM((SSD_STATE, D_SSD), F32),
                        pltpu.VMEM((CONV_HALO + CHUNK, D_CONV), F32),
                        pltpu.VMEM((POOL_HALO + CHUNK, D_POOL), F32)],
        compiler_params=_params("parallel", "arbitrary"),
        name="seq_mix",
    )(rest, rest, rest, rest, cw, cb, dtb, a_neg, dsk, nw, pw, ps, *(a for a, _, _ in casts))


def _out_ln_kernel(ya_ref, yah_ref, yb_ref, yc_ref, h_ref, w_ref, g_ref, b_ref, o_ref, *,
                   n_main_tiles):
    tm = o_ref.shape[0]
    in_main = pl.program_id(0) < n_main_tiles
    for r in range(0, tm, tm // 2):
        rows = slice(r, r + tm // 2)
        ya = jnp.where(in_main, ya_ref[rows, :], yah_ref[rows, :])
        acc = jnp.dot(ya, w_ref[0:D_ATTN, :], preferred_element_type=F32)
        acc += jnp.dot(yb_ref[rows, :], w_ref[D_ATTN:D_ATTN + D_POOL, :],
                       preferred_element_type=F32)
        acc += jnp.dot(yc_ref[rows, :], w_ref[D_ATTN + D_POOL:, :], preferred_element_type=F32)
        o_ref[rows, :] = _layer_norm(ALPHA * h_ref[rows, :] + acc, g_ref[...], b_ref[...])


def _out_ln(ya, ya_h, yb, yc, h, w, g, b, layer):
    t, d = h.shape
    tm = _pick_tile(ROW_TILES, ya.shape[0], ya_h.shape[0])
    n_main = ya.shape[0] // tm
    rows = lambda n: pl.BlockSpec((tm, n), lambda i: (i, 0))
    return pl.pallas_call(
        functools.partial(_out_ln_kernel, n_main_tiles=n_main),
        grid=(t // tm,),
        in_specs=[
            pl.BlockSpec((tm, D_ATTN), lambda i: (jnp.minimum(i, n_main - 1), 0)),
            pl.BlockSpec((tm, D_ATTN), lambda i: (jnp.maximum(i - n_main, 0), 0)),
            rows(D_POOL), rows(D_SSD), rows(d),
            pl.BlockSpec((None, d, d), lambda i: (layer, 0, 0)),
            pl.BlockSpec((1, d), lambda i: (0, 0)),
            pl.BlockSpec((1, d), lambda i: (0, 0)),
        ],
        out_specs=rows(d),
        out_shape=jax.ShapeDtypeStruct((t, d), F32),
        compiler_params=_params("parallel"),
        name="out_proj_ln",
    )(ya, ya_h, yb, yc, h, w, g, b)


def _lane_row(vals, offset):
    return jnp.zeros((1, LANES), F32).at[0, offset:offset + vals.shape[0]].set(vals.astype(F32))


def _proj_weight(w_in):
    wt = jnp.swapaxes(w_in, 1, 2)
    c = D_ATTN
    o = 3 * c + ATTN_HEADS
    f = wt[:, 3 * c:o]
    dt = wt[:, o + D_POOL + D_SSD + D_CONV:]
    pad = jnp.zeros((wt.shape[0], LANES - ATTN_HEADS - SSD_HEADS, wt.shape[2]), wt.dtype)
    segments = (wt[:, :c] * Q_SCALE,
                wt[:, c:3 * c],
                wt[:, o + D_POOL + D_SSD:o + D_POOL + D_SSD + D_CONV],
                wt[:, o:o + D_POOL],
                wt[:, o + D_POOL:o + D_POOL + D_SSD],
                jnp.concatenate([f, dt, pad], axis=1))
    return tuple(w.astype(BF16) for w in segments)


def _mixer(h, nb, nx, layer, w_proj, b_fgate, pool_w, pool_scale, conv_w, conv_b, dt_bias,
           a_log, d_skip, ssd_norm_w, w_out, ln_g, ln_b, casts):
    rest, qa, ka, vt, qa_h, ka_h, vt_h = _proj(h, w_proj, _lane_row(b_fgate, MISC_F0), layer,
                                               nb, nx)
    ya = _attention(qa, ka, vt, ka_h, vt_h).reshape(nb * nx * CHUNK, D_ATTN)
    ya_h = _head_attention(qa_h, ka_h, vt_h)
    yc, yb, *cast = _seq_mix(rest, conv_w, conv_b.reshape(1, D_CONV),
                             _lane_row(dt_bias, MISC_DT0),
                             _lane_row(-LOG2_E * jnp.exp(a_log.astype(F32)), MISC_DT0),
                             jnp.repeat(d_skip, SSD_HEAD_DIM).reshape(1, D_SSD),
                             ssd_norm_w.reshape(1, D_SSD), pool_w.astype(BF16),
                             pool_scale.reshape(1, D_POOL), nb, nx, casts)
    return _out_ln(ya, ya_h, yb, yc, h, w_out, ln_g, ln_b, layer), cast


def kernel(x, meta, f1_gate, f1_up, f1_down, ln1_g, ln1_b, w_in, b_fgate, pool_w, pool_scale,
           conv_w, conv_b, dt_bias, a_log, d_skip, ssd_norm_w, w_out, ln2_g, ln2_b, f2_gate,
           f2_up, f2_down, ln3_g, ln3_b):
    nb, seq, d = x.shape
    assert d == D_MODEL and meta.shape == (N_META, D_MODEL) and seq % ATTN_TQ == 0
    nx = seq // CHUNK
    depth = f1_gate.shape[0]
    head = jnp.concatenate([jnp.zeros((PAD, d), x.dtype), meta.astype(x.dtype)], axis=0)
    heads = jnp.broadcast_to(head[None], (nb, CHUNK, d)).reshape(nb * CHUNK, d)
    row = lambda v: v.reshape(1, d)
    bf = lambda w: w.astype(BF16)
    w_proj, w_o = _proj_weight(w_in), bf(w_out)
    flat = lambda w: w.reshape(depth * w.shape[1], w.shape[2])
    ffn1 = (bf(f1_gate[0]), bf(f1_up[0]), bf(f1_down[0]))
    h = x.reshape(nb * seq, d)
    for i in range(depth):
        h = _ffn_ln(h, *ffn1, row(ln1_g[i]), row(ln1_b[i]), tail=heads if i == 0 else None)
        pending = [f2_gate, f2_up, f2_down] + ([f1_gate, f1_up, f1_down] if i + 1 < depth else [])
        layers = [i] * 3 + [i + 1] * 3
        h, cast = _mixer(h, nb, nx, i, w_proj, b_fgate[i], pool_w[i], pool_scale[i], conv_w[i],
                         conv_b[i], dt_bias[i], a_log[i], d_skip[i], ssd_norm_w[i], w_o,
                         row(ln2_g[i]), row(ln2_b[i]),
                         [(flat(w), l, depth) for w, l in zip(pending, layers)])
        ffn2, ffn1 = cast[:3], (cast[3:] if i + 1 < depth else None)
        h = _ffn_ln(h, *ffn2, row(ln3_g[i]), row(ln3_b[i]),
                    rows=nb * seq if i == depth - 1 else None)
    return h.reshape(nb, seq, d)
```

```python
import functools

import jax
import jax.numpy as jnp
from jax import lax
from jax.experimental import pallas as pl
from jax.experimental.pallas import tpu as pltpu

F32 = jnp.float32
BF16 = jnp.bfloat16

D_MODEL = 2048
N_META = 16
CHUNK = 128
PAD = CHUNK - N_META
LANES = 128

ATTN_HEADS = 8
ATTN_HEAD_DIM = 64
D_ATTN = ATTN_HEADS * ATTN_HEAD_DIM
POOL_WINDOWS = (2, 4, 8, 16)
POOL_GROUPS = 4
D_POOL = 512
POOL_GROUP_DIM = D_POOL // POOL_GROUPS
POOL_HALO = 16
D_SSD = 1024
SSD_HEAD_DIM = 64
SSD_HEADS = D_SSD // SSD_HEAD_DIM
SSD_GROUPS = 2
SSD_STATE = 128
CONV_K = 4
CONV_HALO = 8
D_CONV = D_SSD + 2 * SSD_GROUPS * SSD_STATE
D_FF_TILE = 512
ROW_TILES = (512, 384, 256, 128)
DEPTH = 2
ALPHA = (2 * DEPTH) ** 0.25
LN_EPS = 1e-5
RMS_EPS = 1e-5
NEG_BIG = -1e30

PROJ_TILE = 512
REST_XBC, REST_POOL, REST_Z, REST_MISC = 0, D_CONV, D_CONV + D_POOL, D_CONV + D_POOL + D_SSD
D_REST = REST_MISC + LANES
MISC_F0 = 0
MISC_DT0 = ATTN_HEADS

VMEM_LIMIT = 56 * 1024 * 1024


def _params(*sem):
    return pltpu.CompilerParams(dimension_semantics=sem, vmem_limit_bytes=VMEM_LIMIT)


def _pick_tile(candidates, *sizes):
    for c in candidates:
        if all(n % c == 0 for n in sizes):
            return c
    raise ValueError(f"no tile in {candidates} divides {sizes}")


def _phys_block(b, j, nb, nx):
    return jnp.where(j == 0, nb * nx + b, b * nx + j - 1)


def _layer_norm(y, g, b):
    mu = jnp.mean(y, axis=-1, keepdims=True)
    yc = y - mu
    var = jnp.mean(yc * yc, axis=-1, keepdims=True)
    return yc * lax.rsqrt(var + LN_EPS) * g + b


def _silu(x):
    half = 0.5 * x
    return half + half * jnp.tanh(half)


def _softplus(x):
    return jnp.maximum(x, 0.0) + jnp.log1p(jnp.exp(-jnp.abs(x)))


def _cumsum_rows(x):
    n = x.shape[0]
    row = lax.broadcasted_iota(jnp.int32, x.shape, 0)
    d = 1
    while d < n:
        x = x + jnp.where(row >= d, pltpu.roll(x, d, axis=0), 0.0)
        d *= 2
    return x


def _ffn_ln_kernel(*refs, n_main_tiles):
    if n_main_tiles is None:
        x_ref, wg_ref, wu_ref, wd_ref, g_ref, b_ref, o_ref, xb_ref, acc_ref, y_ref = refs
        read_x = lambda: x_ref[...]
    else:
        x_ref, tail_ref, wg_ref, wu_ref, wd_ref, g_ref, b_ref, o_ref, xb_ref, acc_ref, y_ref = refs
        in_main = pl.program_id(0) < n_main_tiles
        read_x = lambda: jnp.where(in_main, x_ref[...], tail_ref[...])
    i, f = pl.program_id(0), pl.program_id(1)
    n_tiles, last = pl.num_programs(0) - 1, pl.num_programs(1) - 1

    def down_proj(xb):
        gate = jnp.dot(xb, wg_ref[...], preferred_element_type=F32)
        up = jnp.dot(xb, wu_ref[...], preferred_element_type=F32)
        act = (_silu(gate) * up).astype(BF16)
        return jnp.dot(act, wd_ref[...], preferred_element_type=F32)

    def normalize_previous():
        o_ref[...] = _layer_norm(y_ref[...], g_ref[...], b_ref[...])

    @pl.when((i == 0) & (f == 0))
    def _():
        y_ref[...] = jnp.zeros_like(y_ref)

    @pl.when((f == 0) & (i < n_tiles))
    def _():
        normalize_previous()
        xb = read_x().astype(BF16)
        xb_ref[...] = xb
        acc_ref[...] = down_proj(xb)

    @pl.when((f == 0) & (i == n_tiles))
    def _():
        normalize_previous()

    @pl.when((f > 0) & (f < last) & (i < n_tiles))
    def _():
        acc_ref[...] += down_proj(xb_ref[...])

    @pl.when((f == last) & (i < n_tiles))
    def _():
        y_ref[...] = ALPHA * read_x() + 0.5 * (acc_ref[...] + down_proj(xb_ref[...]))


def _ffn_ln(h, wg, wu, wd, g, b, *, tail=None, rows=None):
    d = h.shape[1]
    ff = wg.shape[1]
    t = h.shape[0] + (0 if tail is None else tail.shape[0])
    rows = t if rows is None else rows
    tm = _pick_tile((768,) + ROW_TILES, rows,
                    *((h.shape[0], tail.shape[0]) if tail is not None else ()))
    tf = _pick_tile((D_FF_TILE, 256, 128), ff)
    n_tiles, nf = rows // tm, ff // tf
    assert nf >= 2
    tile = lambda i: jnp.minimum(i, n_tiles - 1)
    col = lambda i, f: jnp.where(i == n_tiles, nf - 1, f)
    x_specs = [pl.BlockSpec((tm, d), lambda i, f: (tile(i), 0))]
    operands = [h]
    n_main = None
    if tail is not None:
        n_main = h.shape[0] // tm
        x_specs = [pl.BlockSpec((tm, d), lambda i, f: (jnp.minimum(i, n_main - 1), 0)),
                   pl.BlockSpec((tm, d), lambda i, f: (jnp.maximum(tile(i) - n_main, 0), 0))]
        operands = [h, tail]
    return pl.pallas_call(
        functools.partial(_ffn_ln_kernel, n_main_tiles=n_main),
        grid=(n_tiles + 1, nf),
        in_specs=x_specs + [
            pl.BlockSpec((d, tf), lambda i, f: (0, col(i, f))),
            pl.BlockSpec((d, tf), lambda i, f: (0, col(i, f))),
            pl.BlockSpec((tf, d), lambda i, f: (col(i, f), 0)),
            pl.BlockSpec((1, d), lambda i, f: (0, 0)),
            pl.BlockSpec((1, d), lambda i, f: (0, 0)),
        ],
        out_specs=pl.BlockSpec((tm, d), lambda i, f: (jnp.maximum(i - 1, 0), 0)),
        out_shape=jax.ShapeDtypeStruct((rows, d), F32),
        scratch_shapes=[pltpu.VMEM((tm, d), BF16), pltpu.VMEM((tm, d), F32),
                        pltpu.VMEM((tm, d), F32)],
        compiler_params=_params("arbitrary", "arbitrary"),
        name="ffn_ln",
    )(*operands, wg, wu, wd, g, b)


LOG2_E = 1.4426950408889634
Q_SCALE = LOG2_E * ATTN_HEAD_DIM ** -0.5
AUG_Q = ATTN_HEAD_DIM
VT_ROWS = ATTN_HEAD_DIM + 16
ATTN_TQ = 2 * CHUNK
ATTN_QK_LEAD = ATTN_HEADS
NT_DIMS = (((1,), (1,)), ((), ()))


def _split3(c):
    hi = c.astype(BF16).astype(F32)
    mid = (c - hi).astype(BF16).astype(F32)
    lo = (c - hi - mid).astype(BF16).astype(F32)
    return hi, mid, lo


def _attn_operands(misc, q, k, v, bias, carry, n_pad):
    row = lax.broadcasted_iota(jnp.int32, (CHUNK, LANES), 0)
    lane = lax.broadcasted_iota(jnp.int32, (CHUNK, LANES), 1)
    is_pad = row < n_pad
    x = misc + bias
    log_f = jnp.minimum(x, 0.0) - jnp.log1p(jnp.exp(-jnp.abs(x)))
    c = _cumsum_rows(jnp.where(is_pad, 0.0, log_f)) + carry
    parts = _split3(c * LOG2_E)
    k_pad = jnp.where(lane == AUG_Q + 3, NEG_BIG, 0.0)
    q_aug, k_aug, v_t = [], [], []
    for pair in range(ATTN_HEADS // 2):
        q2 = q[:, pair * LANES:(pair + 1) * LANES]
        k2 = k[:, pair * LANES:(pair + 1) * LANES]
        v2t = v[:, pair * LANES:(pair + 1) * LANES].T
        for sub in range(2):
            h = 2 * pair + sub
            qh = q2 if sub == 0 else pltpu.roll(q2, ATTN_HEAD_DIM, axis=1)
            kh = k2 if sub == 0 else pltpu.roll(k2, ATTN_HEAD_DIM, axis=1)
            q_extra = jnp.where((lane >= AUG_Q + 3) & (lane < AUG_Q + 6), 1.0, 0.0)
            k_extra = jnp.where((lane >= AUG_Q) & (lane < AUG_Q + 3), 1.0, 0.0)
            for n, part in enumerate(parts):
                col = jnp.broadcast_to(part[:, h:h + 1], (CHUNK, LANES))
                q_extra = jnp.where(lane == AUG_Q + n, col, q_extra)
                k_extra = jnp.where(lane == AUG_Q + 3 + n, -col, k_extra)
            q_aug.append(jnp.where(lane < AUG_Q, qh, q_extra).astype(BF16))
            k_aug.append(jnp.where(is_pad, k_pad, jnp.where(lane < AUG_Q, kh, k_extra)).astype(BF16))
            v_t.append(v2t[sub * ATTN_HEAD_DIM:(sub + 1) * ATTN_HEAD_DIM, :].astype(BF16))
    return q_aug, k_aug, v_t, c[CHUNK - 1:CHUNK, :]


def _proj_kernel(x_ref, wq_ref, wkv_ref, wxbc_ref, wpool_ref, wz_ref, wmisc_ref, bf_ref,
                 rest_ref, qa_ref, ka_ref, vt_ref, qah_ref, kah_ref, vth_ref,
                 carry_ref, head_carry_ref, *, tiles_per_batch):
    g = pl.program_id(0)
    is_head = g == 0
    blocks = x_ref.shape[0] // CHUNK

    @pl.when(is_head)
    def _():
        carry_ref[...] = jnp.zeros_like(carry_ref)
        head_carry_ref[...] = jnp.zeros_like(head_carry_ref)

    xb = x_ref[...].astype(BF16)
    proj = lambda w: lax.dot_general(xb, w, NT_DIMS, preferred_element_type=F32)
    q = proj(wq_ref[...])
    k = proj(wkv_ref[0:D_ATTN, :])
    v = proj(wkv_ref[D_ATTN:2 * D_ATTN, :])
    misc = proj(wmisc_ref[...])
    rest_ref[:, REST_MISC:REST_MISC + LANES] = misc

    tile = jnp.maximum(g - 1, 0)
    first_of_row = tile % tiles_per_batch == 0
    carry = jnp.where(first_of_row, head_carry_ref[pl.ds(tile // tiles_per_batch, 1), :],
                      carry_ref[...])
    n_pad = jnp.where(is_head, PAD, 0)
    ones_row = jnp.where(lax.broadcasted_iota(jnp.int32, (VT_ROWS - ATTN_HEAD_DIM, CHUNK), 0) == 0,
                         1.0, 0.0).astype(BF16)
    block_sums = []
    for r in range(blocks):
        rows = slice(r * CHUNK, (r + 1) * CHUNK)
        half = slice((r % 2) * CHUNK, (r % 2 + 1) * CHUNK)
        q_aug, k_aug, v_t, carry = _attn_operands(
            misc[rows], q[rows], k[rows], v[rows], bf_ref[...],
            jnp.where(is_head, 0.0, carry), n_pad)
        block_sums.append(carry)
        for h in range(ATTN_HEADS):
            qa_ref[0, h, rows, :] = q_aug[h]
            ka_ref[0, h, rows, :] = k_aug[h]
            vt_ref[0, r // 2, h, 0:ATTN_HEAD_DIM, half] = v_t[h]
            vt_ref[0, r // 2, h, ATTN_HEAD_DIM:VT_ROWS, half] = ones_row
    carry_ref[...] = carry

    for w_ref, col0 in ((wxbc_ref, REST_XBC), (wpool_ref, REST_POOL), (wz_ref, REST_Z)):
        for c0 in range(0, w_ref.shape[0], PROJ_TILE):
            rest_ref[:, col0 + c0:col0 + c0 + PROJ_TILE] = proj(w_ref[c0:c0 + PROJ_TILE, :])

    @pl.when(is_head)
    def _():
        for r in range(blocks):
            rows = slice(r * CHUNK, (r + 1) * CHUNK)
            half = slice((r % 2) * CHUNK, (r % 2 + 1) * CHUNK)
            head_carry_ref[r:r + 1, :] = block_sums[r]
            for h in range(ATTN_HEADS):
                qah_ref[r, h] = qa_ref[0, h, rows, :]
                kah_ref[r, h] = ka_ref[0, h, rows, :]
                vth_ref[r, h] = vt_ref[0, r // 2, h, :, half]


def _proj(h, ws, bias, layer, nb, nx):
    t, d = h.shape
    seq = nx * CHUNK
    tm = nb * CHUNK
    assert seq % tm == 0 and tm % ATTN_TQ == 0 and t == nb * seq + tm
    tpb = seq // tm
    n_main = nb * tpb
    x_tile = lambda g: jnp.maximum(g - 1, 0)
    resident = lambda w: pl.BlockSpec((None, w.shape[1], d), lambda g: (layer, 0, 0),
                                      pipeline_mode=pl.Buffered(1))
    whole = lambda shape: pl.BlockSpec(shape, lambda g: (0,) * len(shape))
    rows_of = lambda g: jnp.where(g == 0, n_main, g - 1)
    head_qk = (nb, ATTN_HEADS, CHUNK, LANES)
    head_vt = (nb, ATTN_HEADS, VT_ROWS, CHUNK)
    return pl.pallas_call(
        functools.partial(_proj_kernel, tiles_per_batch=tpb),
        grid=(n_main + 1,),
        in_specs=[pl.BlockSpec((tm, d), lambda g: (rows_of(g), 0))] + [resident(w) for w in ws]
        + [pl.BlockSpec((1, LANES), lambda g: (0, 0))],
        out_specs=[
            pl.BlockSpec((tm, D_REST), lambda g: (rows_of(g), 0)),
            pl.BlockSpec((1, ATTN_HEADS, tm, LANES),
                         lambda g: (x_tile(g) // tpb, 0, x_tile(g) % tpb, 0)),
            pl.BlockSpec((1, ATTN_HEADS, tm, LANES),
                         lambda g: (x_tile(g) // tpb, 0, x_tile(g) % tpb, 0)),
            pl.BlockSpec((1, tm // ATTN_TQ, ATTN_HEADS, VT_ROWS, ATTN_TQ),
                         lambda g: (x_tile(g) // tpb, x_tile(g) % tpb, 0, 0, 0)),
            whole(head_qk), whole(head_qk), whole(head_vt),
        ],
        out_shape=[
            jax.ShapeDtypeStruct((t, D_REST), F32),
            jax.ShapeDtypeStruct((nb, ATTN_HEADS, seq, LANES), BF16),
            jax.ShapeDtypeStruct((nb, ATTN_HEADS, seq, LANES), BF16),
            jax.ShapeDtypeStruct((nb, seq // ATTN_TQ, ATTN_HEADS, VT_ROWS, ATTN_TQ), BF16),
            jax.ShapeDtypeStruct(head_qk, BF16),
            jax.ShapeDtypeStruct(head_qk, BF16),
            jax.ShapeDtypeStruct(head_vt, BF16),
        ],
        scratch_shapes=[pltpu.VMEM((1, LANES), F32), pltpu.VMEM((nb, LANES), F32)],
        compiler_params=_params("arbitrary"),
        name="in_proj",
    )(h, *ws, bias)


def _softmax_step(st, m_old, mask):
    if mask is not None:
        st = jnp.where(mask, st, NEG_BIG)
    m_new = jnp.maximum(m_old, jnp.max(st, axis=0, keepdims=True))
    return m_new, jnp.exp2(m_old - m_new), jnp.exp2(st - m_new).astype(BF16)


def _write_heads(o_ref, accs):
    for pair in range(ATTN_HEADS // 2):
        halves = []
        for a in accs[2 * pair:2 * pair + 2]:
            halves.append(a[0:ATTN_HEAD_DIM, :] * (1.0 / a[ATTN_HEAD_DIM:ATTN_HEAD_DIM + 1, :]))
        o_ref[:, pair * LANES:(pair + 1) * LANES] = jnp.concatenate(halves, axis=0).T.astype(BF16)


def _attn_kernel(qa_ref, ka_ref, vt_ref, kah_ref, vth_ref, o_ref, acc_ref, st_ref, m_ref):
    i = pl.program_id(1)
    key = lax.broadcasted_iota(jnp.int32, (ATTN_TQ, ATTN_TQ), 0)
    qry = lax.broadcasted_iota(jnp.int32, (ATTN_TQ, ATTN_TQ), 1)
    acc_ref[...] = jnp.zeros_like(acc_ref)
    m_ref[...] = jnp.full(m_ref.shape, NEG_BIG, F32)

    def step(load_keys, nkeys, vt_chunks, mask=None):
        def qk(h):
            st_ref[h, 0:nkeys, :] = lax.dot_general(load_keys(h), qa_ref[0, h], NT_DIMS,
                                                    preferred_element_type=F32)

        def softmax_pv(h):
            m_new, rescale, p = _softmax_step(st_ref[h, 0:nkeys, :], m_ref[h], mask)
            m_ref[h] = m_new
            acc = rescale * acc_ref[h]
            r = 0
            for vt in vt_chunks(h):
                n = vt.shape[1]
                acc += jnp.dot(vt, p[r:r + n, :], preferred_element_type=F32)
                r += n
            acc_ref[h] = acc

        for h in range(ATTN_HEADS + ATTN_QK_LEAD):
            if h < ATTN_HEADS:
                qk(h)
            if h >= ATTN_QK_LEAD:
                softmax_pv(h - ATTN_QK_LEAD)

    def x_keys(start, nkeys):
        return lambda h: ka_ref[0, h, pl.ds(start, nkeys), :]

    step(lambda h: kah_ref[0, h], CHUNK, lambda h: (vth_ref[0, h],))

    def quad_step(s, _):
        step(x_keys(pl.multiple_of(s * 2 * ATTN_TQ, 2 * ATTN_TQ), 2 * ATTN_TQ), 2 * ATTN_TQ,
             lambda h: (vt_ref[0, 2 * s, h], vt_ref[0, 2 * s + 1, h]))
        return 0

    lax.fori_loop(0, i // 2, quad_step, 0)

    @pl.when(i % 2 == 1)
    def _():
        step(x_keys(pl.multiple_of((i - 1) * ATTN_TQ, ATTN_TQ), ATTN_TQ), ATTN_TQ,
             lambda h: (vt_ref[0, i - 1, h],))

    step(x_keys(pl.multiple_of(i * ATTN_TQ, ATTN_TQ), ATTN_TQ), ATTN_TQ,
         lambda h: (vt_ref[0, i, h],), key <= qry)
    _write_heads(o_ref.at[0], [acc_ref[h] for h in range(ATTN_HEADS)])


def _attention(qa, ka, vt, ka_h, vt_h):
    nb, _, seq, _ = qa.shape
    nq = seq // ATTN_TQ
    per_row = lambda shape: pl.BlockSpec((1,) + shape, lambda b, i: (b,) + (0,) * len(shape))
    return pl.pallas_call(
        _attn_kernel,
        grid=(nb, nq),
        in_specs=[
            pl.BlockSpec((1, ATTN_HEADS, ATTN_TQ, LANES), lambda b, i: (b, 0, i, 0)),
            per_row((ATTN_HEADS, seq, LANES)),
            per_row((nq, ATTN_HEADS, VT_ROWS, ATTN_TQ)),
            per_row((ATTN_HEADS, CHUNK, LANES)),
            per_row((ATTN_HEADS, VT_ROWS, CHUNK)),
        ],
        out_specs=pl.BlockSpec((1, ATTN_TQ, D_ATTN), lambda b, i: (b, i, 0)),
        out_shape=jax.ShapeDtypeStruct((nb, seq, D_ATTN), BF16),
        scratch_shapes=[pltpu.VMEM((ATTN_HEADS, VT_ROWS, ATTN_TQ), F32),
                        pltpu.VMEM((ATTN_HEADS, 2 * ATTN_TQ, ATTN_TQ), F32),
                        pltpu.VMEM((ATTN_HEADS, 1, ATTN_TQ), F32)],
        compiler_params=_params("parallel", "arbitrary"),
        name="fox_attention",
    )(qa, ka, vt, ka_h, vt_h)


def _head_attn_kernel(qah_ref, kah_ref, vth_ref, o_ref):
    key = lax.broadcasted_iota(jnp.int32, (CHUNK, CHUNK), 0)
    qry = lax.broadcasted_iota(jnp.int32, (CHUNK, CHUNK), 1)
    accs = []
    for h in range(ATTN_HEADS):
        st = lax.dot_general(kah_ref[0, h], qah_ref[0, h], NT_DIMS, preferred_element_type=F32)
        _, _, p = _softmax_step(st, jnp.full((1, CHUNK), NEG_BIG, F32), key <= qry)
        accs.append(jnp.dot(vth_ref[0, h], p, preferred_element_type=F32))
    _write_heads(o_ref, accs)


def _head_attention(qa_h, ka_h, vt_h):
    nb = qa_h.shape[0]
    per_row = lambda shape: pl.BlockSpec((1,) + shape, lambda b: (b,) + (0,) * len(shape))
    return pl.pallas_call(
        _head_attn_kernel,
        grid=(nb,),
        in_specs=[per_row((ATTN_HEADS, CHUNK, LANES)), per_row((ATTN_HEADS, CHUNK, LANES)),
                  per_row((ATTN_HEADS, VT_ROWS, CHUNK))],
        out_specs=pl.BlockSpec((CHUNK, D_ATTN), lambda b: (b, 0)),
        out_shape=jax.ShapeDtypeStruct((nb * CHUNK, D_ATTN), BF16),
        compiler_params=_params("parallel"),
        name="head_attention",
    )(qa_h, ka_h, vt_h)


def _expand_heads(v):
    r = v.shape[0]
    lane = lax.broadcasted_iota(jnp.int32, (r, LANES), 1)
    parts = []
    for k in range(SSD_HEADS // 2):
        l0 = MISC_DT0 + 2 * k
        a = jnp.broadcast_to(v[:, l0:l0 + 1], (r, LANES))
        b = jnp.broadcast_to(v[:, l0 + 1:l0 + 2], (r, LANES))
        parts.append(jnp.where(lane < SSD_HEAD_DIM, a, b))
    return jnp.concatenate(parts, axis=1)


def _pool_chunk(c, u_ref, pw_ref, ps_ref, yb_ref, buf_ref):
    @pl.when(c == 0)
    def _():
        buf_ref[0:POOL_HALO, :] = jnp.zeros((POOL_HALO, D_POOL), F32)

    @pl.when(c > 0)
    def _():
        buf_ref[0:POOL_HALO, :] = buf_ref[CHUNK:CHUNK + POOL_HALO, :]

    buf_ref[POOL_HALO:POOL_HALO + CHUNK, :] = u_ref[...]

    @pl.when(c == 0)
    def _():
        buf_ref[POOL_HALO:POOL_HALO + PAD, :] = jnp.zeros((PAD, D_POOL), F32)

    seen = c * CHUNK - PAD + 1 + lax.broadcasted_iota(jnp.int32, (CHUNK, POOL_GROUP_DIM), 0)
    for g, w in enumerate(POOL_WINDOWS):
        lo, hi = g * POOL_GROUP_DIM, (g + 1) * POOL_GROUP_DIM
        assert w & (w - 1) == 0 and w <= POOL_HALO
        win = buf_ref[:, lo:hi]
        s = 1
        while s < w:
            win = win + pltpu.roll(win, s, axis=0)
            s *= 2
        win = win[POOL_HALO:, :]
        u = buf_ref[POOL_HALO:POOL_HALO + CHUNK, lo:hi]
        cnt = jnp.clip(seen, 1, w).astype(F32)
        diff = (win / cnt - u).astype(BF16)
        mixed = jnp.dot(diff, pw_ref[g], preferred_element_type=F32)
        yb_ref[:, lo:hi] = (mixed * ps_ref[:, lo:hi]).astype(BF16)


def _seq_mix_kernel(*refs, n_cast):
    (xbc_ref, z_ref, misc_ref, u_ref, cw_ref, cb_ref, dtb_ref, a_ref, dsk_ref, nw_ref, pw_ref,
     ps_ref) = refs[:12]
    cast_src = refs[12:12 + n_cast]
    yc_ref, yb_ref = refs[12 + n_cast:14 + n_cast]
    cast_dst = refs[14 + n_cast:14 + 2 * n_cast]
    state_ref, ext_ref, buf_ref = refs[14 + 2 * n_cast:]
    c = pl.program_id(1)
    for src_ref, dst_ref in zip(cast_src, cast_dst):
        dst_ref[...] = src_ref[...].astype(BF16)
    _pool_chunk(c, u_ref, pw_ref, ps_ref, yb_ref, buf_ref)

    @pl.when(c == 0)
    def _():
        state_ref[...] = jnp.zeros_like(state_ref)
        ext_ref[0:CONV_HALO, :] = jnp.zeros((CONV_HALO, D_CONV), F32)

    @pl.when(c > 0)
    def _():
        ext_ref[0:CONV_HALO, :] = ext_ref[CHUNK:CHUNK + CONV_HALO, :]

    ext_ref[CONV_HALO:CONV_HALO + CHUNK, :] = xbc_ref[...]

    @pl.when(c == 0)
    def _():
        ext_ref[CONV_HALO:CONV_HALO + PAD, :] = jnp.zeros((PAD, D_CONV), F32)

    conv = cb_ref[...]
    for k in range(CONV_K):
        off = CONV_HALO - (CONV_K - 1) + k
        conv = conv + cw_ref[k:k + 1, :] * ext_ref[off:off + CHUNK, :]
    xc = _silu(conv)
    xs = xc[:, :D_SSD]
    gn = SSD_GROUPS * SSD_STATE
    bm = xc[:, D_SSD:D_SSD + gn]
    cm = xc[:, D_SSD + gn:D_SSD + 2 * gn]

    row = lax.broadcasted_iota(jnp.int32, (CHUNK, CHUNK), 0)
    col = lax.broadcasted_iota(jnp.int32, (CHUNK, CHUNK), 1)
    dt = _softplus(misc_ref[...] + dtb_ref[...])
    dt = jnp.where(row >= jnp.where(c == 0, PAD, 0), dt, 0.0)
    a_cs = _cumsum_rows(dt * a_ref[...])
    a_cs_t = a_cs.T
    a_last = a_cs[CHUNK - 1:CHUNK, :]
    x_dt = xs * _expand_heads(dt)
    x_dt_b = x_dt.astype(BF16)
    decay_out = _expand_heads(jnp.exp2(a_cs))
    x_state = (x_dt * _expand_heads(jnp.exp2(a_last - a_cs))).astype(BF16)
    chunk_decay = _expand_heads(jnp.exp2(a_last))

    causal = col <= row
    hpg = SSD_HEADS // SSD_GROUPS
    gw = hpg * SSD_HEAD_DIM
    for g in range(SSD_GROUPS):
        n0, n1 = g * SSD_STATE, (g + 1) * SSD_STATE
        cm_g = cm[:, n0:n1].astype(BF16)
        bm_g = bm[:, n0:n1]
        cb = lax.dot_general(cm_g, bm_g.astype(BF16), NT_DIMS, preferred_element_type=F32)
        st = state_ref[:, g * gw:(g + 1) * gw]
        y_off = jnp.dot(cm_g, st.astype(BF16), preferred_element_type=F32)
        y_g = y_off * decay_out[:, g * gw:(g + 1) * gw]
        diag = []
        for r in range(hpg):
            h = g * hpg + r
            lane = MISC_DT0 + h
            seg = a_cs[:, lane:lane + 1] - a_cs_t[lane:lane + 1, :]
            m = (cb * jnp.exp2(jnp.where(causal, seg, NEG_BIG))).astype(BF16)
            diag.append(jnp.dot(m, x_dt_b[:, h * SSD_HEAD_DIM:(h + 1) * SSD_HEAD_DIM],
                                preferred_element_type=F32))
        y_g = y_g + jnp.concatenate(diag, axis=1)
        new = jnp.dot(bm_g.T.astype(BF16), x_state[:, g * gw:(g + 1) * gw],
                      preferred_element_type=F32)
        state_ref[:, g * gw:(g + 1) * gw] = chunk_decay[:, g * gw:(g + 1) * gw] * st + new

        sl = slice(g * gw, (g + 1) * gw)
        y_g = y_g + xs[:, sl] * dsk_ref[:, sl]
        gy = y_g * _silu(z_ref[:, sl])
        ms = jnp.mean(gy * gy, axis=-1, keepdims=True)
        yc_ref[:, sl] = (gy * lax.rsqrt(ms + RMS_EPS) * nw_ref[:, sl]).astype(BF16)


def _seq_mix(rest, cw, cb, dtb, a_neg, dsk, nw, pw, ps, nb, nx, casts=()):
    t = rest.shape[0]
    steps = nb * (nx + 1)
    cast_in, cast_out, cast_shapes = [], [], []
    for arr, layer, depth in casts:
        r, cols = arr.shape[0] // depth, arr.shape[1]
        rb = next(k for k in range(16, r + 1, 16) if r % k == 0 and r // k <= steps)
        nblk = r // rb
        blk_of = lambda b, c, nblk=nblk: jnp.minimum(b * (nx + 1) + c, nblk - 1)
        cast_in.append(pl.BlockSpec((rb, cols),
                                    lambda b, c, o=layer * nblk, f=blk_of: (o + f(b, c), 0)))
        cast_out.append(pl.BlockSpec((rb, cols), lambda b, c, f=blk_of: (f(b, c), 0)))
        cast_shapes.append(jax.ShapeDtypeStruct((r, cols), BF16))
    vec = lambda n: pl.BlockSpec((1, n), lambda b, c: (0, 0))
    blk = lambda b, c: _phys_block(b, c, nb, nx)
    rows = lambda n, col0: pl.BlockSpec((CHUNK, n), lambda b, c: (blk(b, c), col0 // n))
    return pl.pallas_call(
        functools.partial(_seq_mix_kernel, n_cast=len(casts)),
        grid=(nb, nx + 1),
        in_specs=[
            rows(D_CONV, REST_XBC), rows(D_SSD, REST_Z), rows(LANES, REST_MISC),
            rows(D_POOL, REST_POOL),
            pl.BlockSpec((CONV_K, D_CONV), lambda b, c: (0, 0)),
            vec(D_CONV), vec(LANES), vec(LANES), vec(D_SSD), vec(D_SSD),
            pl.BlockSpec((POOL_GROUPS, POOL_GROUP_DIM, POOL_GROUP_DIM), lambda b, c: (0, 0, 0)),
            vec(D_POOL),
        ] + cast_in,
        out_specs=[rows(D_SSD, 0), rows(D_POOL, 0)] + cast_out,
        out_shape=[jax.ShapeDtypeStruct((t, D_SSD), BF16),
                   jax.ShapeDtypeStruct((t, D_POOL), BF16)] + cast_shapes,
        scratch_shapes=[pltpu.VMEM((SSD_STATE, D_SSD), F32),
                        pltpu.VMEM((CONV_HALO + CHUNK, D_CONV), F32),
                        pltpu.VMEM((POOL_HALO + CHUNK, D_POOL), F32)],
        compiler_params=_params("parallel", "arbitrary"),
        name="seq_mix",
    )(rest, rest, rest, rest, cw, cb, dtb, a_neg, dsk, nw, pw, ps, *(a for a, _, _ in casts))


def _out_ln_kernel(ya_ref, yah_ref, yb_ref, yc_ref, h_ref, w_ref, g_ref, b_ref, o_ref, *,
                   n_main_tiles):
    tm = o_ref.shape[0]
    in_main = pl.program_id(0) < n_main_tiles
    for r in range(0, tm, tm // 2):
        rows = slice(r, r + tm // 2)
        ya = jnp.where(in_main, ya_ref[rows, :], yah_ref[rows, :])
        acc = jnp.dot(ya, w_ref[0:D_ATTN, :], preferred_element_type=F32)
        acc += jnp.dot(yb_ref[rows, :], w_ref[D_ATTN:D_ATTN + D_POOL, :],
                       preferred_element_type=F32)
        acc += jnp.dot(yc_ref[rows, :], w_ref[D_ATTN + D_POOL:, :], preferred_element_type=F32)
        o_ref[rows, :] = _layer_norm(ALPHA * h_ref[rows, :] + acc, g_ref[...], b_ref[...])


def _out_ln(ya, ya_h, yb, yc, h, w, g, b):
    t, d = h.shape
    tm = _pick_tile(ROW_TILES, ya.shape[0], ya_h.shape[0])
    n_main = ya.shape[0] // tm
    rows = lambda n: pl.BlockSpec((tm, n), lambda i: (i, 0))
    return pl.pallas_call(
        functools.partial(_out_ln_kernel, n_main_tiles=n_main),
        grid=(t // tm,),
        in_specs=[
            pl.BlockSpec((tm, D_ATTN), lambda i: (jnp.minimum(i, n_main - 1), 0)),
            pl.BlockSpec((tm, D_ATTN), lambda i: (jnp.maximum(i - n_main, 0), 0)),
            rows(D_POOL), rows(D_SSD), rows(d),
            pl.BlockSpec((d, d), lambda i: (0, 0)),
            pl.BlockSpec((1, d), lambda i: (0, 0)),
            pl.BlockSpec((1, d), lambda i: (0, 0)),
        ],
        out_specs=rows(d),
        out_shape=jax.ShapeDtypeStruct((t, d), F32),
        compiler_params=_params("parallel"),
        name="out_proj_ln",
    )(ya, ya_h, yb, yc, h, w, g, b)


def _lane_row(vals, offset):
    return jnp.zeros((1, LANES), F32).at[0, offset:offset + vals.shape[0]].set(vals.astype(F32))


def _proj_weight(w_in):
    wt = jnp.swapaxes(w_in, 1, 2)
    c = D_ATTN
    o = 3 * c + ATTN_HEADS
    f = wt[:, 3 * c:o]
    dt = wt[:, o + D_POOL + D_SSD + D_CONV:]
    pad = jnp.zeros((wt.shape[0], LANES - ATTN_HEADS - SSD_HEADS, wt.shape[2]), wt.dtype)
    segments = (wt[:, :c] * Q_SCALE,
                wt[:, c:3 * c],
                wt[:, o + D_POOL + D_SSD:o + D_POOL + D_SSD + D_CONV],
                wt[:, o:o + D_POOL],
                wt[:, o + D_POOL:o + D_POOL + D_SSD],
                jnp.concatenate([f, dt, pad], axis=1))
    return tuple(w.astype(BF16) for w in segments)


def _mixer(h, nb, nx, layer, w_proj, b_fgate, pool_w, pool_scale, conv_w, conv_b, dt_bias,
           a_log, d_skip, ssd_norm_w, w_out, ln_g, ln_b, casts):
    rest, qa, ka, vt, qa_h, ka_h, vt_h = _proj(h, w_proj, _lane_row(b_fgate, MISC_F0), layer,
                                               nb, nx)
    ya = _attention(qa, ka, vt, ka_h, vt_h).reshape(nb * nx * CHUNK, D_ATTN)
    ya_h = _head_attention(qa_h, ka_h, vt_h)
    yc, yb, *cast = _seq_mix(rest, conv_w, conv_b.reshape(1, D_CONV),
                             _lane_row(dt_bias, MISC_DT0),
                             _lane_row(-LOG2_E * jnp.exp(a_log.astype(F32)), MISC_DT0),
                             jnp.repeat(d_skip, SSD_HEAD_DIM).reshape(1, D_SSD),
                             ssd_norm_w.reshape(1, D_SSD), pool_w.astype(BF16),
                             pool_scale.reshape(1, D_POOL), nb, nx, casts + [w_out])
    return _out_ln(ya, ya_h, yb, yc, h, cast[-1], ln_g, ln_b), cast[:-1]


def kernel(x, meta, f1_gate, f1_up, f1_down, ln1_g, ln1_b, w_in, b_fgate, pool_w, pool_scale,
           conv_w, conv_b, dt_bias, a_log, d_skip, ssd_norm_w, w_out, ln2_g, ln2_b, f2_gate,
           f2_up, f2_down, ln3_g, ln3_b):
    nb, seq, d = x.shape
    assert d == D_MODEL and meta.shape == (N_META, D_MODEL) and seq % ATTN_TQ == 0
    nx = seq // CHUNK
    depth = f1_gate.shape[0]
    head = jnp.concatenate([jnp.zeros((PAD, d), x.dtype), meta.astype(x.dtype)], axis=0)
    heads = jnp.broadcast_to(head[None], (nb, CHUNK, d)).reshape(nb * CHUNK, d)
    row = lambda v: v.reshape(1, d)
    bf = lambda w: w.astype(BF16)
    w_proj = _proj_weight(w_in)
    flat = lambda w: w.reshape(depth * w.shape[1], w.shape[2])
    ffn1 = (bf(f1_gate[0]), bf(f1_up[0]), bf(f1_down[0]))
    h = x.reshape(nb * seq, d)
    for i in range(depth):
        h = _ffn_ln(h, *ffn1, row(ln1_g[i]), row(ln1_b[i]), tail=heads if i == 0 else None)
        pending = [f2_gate, f2_up, f2_down] + ([f1_gate, f1_up, f1_down] if i + 1 < depth else [])
        layers = [i] * 3 + [i + 1] * 3
        h, cast = _mixer(h, nb, nx, i, w_proj, b_fgate[i], pool_w[i], pool_scale[i], conv_w[i],
                         conv_b[i], dt_bias[i], a_log[i], d_skip[i], ssd_norm_w[i],
                         (flat(w_out), i, depth),
                         row(ln2_g[i]), row(ln2_b[i]),
                         [(flat(w), l, depth) for w, l in zip(pending, layers)])
        ffn2, ffn1 = cast[:3], (cast[3:] if i + 1 < depth else None)
        h = _ffn_ln(h, *ffn2, row(ln3_g[i]), row(ln3_b[i]),
                    rows=nb * seq if i == depth - 1 else None)
    return h.reshape(nb, seq, d)
```

```python
import functools

import jax
import jax.numpy as jnp
from jax import lax
from jax.experimental import pallas as pl
from jax.experimental.pallas import tpu as pltpu

F32 = jnp.float32
BF16 = jnp.bfloat16

D_MODEL = 2048
N_META = 16
CHUNK = 128
PAD = CHUNK - N_META
LANES = 128

ATTN_HEADS = 8
ATTN_HEAD_DIM = 64
D_ATTN = ATTN_HEADS * ATTN_HEAD_DIM
POOL_WINDOWS = (2, 4, 8, 16)
POOL_GROUPS = 4
D_POOL = 512
POOL_GROUP_DIM = D_POOL // POOL_GROUPS
POOL_HALO = 16
D_SSD = 1024
SSD_HEAD_DIM = 64
SSD_HEADS = D_SSD // SSD_HEAD_DIM
SSD_GROUPS = 2
SSD_STATE = 128
CONV_K = 4
CONV_HALO = 8
D_CONV = D_SSD + 2 * SSD_GROUPS * SSD_STATE
D_FF_TILE = 512
ROW_TILES = (512, 384, 256, 128)
DEPTH = 2
ALPHA = (2 * DEPTH) ** 0.25
LN_EPS = 1e-5
RMS_EPS = 1e-5
NEG_BIG = -1e30

PROJ_TILE = 512
REST_XBC, REST_POOL, REST_Z, REST_MISC = 0, D_CONV, D_CONV + D_POOL, D_CONV + D_POOL + D_SSD
D_REST = REST_MISC + LANES
MISC_F0 = 0
MISC_DT0 = ATTN_HEADS

VMEM_LIMIT = 56 * 1024 * 1024


def _params(*sem):
    return pltpu.CompilerParams(dimension_semantics=sem, vmem_limit_bytes=VMEM_LIMIT)


def _pick_tile(candidates, *sizes):
    for c in candidates:
        if all(n % c == 0 for n in sizes):
            return c
    raise ValueError(f"no tile in {candidates} divides {sizes}")


def _phys_block(b, j, nb, nx):
    return jnp.where(j == 0, nb * nx + b, b * nx + j - 1)


def _layer_norm(y, g, b):
    mu = jnp.mean(y, axis=-1, keepdims=True)
    yc = y - mu
    var = jnp.mean(yc * yc, axis=-1, keepdims=True)
    return yc * lax.rsqrt(var + LN_EPS) * g + b


def _silu(x):
    half = 0.5 * x
    return half + half * jnp.tanh(half)


def _softplus(x):
    return jnp.maximum(x, 0.0) + jnp.log1p(jnp.exp(-jnp.abs(x)))


def _cumsum_rows(x):
    n = x.shape[0]
    row = lax.broadcasted_iota(jnp.int32, x.shape, 0)
    d = 1
    while d < n:
        x = x + jnp.where(row >= d, pltpu.roll(x, d, axis=0), 0.0)
        d *= 2
    return x


def _ffn_ln_kernel(*refs, n_main_tiles):
    if n_main_tiles is None:
        x_ref, wg_ref, wu_ref, wd_ref, g_ref, b_ref, o_ref, xb_ref, acc_ref, y_ref = refs
        read_x = lambda: x_ref[...]
    else:
        x_ref, tail_ref, wg_ref, wu_ref, wd_ref, g_ref, b_ref, o_ref, xb_ref, acc_ref, y_ref = refs
        in_main = pl.program_id(0) < n_main_tiles
        read_x = lambda: jnp.where(in_main, x_ref[...], tail_ref[...])
    i, f = pl.program_id(0), pl.program_id(1)
    n_tiles, last = pl.num_programs(0) - 1, pl.num_programs(1) - 1

    def down_proj(xb):
        gate = jnp.dot(xb, wg_ref[...], preferred_element_type=F32)
        up = jnp.dot(xb, wu_ref[...], preferred_element_type=F32)
        act = (_silu(gate) * up).astype(BF16)
        return jnp.dot(act, wd_ref[...], preferred_element_type=F32)

    def normalize_previous():
        o_ref[...] = _layer_norm(y_ref[...], g_ref[...], b_ref[...])

    @pl.when((i == 0) & (f == 0))
    def _():
        y_ref[...] = jnp.zeros_like(y_ref)

    @pl.when((f == 0) & (i < n_tiles))
    def _():
        normalize_previous()
        xb = read_x().astype(BF16)
        xb_ref[...] = xb
        acc_ref[...] = down_proj(xb)

    @pl.when((f == 0) & (i == n_tiles))
    def _():
        normalize_previous()

    @pl.when((f > 0) & (f < last) & (i < n_tiles))
    def _():
        acc_ref[...] += down_proj(xb_ref[...])

    @pl.when((f == last) & (i < n_tiles))
    def _():
        y_ref[...] = ALPHA * read_x() + 0.5 * (acc_ref[...] + down_proj(xb_ref[...]))


def _ffn_ln(h, wg, wu, wd, g, b, *, tail=None, rows=None):
    d = h.shape[1]
    ff = wg.shape[1]
    t = h.shape[0] + (0 if tail is None else tail.shape[0])
    rows = t if rows is None else rows
    tm = _pick_tile((768,) + ROW_TILES, rows,
                    *((h.shape[0], tail.shape[0]) if tail is not None else ()))
    tf = _pick_tile((D_FF_TILE, 256, 128), ff)
    n_tiles, nf = rows // tm, ff // tf
    assert nf >= 2
    tile = lambda i: jnp.minimum(i, n_tiles - 1)
    col = lambda i, f: jnp.where(i == n_tiles, nf - 1, f)
    x_specs = [pl.BlockSpec((tm, d), lambda i, f: (tile(i), 0))]
    operands = [h]
    n_main = None
    if tail is not None:
        n_main = h.shape[0] // tm
        x_specs = [pl.BlockSpec((tm, d), lambda i, f: (jnp.minimum(i, n_main - 1), 0)),
                   pl.BlockSpec((tm, d), lambda i, f: (jnp.maximum(tile(i) - n_main, 0), 0))]
        operands = [h, tail]
    return pl.pallas_call(
        functools.partial(_ffn_ln_kernel, n_main_tiles=n_main),
        grid=(n_tiles + 1, nf),
        in_specs=x_specs + [
            pl.BlockSpec((d, tf), lambda i, f: (0, col(i, f))),
            pl.BlockSpec((d, tf), lambda i, f: (0, col(i, f))),
            pl.BlockSpec((tf, d), lambda i, f: (col(i, f), 0)),
            pl.BlockSpec((1, d), lambda i, f: (0, 0)),
            pl.BlockSpec((1, d), lambda i, f: (0, 0)),
        ],
        out_specs=pl.BlockSpec((tm, d), lambda i, f: (jnp.maximum(i - 1, 0), 0)),
        out_shape=jax.ShapeDtypeStruct((rows, d), F32),
        scratch_shapes=[pltpu.VMEM((tm, d), BF16), pltpu.VMEM((tm, d), F32),
                        pltpu.VMEM((tm, d), F32)],
        compiler_params=_params("arbitrary", "arbitrary"),
        name="ffn_ln",
    )(*operands, wg, wu, wd, g, b)


LOG2_E = 1.4426950408889634
Q_SCALE = LOG2_E * ATTN_HEAD_DIM ** -0.5
AUG_Q = ATTN_HEAD_DIM
VT_ROWS = ATTN_HEAD_DIM + 16
ATTN_TQ = 2 * CHUNK
ATTN_QK_LEAD = ATTN_HEADS
NT_DIMS = (((1,), (1,)), ((), ()))


def _split3(c):
    hi = c.astype(BF16).astype(F32)
    mid = (c - hi).astype(BF16).astype(F32)
    lo = (c - hi - mid).astype(BF16).astype(F32)
    return hi, mid, lo


def _attn_operands(misc, q, k, v, bias, carry, n_pad):
    row = lax.broadcasted_iota(jnp.int32, (CHUNK, LANES), 0)
    lane = lax.broadcasted_iota(jnp.int32, (CHUNK, LANES), 1)
    is_pad = row < n_pad
    x = misc + bias
    log_f = jnp.minimum(x, 0.0) - jnp.log1p(jnp.exp(-jnp.abs(x)))
    c = _cumsum_rows(jnp.where(is_pad, 0.0, log_f)) + carry
    parts = _split3(c * LOG2_E)
    k_pad = jnp.where(lane == AUG_Q + 3, NEG_BIG, 0.0)
    q_aug, k_aug, v_t = [], [], []
    for pair in range(ATTN_HEADS // 2):
        q2 = q[:, pair * LANES:(pair + 1) * LANES]
        k2 = k[:, pair * LANES:(pair + 1) * LANES]
        v2t = v[:, pair * LANES:(pair + 1) * LANES].T
        for sub in range(2):
            h = 2 * pair + sub
            qh = q2 if sub == 0 else pltpu.roll(q2, ATTN_HEAD_DIM, axis=1)
            kh = k2 if sub == 0 else pltpu.roll(k2, ATTN_HEAD_DIM, axis=1)
            q_extra = jnp.where((lane >= AUG_Q + 3) & (lane < AUG_Q + 6), 1.0, 0.0)
            k_extra = jnp.where((lane >= AUG_Q) & (lane < AUG_Q + 3), 1.0, 0.0)
            for n, part in enumerate(parts):
                col = jnp.broadcast_to(part[:, h:h + 1], (CHUNK, LANES))
                q_extra = jnp.where(lane == AUG_Q + n, col, q_extra)
                k_extra = jnp.where(lane == AUG_Q + 3 + n, -col, k_extra)
            q_aug.append(jnp.where(lane < AUG_Q, qh, q_extra).astype(BF16))
            k_aug.append(jnp.where(is_pad, k_pad, jnp.where(lane < AUG_Q, kh, k_extra)).astype(BF16))
            v_t.append(v2t[sub * ATTN_HEAD_DIM:(sub + 1) * ATTN_HEAD_DIM, :].astype(BF16))
    return q_aug, k_aug, v_t, c[CHUNK - 1:CHUNK, :]


def _proj_kernel(x_ref, wq_ref, wkv_ref, wxbc_ref, wpool_ref, wz_ref, wmisc_ref, bf_ref,
                 rest_ref, qa_ref, ka_ref, vt_ref, qah_ref, kah_ref, vth_ref,
                 carry_ref, head_carry_ref, *, tiles_per_batch):
    g = pl.program_id(0)
    is_head = g == 0
    blocks = x_ref.shape[0] // CHUNK

    @pl.when(is_head)
    def _():
        carry_ref[...] = jnp.zeros_like(carry_ref)
        head_carry_ref[...] = jnp.zeros_like(head_carry_ref)

    xb = x_ref[...].astype(BF16)
    proj = lambda w: lax.dot_general(xb, w, NT_DIMS, preferred_element_type=F32)
    q = proj(wq_ref[...])
    k = proj(wkv_ref[0:D_ATTN, :])
    v = proj(wkv_ref[D_ATTN:2 * D_ATTN, :])
    misc = proj(wmisc_ref[...])
    rest_ref[:, REST_MISC:REST_MISC + LANES] = misc

    tile = jnp.maximum(g - 1, 0)
    first_of_row = tile % tiles_per_batch == 0
    carry = jnp.where(first_of_row, head_carry_ref[pl.ds(tile // tiles_per_batch, 1), :],
                      carry_ref[...])
    n_pad = jnp.where(is_head, PAD, 0)
    ones_row = jnp.where(lax.broadcasted_iota(jnp.int32, (VT_ROWS - ATTN_HEAD_DIM, CHUNK), 0) == 0,
                         1.0, 0.0).astype(BF16)
    block_sums = []
    for r in range(blocks):
        rows = slice(r * CHUNK, (r + 1) * CHUNK)
        half = slice((r % 2) * CHUNK, (r % 2 + 1) * CHUNK)
        q_aug, k_aug, v_t, carry = _attn_operands(
            misc[rows], q[rows], k[rows], v[rows], bf_ref[...],
            jnp.where(is_head, 0.0, carry), n_pad)
        block_sums.append(carry)
        for h in range(ATTN_HEADS):
            qa_ref[0, h, rows, :] = q_aug[h]
            ka_ref[0, h, rows, :] = k_aug[h]
            vt_ref[0, r // 2, h, 0:ATTN_HEAD_DIM, half] = v_t[h]
            vt_ref[0, r // 2, h, ATTN_HEAD_DIM:VT_ROWS, half] = ones_row
    carry_ref[...] = carry

    for w_ref, col0 in ((wxbc_ref, REST_XBC), (wpool_ref, REST_POOL), (wz_ref, REST_Z)):
        for c0 in range(0, w_ref.shape[0], PROJ_TILE):
            rest_ref[:, col0 + c0:col0 + c0 + PROJ_TILE] = proj(w_ref[c0:c0 + PROJ_TILE, :])

    @pl.when(is_head)
    def _():
        for r in range(blocks):
            rows = slice(r * CHUNK, (r + 1) * CHUNK)
            half = slice((r % 2) * CHUNK, (r % 2 + 1) * CHUNK)
            head_carry_ref[r:r + 1, :] = block_sums[r]
            for h in range(ATTN_HEADS):
                qah_ref[r, h] = qa_ref[0, h, rows, :]
                kah_ref[r, h] = ka_ref[0, h, rows, :]
                vth_ref[r, h] = vt_ref[0, r // 2, h, :, half]


def _proj(h, ws, bias, layer, nb, nx):
    t, d = h.shape
    seq = nx * CHUNK
    tm = nb * CHUNK
    assert seq % tm == 0 and tm % ATTN_TQ == 0 and t == nb * seq + tm
    tpb = seq // tm
    n_main = nb * tpb
    x_tile = lambda g: jnp.maximum(g - 1, 0)
    resident = lambda w: pl.BlockSpec((None, w.shape[1], d), lambda g: (layer, 0, 0),
                                      pipeline_mode=pl.Buffered(1))
    whole = lambda shape: pl.BlockSpec(shape, lambda g: (0,) * len(shape))
    rows_of = lambda g: jnp.where(g == 0, n_main, g - 1)
    head_qk = (nb, ATTN_HEADS, CHUNK, LANES)
    head_vt = (nb, ATTN_HEADS, VT_ROWS, CHUNK)
    return pl.pallas_call(
        functools.partial(_proj_kernel, tiles_per_batch=tpb),
        grid=(n_main + 1,),
        in_specs=[pl.BlockSpec((tm, d), lambda g: (rows_of(g), 0))] + [resident(w) for w in ws]
        + [pl.BlockSpec((1, LANES), lambda g: (0, 0))],
        out_specs=[
            pl.BlockSpec((tm, D_REST), lambda g: (rows_of(g), 0)),
            pl.BlockSpec((1, ATTN_HEADS, tm, LANES),
                         lambda g: (x_tile(g) // tpb, 0, x_tile(g) % tpb, 0)),
            pl.BlockSpec((1, ATTN_HEADS, tm, LANES),
                         lambda g: (x_tile(g) // tpb, 0, x_tile(g) % tpb, 0)),
            pl.BlockSpec((1, tm // ATTN_TQ, ATTN_HEADS, VT_ROWS, ATTN_TQ),
                         lambda g: (x_tile(g) // tpb, x_tile(g) % tpb, 0, 0, 0)),
            whole(head_qk), whole(head_qk), whole(head_vt),
        ],
        out_shape=[
            jax.ShapeDtypeStruct((t, D_REST), F32),
            jax.ShapeDtypeStruct((nb, ATTN_HEADS, seq, LANES), BF16),
            jax.ShapeDtypeStruct((nb, ATTN_HEADS, seq, LANES), BF16),
            jax.ShapeDtypeStruct((nb, seq // ATTN_TQ, ATTN_HEADS, VT_ROWS, ATTN_TQ), BF16),
            jax.ShapeDtypeStruct(head_qk, BF16),
            jax.ShapeDtypeStruct(head_qk, BF16),
            jax.ShapeDtypeStruct(head_vt, BF16),
        ],
        scratch_shapes=[pltpu.VMEM((1, LANES), F32), pltpu.VMEM((nb, LANES), F32)],
        compiler_params=_params("arbitrary"),
        name="in_proj",
    )(h, *ws, bias)


def _softmax_step(st, m_old, mask):
    if mask is not None:
        st = jnp.where(mask, st, NEG_BIG)
    m_new = jnp.maximum(m_old, jnp.max(st, axis=0, keepdims=True))
    return m_new, jnp.exp2(m_old - m_new), jnp.exp2(st - m_new).astype(BF16)


def _write_heads(o_ref, accs):
    for pair in range(ATTN_HEADS // 2):
        halves = []
        for a in accs[2 * pair:2 * pair + 2]:
            halves.append(a[0:ATTN_HEAD_DIM, :] * (1.0 / a[ATTN_HEAD_DIM:ATTN_HEAD_DIM + 1, :]))
        o_ref[:, pair * LANES:(pair + 1) * LANES] = jnp.concatenate(halves, axis=0).T.astype(BF16)


def _attn_kernel(qa_ref, ka_ref, vt_ref, kah_ref, vth_ref, o_ref, acc_ref, st_ref, m_ref):
    i = pl.program_id(1)
    key = lax.broadcasted_iota(jnp.int32, (ATTN_TQ, ATTN_TQ), 0)
    qry = lax.broadcasted_iota(jnp.int32, (ATTN_TQ, ATTN_TQ), 1)
    acc_ref[...] = jnp.zeros_like(acc_ref)
    m_ref[...] = jnp.full(m_ref.shape, NEG_BIG, F32)

    def step(load_keys, nkeys, vt_chunks, mask=None):
        def qk(h):
            st_ref[h, 0:nkeys, :] = lax.dot_general(load_keys(h), qa_ref[0, h], NT_DIMS,
                                                    preferred_element_type=F32)

        def softmax_pv(h):
            m_new, rescale, p = _softmax_step(st_ref[h, 0:nkeys, :], m_ref[h], mask)
            m_ref[h] = m_new
            acc = rescale * acc_ref[h]
            r = 0
            for vt in vt_chunks(h):
                n = vt.shape[1]
                acc += jnp.dot(vt, p[r:r + n, :], preferred_element_type=F32)
                r += n
            acc_ref[h] = acc

        for h in range(ATTN_HEADS + ATTN_QK_LEAD):
            if h < ATTN_HEADS:
                qk(h)
            if h >= ATTN_QK_LEAD:
                softmax_pv(h - ATTN_QK_LEAD)

    def x_keys(start, nkeys):
        return lambda h: ka_ref[0, h, pl.ds(start, nkeys), :]

    step(lambda h: kah_ref[0, h], CHUNK, lambda h: (vth_ref[0, h],))

    def quad_step(s, _):
        step(x_keys(pl.multiple_of(s * 2 * ATTN_TQ, 2 * ATTN_TQ), 2 * ATTN_TQ), 2 * ATTN_TQ,
             lambda h: (vt_ref[0, 2 * s, h], vt_ref[0, 2 * s + 1, h]))
        return 0

    lax.fori_loop(0, i // 2, quad_step, 0)

    @pl.when(i % 2 == 1)
    def _():
        step(x_keys(pl.multiple_of((i - 1) * ATTN_TQ, ATTN_TQ), ATTN_TQ), ATTN_TQ,
             lambda h: (vt_ref[0, i - 1, h],))

    step(x_keys(pl.multiple_of(i * ATTN_TQ, ATTN_TQ), ATTN_TQ), ATTN_TQ,
         lambda h: (vt_ref[0, i, h],), key <= qry)
    _write_heads(o_ref.at[0], [acc_ref[h] for h in range(ATTN_HEADS)])


def _attention(qa, ka, vt, ka_h, vt_h):
    nb, _, seq, _ = qa.shape
    nq = seq // ATTN_TQ
    per_row = lambda shape: pl.BlockSpec((1,) + shape, lambda b, i: (b,) + (0,) * len(shape))
    return pl.pallas_call(
        _attn_kernel,
        grid=(nb, nq),
        in_specs=[
            pl.BlockSpec((1, ATTN_HEADS, ATTN_TQ, LANES), lambda b, i: (b, 0, i, 0)),
            per_row((ATTN_HEADS, seq, LANES)),
            per_row((nq, ATTN_HEADS, VT_ROWS, ATTN_TQ)),
            per_row((ATTN_HEADS, CHUNK, LANES)),
            per_row((ATTN_HEADS, VT_ROWS, CHUNK)),
        ],
        out_specs=pl.BlockSpec((1, ATTN_TQ, D_ATTN), lambda b, i: (b, i, 0)),
        out_shape=jax.ShapeDtypeStruct((nb, seq, D_ATTN), BF16),
        scratch_shapes=[pltpu.VMEM((ATTN_HEADS, VT_ROWS, ATTN_TQ), F32),
                        pltpu.VMEM((ATTN_HEADS, 2 * ATTN_TQ, ATTN_TQ), F32),
                        pltpu.VMEM((ATTN_HEADS, 1, ATTN_TQ), F32)],
        compiler_params=_params("parallel", "arbitrary"),
        name="fox_attention",
    )(qa, ka, vt, ka_h, vt_h)


def _head_attn_kernel(qah_ref, kah_ref, vth_ref, o_ref):
    key = lax.broadcasted_iota(jnp.int32, (CHUNK, CHUNK), 0)
    qry = lax.broadcasted_iota(jnp.int32, (CHUNK, CHUNK), 1)
    accs = []
    for h in range(ATTN_HEADS):
        st = lax.dot_general(kah_ref[0, h], qah_ref[0, h], NT_DIMS, preferred_element_type=F32)
        _, _, p = _softmax_step(st, jnp.full((1, CHUNK), NEG_BIG, F32), key <= qry)
        accs.append(jnp.dot(vth_ref[0, h], p, preferred_element_type=F32))
    _write_heads(o_ref, accs)


def _head_attention(qa_h, ka_h, vt_h):
    nb = qa_h.shape[0]
    per_row = lambda shape: pl.BlockSpec((1,) + shape, lambda b: (b,) + (0,) * len(shape))
    return pl.pallas_call(
        _head_attn_kernel,
        grid=(nb,),
        in_specs=[per_row((ATTN_HEADS, CHUNK, LANES)), per_row((ATTN_HEADS, CHUNK, LANES)),
                  per_row((ATTN_HEADS, VT_ROWS, CHUNK))],
        out_specs=pl.BlockSpec((CHUNK, D_ATTN), lambda b: (b, 0)),
        out_shape=jax.ShapeDtypeStruct((nb * CHUNK, D_ATTN), BF16),
        compiler_params=_params("parallel"),
        name="head_attention",
    )(qa_h, ka_h, vt_h)


def _expand_heads(v, expand_ref):
    parts = jnp.concatenate(_split3(v), axis=1).astype(BF16)
    return jnp.dot(parts, expand_ref[...], preferred_element_type=F32)


def _pool_chunk(c, u_ref, pw_ref, ps_ref, yb_ref, buf_ref):
    @pl.when(c == 0)
    def _():
        buf_ref[0:POOL_HALO, :] = jnp.zeros((POOL_HALO, D_POOL), F32)

    @pl.when(c > 0)
    def _():
        buf_ref[0:POOL_HALO, :] = buf_ref[CHUNK:CHUNK + POOL_HALO, :]

    buf_ref[POOL_HALO:POOL_HALO + CHUNK, :] = u_ref[...]

    @pl.when(c == 0)
    def _():
        buf_ref[POOL_HALO:POOL_HALO + PAD, :] = jnp.zeros((PAD, D_POOL), F32)

    seen = c * CHUNK - PAD + 1 + lax.broadcasted_iota(jnp.int32, (CHUNK, POOL_GROUP_DIM), 0)
    for g, w in enumerate(POOL_WINDOWS):
        lo, hi = g * POOL_GROUP_DIM, (g + 1) * POOL_GROUP_DIM
        assert w & (w - 1) == 0 and w <= POOL_HALO
        win = buf_ref[:, lo:hi]
        s = 1
        while s < w:
            win = win + pltpu.roll(win, s, axis=0)
            s *= 2
        win = win[POOL_HALO:, :]
        u = buf_ref[POOL_HALO:POOL_HALO + CHUNK, lo:hi]
        cnt = jnp.clip(seen, 1, w).astype(F32)
        diff = (win / cnt - u).astype(BF16)
        mixed = jnp.dot(diff, pw_ref[g], preferred_element_type=F32)
        yb_ref[:, lo:hi] = (mixed * ps_ref[:, lo:hi]).astype(BF16)


def _seq_mix_kernel(*refs, n_cast):
    (xbc_ref, z_ref, misc_ref, u_ref, cw_ref, cb_ref, dtb_ref, a_ref, dsk_ref, nw_ref, pw_ref,
     ps_ref, expand_ref) = refs[:13]
    cast_src = refs[13:13 + n_cast]
    yc_ref, yb_ref = refs[13 + n_cast:15 + n_cast]
    cast_dst = refs[15 + n_cast:15 + 2 * n_cast]
    state_ref, ext_ref, buf_ref = refs[15 + 2 * n_cast:]
    c = pl.program_id(1)
    for src_ref, dst_ref in zip(cast_src, cast_dst):
        dst_ref[...] = src_ref[...].astype(BF16)
    _pool_chunk(c, u_ref, pw_ref, ps_ref, yb_ref, buf_ref)

    @pl.when(c == 0)
    def _():
        state_ref[...] = jnp.zeros_like(state_ref)
        ext_ref[0:CONV_HALO, :] = jnp.zeros((CONV_HALO, D_CONV), F32)

    @pl.when(c > 0)
    def _():
        ext_ref[0:CONV_HALO, :] = ext_ref[CHUNK:CHUNK + CONV_HALO, :]

    ext_ref[CONV_HALO:CONV_HALO + CHUNK, :] = xbc_ref[...]

    @pl.when(c == 0)
    def _():
        ext_ref[CONV_HALO:CONV_HALO + PAD, :] = jnp.zeros((PAD, D_CONV), F32)

    conv = cb_ref[...]
    for k in range(CONV_K):
        off = CONV_HALO - (CONV_K - 1) + k
        conv = conv + cw_ref[k:k + 1, :] * ext_ref[off:off + CHUNK, :]
    xc = _silu(conv)
    xs = xc[:, :D_SSD]
    gn = SSD_GROUPS * SSD_STATE
    bm = xc[:, D_SSD:D_SSD + gn]
    cm = xc[:, D_SSD + gn:D_SSD + 2 * gn]

    row = lax.broadcasted_iota(jnp.int32, (CHUNK, CHUNK), 0)
    col = lax.broadcasted_iota(jnp.int32, (CHUNK, CHUNK), 1)
    dt = _softplus(misc_ref[...] + dtb_ref[...])
    dt = jnp.where(row >= jnp.where(c == 0, PAD, 0), dt, 0.0)
    a_cs = _cumsum_rows(dt * a_ref[...])
    a_cs_t = a_cs.T
    a_last = a_cs[CHUNK - 1:CHUNK, :]
    x_dt = xs * _expand_heads(dt, expand_ref)
    x_dt_b = x_dt.astype(BF16)
    decay_out = _expand_heads(jnp.exp2(a_cs), expand_ref)
    x_state = (x_dt * _expand_heads(jnp.exp2(a_last - a_cs), expand_ref)).astype(BF16)
    chunk_decay = decay_out[CHUNK - 1:CHUNK, :]

    causal = col <= row
    hpg = SSD_HEADS // SSD_GROUPS
    gw = hpg * SSD_HEAD_DIM
    for g in range(SSD_GROUPS):
        n0, n1 = g * SSD_STATE, (g + 1) * SSD_STATE
        cm_g = cm[:, n0:n1].astype(BF16)
        bm_g = bm[:, n0:n1]
        cb = lax.dot_general(cm_g, bm_g.astype(BF16), NT_DIMS, preferred_element_type=F32)
        st = state_ref[:, g * gw:(g + 1) * gw]
        y_off = jnp.dot(cm_g, st.astype(BF16), preferred_element_type=F32)
        y_g = y_off * decay_out[:, g * gw:(g + 1) * gw]
        diag = []
        for r in range(hpg):
            h = g * hpg + r
            lane = MISC_DT0 + h
            seg = a_cs[:, lane:lane + 1] - a_cs_t[lane:lane + 1, :]
            m = (cb * jnp.exp2(jnp.where(causal, seg, NEG_BIG))).astype(BF16)
            diag.append(jnp.dot(m, x_dt_b[:, h * SSD_HEAD_DIM:(h + 1) * SSD_HEAD_DIM],
                                preferred_element_type=F32))
        y_g = y_g + jnp.concatenate(diag, axis=1)
        new = jnp.dot(bm_g.T.astype(BF16), x_state[:, g * gw:(g + 1) * gw],
                      preferred_element_type=F32)
        state_ref[:, g * gw:(g + 1) * gw] = chunk_decay[:, g * gw:(g + 1) * gw] * st + new

        sl = slice(g * gw, (g + 1) * gw)
        y_g = y_g + xs[:, sl] * dsk_ref[:, sl]
        gy = y_g * _silu(z_ref[:, sl])
        ms = jnp.mean(gy * gy, axis=-1, keepdims=True)
        yc_ref[:, sl] = (gy * lax.rsqrt(ms + RMS_EPS) * nw_ref[:, sl]).astype(BF16)


def _seq_mix(rest, cw, cb, dtb, a_neg, dsk, nw, pw, ps, nb, nx, casts=()):
    t = rest.shape[0]
    steps = nb * (nx + 1)
    src = lax.broadcasted_iota(jnp.int32, (3 * LANES, D_SSD), 0) % LANES
    dst = lax.broadcasted_iota(jnp.int32, (3 * LANES, D_SSD), 1) // SSD_HEAD_DIM
    expand = (src == dst + MISC_DT0).astype(BF16)
    cast_in, cast_out, cast_shapes = [], [], []
    for arr, layer, depth in casts:
        r, cols = arr.shape[0] // depth, arr.shape[1]
        rb = next(k for k in range(16, r + 1, 16) if r % k == 0 and r // k <= steps)
        nblk = r // rb
        blk_of = lambda b, c, nblk=nblk: jnp.minimum(b * (nx + 1) + c, nblk - 1)
        cast_in.append(pl.BlockSpec((rb, cols),
                                    lambda b, c, o=layer * nblk, f=blk_of: (o + f(b, c), 0)))
        cast_out.append(pl.BlockSpec((rb, cols), lambda b, c, f=blk_of: (f(b, c), 0)))
        cast_shapes.append(jax.ShapeDtypeStruct((r, cols), BF16))
    vec = lambda n: pl.BlockSpec((1, n), lambda b, c: (0, 0))
    blk = lambda b, c: _phys_block(b, c, nb, nx)
    rows = lambda n, col0: pl.BlockSpec((CHUNK, n), lambda b, c: (blk(b, c), col0 // n))
    return pl.pallas_call(
        functools.partial(_seq_mix_kernel, n_cast=len(casts)),
        grid=(nb, nx + 1),
        in_specs=[
            rows(D_CONV, REST_XBC), rows(D_SSD, REST_Z), rows(LANES, REST_MISC),
            rows(D_POOL, REST_POOL),
            pl.BlockSpec((CONV_K, D_CONV), lambda b, c: (0, 0)),
            vec(D_CONV), vec(LANES), vec(LANES), vec(D_SSD), vec(D_SSD),
            pl.BlockSpec((POOL_GROUPS, POOL_GROUP_DIM, POOL_GROUP_DIM), lambda b, c: (0, 0, 0)),
            vec(D_POOL),
            pl.BlockSpec(expand.shape, lambda b, c: (0, 0)),
        ] + cast_in,
        out_specs=[rows(D_SSD, 0), rows(D_POOL, 0)] + cast_out,
        out_shape=[jax.ShapeDtypeStruct((t, D_SSD), BF16),
                   jax.ShapeDtypeStruct((t, D_POOL), BF16)] + cast_shapes,
        scratch_shapes=[pltpu.VMEM((SSD_STATE, D_SSD), F32),
                        pltpu.VMEM((CONV_HALO + CHUNK, D_CONV), F32),
                        pltpu.VMEM((POOL_HALO + CHUNK, D_POOL), F32)],
        compiler_params=_params("parallel", "arbitrary"),
        name="seq_mix",
    )(rest, rest, rest, rest, cw, cb, dtb, a_neg, dsk, nw, pw, ps, expand,
      *(a for a, _, _ in casts))


def _out_ln_kernel(ya_ref, yah_ref, yb_ref, yc_ref, h_ref, w_ref, g_ref, b_ref, o_ref, *,
                   n_main_tiles):
    tm = o_ref.shape[0]
    in_main = pl.program_id(0) < n_main_tiles
    for r in range(0, tm, tm // 2):
        rows = slice(r, r + tm // 2)
        ya = jnp.where(in_main, ya_ref[rows, :], yah_ref[rows, :])
        acc = jnp.dot(ya, w_ref[0:D_ATTN, :], preferred_element_type=F32)
        acc += jnp.dot(yb_ref[rows, :], w_ref[D_ATTN:D_ATTN + D_POOL, :],
                       preferred_element_type=F32)
        acc += jnp.dot(yc_ref[rows, :], w_ref[D_ATTN + D_POOL:, :], preferred_element_type=F32)
        o_ref[rows, :] = _layer_norm(ALPHA * h_ref[rows, :] + acc, g_ref[...], b_ref[...])


def _out_ln(ya, ya_h, yb, yc, h, w, g, b):
    t, d = h.shape
    tm = _pick_tile(ROW_TILES, ya.shape[0], ya_h.shape[0])
    n_main = ya.shape[0] // tm
    rows = lambda n: pl.BlockSpec((tm, n), lambda i: (i, 0))
    return pl.pallas_call(
        functools.partial(_out_ln_kernel, n_main_tiles=n_main),
        grid=(t // tm,),
        in_specs=[
            pl.BlockSpec((tm, D_ATTN), lambda i: (jnp.minimum(i, n_main - 1), 0)),
            pl.BlockSpec((tm, D_ATTN), lambda i: (jnp.maximum(i - n_main, 0), 0)),
            rows(D_POOL), rows(D_SSD), rows(d),
            pl.BlockSpec((d, d), lambda i: (0, 0)),
            pl.BlockSpec((1, d), lambda i: (0, 0)),
            pl.BlockSpec((1, d), lambda i: (0, 0)),
        ],
        out_specs=rows(d),
        out_shape=jax.ShapeDtypeStruct((t, d), F32),
        compiler_params=_params("parallel"),
        name="out_proj_ln",
    )(ya, ya_h, yb, yc, h, w, g, b)


def _lane_row(vals, offset):
    return jnp.zeros((1, LANES), F32).at[0, offset:offset + vals.shape[0]].set(vals.astype(F32))


def _proj_weight(w_in):
    wt = jnp.swapaxes(w_in, 1, 2)
    c = D_ATTN
    o = 3 * c + ATTN_HEADS
    f = wt[:, 3 * c:o]
    dt = wt[:, o + D_POOL + D_SSD + D_CONV:]
    pad = jnp.zeros((wt.shape[0], LANES - ATTN_HEADS - SSD_HEADS, wt.shape[2]), wt.dtype)
    segments = (wt[:, :c] * Q_SCALE,
                wt[:, c:3 * c],
                wt[:, o + D_POOL + D_SSD:o + D_POOL + D_SSD + D_CONV],
                wt[:, o:o + D_POOL],
                wt[:, o + D_POOL:o + D_POOL + D_SSD],
                jnp.concatenate([f, dt, pad], axis=1))
    return tuple(w.astype(BF16) for w in segments)


def _mixer(h, nb, nx, layer, w_proj, b_fgate, pool_w, pool_scale, conv_w, conv_b, dt_bias,
           a_log, d_skip, ssd_norm_w, w_out, ln_g, ln_b, casts):
    rest, qa, ka, vt, qa_h, ka_h, vt_h = _proj(h, w_proj, _lane_row(b_fgate, MISC_F0), layer,
                                               nb, nx)
    ya = _attention(qa, ka, vt, ka_h, vt_h).reshape(nb * nx * CHUNK, D_ATTN)
    ya_h = _head_attention(qa_h, ka_h, vt_h)
    yc, yb, *cast = _seq_mix(rest, conv_w, conv_b.reshape(1, D_CONV),
                             _lane_row(dt_bias, MISC_DT0),
                             _lane_row(-LOG2_E * jnp.exp(a_log.astype(F32)), MISC_DT0),
                             jnp.repeat(d_skip, SSD_HEAD_DIM).reshape(1, D_SSD),
                             ssd_norm_w.reshape(1, D_SSD), pool_w.astype(BF16),
                             pool_scale.reshape(1, D_POOL), nb, nx, casts + [w_out])
    return _out_ln(ya, ya_h, yb, yc, h, cast[-1], ln_g, ln_b), cast[:-1]


def kernel(x, meta, f1_gate, f1_up, f1_down, ln1_g, ln1_b, w_in, b_fgate, pool_w, pool_scale,
           conv_w, conv_b, dt_bias, a_log, d_skip, ssd_norm_w, w_out, ln2_g, ln2_b, f2_gate,
           f2_up, f2_down, ln3_g, ln3_b):
    nb, seq, d = x.shape
    assert d == D_MODEL and meta.shape == (N_META, D_MODEL) and seq % ATTN_TQ == 0
    nx = seq // CHUNK
    depth = f1_gate.shape[0]
    head = jnp.concatenate([jnp.zeros((PAD, d), x.dtype), meta.astype(x.dtype)], axis=0)
    heads = jnp.broadcast_to(head[None], (nb, CHUNK, d)).reshape(nb * CHUNK, d)
    row = lambda v: v.reshape(1, d)
    bf = lambda w: w.astype(BF16)
    w_proj = _proj_weight(w_in)
    flat = lambda w: w.reshape(depth * w.shape[1], w.shape[2])
    ffn1 = (bf(f1_gate[0]), bf(f1_up[0]), bf(f1_down[0]))
    h = x.reshape(nb * seq, d)
    for i in range(depth):
        h = _ffn_ln(h, *ffn1, row(ln1_g[i]), row(ln1_b[i]), tail=heads if i == 0 else None)
        pending = [f2_gate, f2_up, f2_down] + ([f1_gate, f1_up, f1_down] if i + 1 < depth else [])
        layers = [i] * 3 + [i + 1] * 3
        h, cast = _mixer(h, nb, nx, i, w_proj, b_fgate[i], pool_w[i], pool_scale[i], conv_w[i],
                         conv_b[i], dt_bias[i], a_log[i], d_skip[i], ssd_norm_w[i],
                         (flat(w_out), i, depth),
                         row(ln2_g[i]), row(ln2_b[i]),
                         [(flat(w), l, depth) for w, l in zip(pending, layers)])
        ffn2, ffn1 = cast[:3], (cast[3:] if i + 1 < depth else None)
        h = _ffn_ln(h, *ffn2, row(ln3_g[i]), row(ln3_b[i]),
                    rows=nb * seq if i == depth - 1 else None)
    return h.reshape(nb, seq, d)
```

```python
import functools

import jax
import jax.numpy as jnp
from jax import lax
from jax.experimental import pallas as pl
from jax.experimental.pallas import tpu as pltpu

F32 = jnp.float32
BF16 = jnp.bfloat16

D_MODEL = 2048
N_META = 16
CHUNK = 128
PAD = CHUNK - N_META
LANES = 128

ATTN_HEADS = 8
ATTN_HEAD_DIM = 64
D_ATTN = ATTN_HEADS * ATTN_HEAD_DIM
POOL_WINDOWS = (2, 4, 8, 16)
POOL_GROUPS = 4
D_POOL = 512
POOL_GROUP_DIM = D_POOL // POOL_GROUPS
POOL_HALO = 16
D_SSD = 1024
SSD_HEAD_DIM = 64
SSD_HEADS = D_SSD // SSD_HEAD_DIM
SSD_GROUPS = 2
SSD_STATE = 128
CONV_K = 4
CONV_HALO = 8
D_CONV = D_SSD + 2 * SSD_GROUPS * SSD_STATE
D_FF_TILE = 512
ROW_TILES = (512, 384, 256, 128)
DEPTH = 2
ALPHA = (2 * DEPTH) ** 0.25
LN_EPS = 1e-5
RMS_EPS = 1e-5
NEG_BIG = -1e30

PROJ_TILE = 512
REST_XBC, REST_POOL, REST_Z, REST_MISC = 0, D_CONV, D_CONV + D_POOL, D_CONV + D_POOL + D_SSD
D_REST = REST_MISC + LANES
MISC_F0 = 0
MISC_DT0 = ATTN_HEADS

VMEM_LIMIT = 56 * 1024 * 1024


def _params(*sem):
    return pltpu.CompilerParams(dimension_semantics=sem, vmem_limit_bytes=VMEM_LIMIT)


def _pick_tile(candidates, *sizes):
    for c in candidates:
        if all(n % c == 0 for n in sizes):
            return c
    raise ValueError(f"no tile in {candidates} divides {sizes}")


def _phys_block(b, j, nb, nx):
    return jnp.where(j == 0, nb * nx + b, b * nx + j - 1)


def _layer_norm(y, g, b):
    mu = jnp.mean(y, axis=-1, keepdims=True)
    yc = y - mu
    var = jnp.mean(yc * yc, axis=-1, keepdims=True)
    return yc * lax.rsqrt(var + LN_EPS) * g + b


def _silu(x):
    half = 0.5 * x
    return half + half * jnp.tanh(half)


def _softplus(x):
    return jnp.maximum(x, 0.0) + jnp.log1p(jnp.exp(-jnp.abs(x)))


def _cumsum_rows(x):
    n = x.shape[0]
    row = lax.broadcasted_iota(jnp.int32, x.shape, 0)
    d = 1
    while d < n:
        x = x + jnp.where(row >= d, pltpu.roll(x, d, axis=0), 0.0)
        d *= 2
    return x


def _ffn_ln_kernel(*refs, n_main_tiles):
    if n_main_tiles is None:
        x_ref, wg_ref, wu_ref, wd_ref, g_ref, b_ref, o_ref, xb_ref, acc_ref, y_ref = refs
        read_x = lambda: x_ref[...]
    else:
        x_ref, tail_ref, wg_ref, wu_ref, wd_ref, g_ref, b_ref, o_ref, xb_ref, acc_ref, y_ref = refs
        in_main = pl.program_id(0) < n_main_tiles
        read_x = lambda: jnp.where(in_main, x_ref[...], tail_ref[...])
    i, f = pl.program_id(0), pl.program_id(1)
    n_tiles, last = pl.num_programs(0) - 1, pl.num_programs(1) - 1

    def down_proj(xb):
        gate = jnp.dot(xb, wg_ref[...], preferred_element_type=F32)
        up = jnp.dot(xb, wu_ref[...], preferred_element_type=F32)
        act = (_silu(gate) * up).astype(BF16)
        return jnp.dot(act, wd_ref[...], preferred_element_type=F32)

    def normalize_previous():
        o_ref[...] = _layer_norm(y_ref[...], g_ref[...], b_ref[...])

    @pl.when((i == 0) & (f == 0))
    def _():
        y_ref[...] = jnp.zeros_like(y_ref)

    @pl.when((f == 0) & (i < n_tiles))
    def _():
        normalize_previous()
        xb = read_x().astype(BF16)
        xb_ref[...] = xb
        acc_ref[...] = down_proj(xb)

    @pl.when((f == 0) & (i == n_tiles))
    def _():
        normalize_previous()

    @pl.when((f > 0) & (f < last) & (i < n_tiles))
    def _():
        acc_ref[...] += down_proj(xb_ref[...])

    @pl.when((f == last) & (i < n_tiles))
    def _():
        y_ref[...] = ALPHA * read_x() + 0.5 * (acc_ref[...] + down_proj(xb_ref[...]))


def _ffn_ln(h, wg, wu, wd, g, b, *, tail=None, rows=None):
    d = h.shape[1]
    ff = wg.shape[1]
    t = h.shape[0] + (0 if tail is None else tail.shape[0])
    rows = t if rows is None else rows
    tm = _pick_tile((768,) + ROW_TILES, rows,
                    *((h.shape[0], tail.shape[0]) if tail is not None else ()))
    tf = _pick_tile((D_FF_TILE, 256, 128), ff)
    n_tiles, nf = rows // tm, ff // tf
    assert nf >= 2
    tile = lambda i: jnp.minimum(i, n_tiles - 1)
    col = lambda i, f: jnp.where(i == n_tiles, nf - 1, f)
    x_specs = [pl.BlockSpec((tm, d), lambda i, f: (tile(i), 0))]
    operands = [h]
    n_main = None
    if tail is not None:
        n_main = h.shape[0] // tm
        x_specs = [pl.BlockSpec((tm, d), lambda i, f: (jnp.minimum(i, n_main - 1), 0)),
                   pl.BlockSpec((tm, d), lambda i, f: (jnp.maximum(tile(i) - n_main, 0), 0))]
        operands = [h, tail]
    return pl.pallas_call(
        functools.partial(_ffn_ln_kernel, n_main_tiles=n_main),
        grid=(n_tiles + 1, nf),
        in_specs=x_specs + [
            pl.BlockSpec((d, tf), lambda i, f: (0, col(i, f))),
            pl.BlockSpec((d, tf), lambda i, f: (0, col(i, f))),
            pl.BlockSpec((tf, d), lambda i, f: (col(i, f), 0)),
            pl.BlockSpec((1, d), lambda i, f: (0, 0)),
            pl.BlockSpec((1, d), lambda i, f: (0, 0)),
        ],
        out_specs=pl.BlockSpec((tm, d), lambda i, f: (jnp.maximum(i - 1, 0), 0)),
        out_shape=jax.ShapeDtypeStruct((rows, d), F32),
        scratch_shapes=[pltpu.VMEM((tm, d), BF16), pltpu.VMEM((tm, d), F32),
                        pltpu.VMEM((tm, d), F32)],
        compiler_params=_params("arbitrary", "arbitrary"),
        name="ffn_ln",
    )(*operands, wg, wu, wd, g, b)


LOG2_E = 1.4426950408889634
Q_SCALE = LOG2_E * ATTN_HEAD_DIM ** -0.5
AUG_Q = ATTN_HEAD_DIM
VT_ROWS = ATTN_HEAD_DIM + 16
ATTN_TQ = 2 * CHUNK
ATTN_QK_LEAD = ATTN_HEADS
NT_DIMS = (((1,), (1,)), ((), ()))


def _split3(c):
    hi = c.astype(BF16).astype(F32)
    mid = (c - hi).astype(BF16).astype(F32)
    lo = (c - hi - mid).astype(BF16).astype(F32)
    return hi, mid, lo


def _attn_operands(misc, q, k, v, bias, carry, n_pad):
    row = lax.broadcasted_iota(jnp.int32, (CHUNK, LANES), 0)
    lane = lax.broadcasted_iota(jnp.int32, (CHUNK, LANES), 1)
    is_pad = row < n_pad
    x = misc + bias
    log_f = jnp.minimum(x, 0.0) - jnp.log1p(jnp.exp(-jnp.abs(x)))
    c = _cumsum_rows(jnp.where(is_pad, 0.0, log_f)) + carry
    parts = _split3(c * LOG2_E)
    k_pad = jnp.where(lane == AUG_Q + 3, NEG_BIG, 0.0)
    q_aug, k_aug, v_t = [], [], []
    for pair in range(ATTN_HEADS // 2):
        q2 = q[:, pair * LANES:(pair + 1) * LANES]
        k2 = k[:, pair * LANES:(pair + 1) * LANES]
        v2t = v[:, pair * LANES:(pair + 1) * LANES].T
        for sub in range(2):
            h = 2 * pair + sub
            qh = q2 if sub == 0 else pltpu.roll(q2, ATTN_HEAD_DIM, axis=1)
            kh = k2 if sub == 0 else pltpu.roll(k2, ATTN_HEAD_DIM, axis=1)
            q_extra = jnp.where((lane >= AUG_Q + 3) & (lane < AUG_Q + 6), 1.0, 0.0)
            k_extra = jnp.where((lane >= AUG_Q) & (lane < AUG_Q + 3), 1.0, 0.0)
            for n, part in enumerate(parts):
                col = jnp.broadcast_to(part[:, h:h + 1], (CHUNK, LANES))
                q_extra = jnp.where(lane == AUG_Q + n, col, q_extra)
                k_extra = jnp.where(lane == AUG_Q + 3 + n, -col, k_extra)
            q_aug.append(jnp.where(lane < AUG_Q, qh, q_extra).astype(BF16))
            k_aug.append(jnp.where(is_pad, k_pad, jnp.where(lane < AUG_Q, kh, k_extra)).astype(BF16))
            v_t.append(v2t[sub * ATTN_HEAD_DIM:(sub + 1) * ATTN_HEAD_DIM, :].astype(BF16))
    return q_aug, k_aug, v_t, c[CHUNK - 1:CHUNK, :]


def _proj_kernel(x_ref, wq_ref, wkv_ref, wxbc_ref, wpool_ref, wz_ref, wmisc_ref, bf_ref,
                 rest_ref, qa_ref, ka_ref, vt_ref, qah_ref, kah_ref, vth_ref,
                 carry_ref, head_carry_ref, *, tiles_per_batch):
    g = pl.program_id(0)
    is_head = g == 0
    blocks = x_ref.shape[0] // CHUNK

    @pl.when(is_head)
    def _():
        carry_ref[...] = jnp.zeros_like(carry_ref)
        head_carry_ref[...] = jnp.zeros_like(head_carry_ref)

    xb = x_ref[...].astype(BF16)
    proj = lambda w: lax.dot_general(xb, w, NT_DIMS, preferred_element_type=F32)
    q = proj(wq_ref[...])
    k = proj(wkv_ref[0:D_ATTN, :])
    v = proj(wkv_ref[D_ATTN:2 * D_ATTN, :])
    misc = proj(wmisc_ref[...])
    rest_ref[:, REST_MISC:REST_MISC + LANES] = misc

    tile = jnp.maximum(g - 1, 0)
    first_of_row = tile % tiles_per_batch == 0
    carry = jnp.where(first_of_row, head_carry_ref[pl.ds(tile // tiles_per_batch, 1), :],
                      carry_ref[...])
    n_pad = jnp.where(is_head, PAD, 0)
    ones_row = jnp.where(lax.broadcasted_iota(jnp.int32, (VT_ROWS - ATTN_HEAD_DIM, CHUNK), 0) == 0,
                         1.0, 0.0).astype(BF16)
    block_sums = []
    for r in range(blocks):
        rows = slice(r * CHUNK, (r + 1) * CHUNK)
        half = slice((r % 2) * CHUNK, (r % 2 + 1) * CHUNK)
        q_aug, k_aug, v_t, carry = _attn_operands(
            misc[rows], q[rows], k[rows], v[rows], bf_ref[...],
            jnp.where(is_head, 0.0, carry), n_pad)
        block_sums.append(carry)
        for h in range(ATTN_HEADS):
            qa_ref[0, h, rows, :] = q_aug[h]
            ka_ref[0, h, rows, :] = k_aug[h]
            vt_ref[0, r // 2, h, 0:ATTN_HEAD_DIM, half] = v_t[h]
            vt_ref[0, r // 2, h, ATTN_HEAD_DIM:VT_ROWS, half] = ones_row
    carry_ref[...] = carry

    for w_ref, col0 in ((wxbc_ref, REST_XBC), (wpool_ref, REST_POOL), (wz_ref, REST_Z)):
        for c0 in range(0, w_ref.shape[0], PROJ_TILE):
            rest_ref[:, col0 + c0:col0 + c0 + PROJ_TILE] = proj(w_ref[c0:c0 + PROJ_TILE, :])

    @pl.when(is_head)
    def _():
        for r in range(blocks):
            rows = slice(r * CHUNK, (r + 1) * CHUNK)
            half = slice((r % 2) * CHUNK, (r % 2 + 1) * CHUNK)
            head_carry_ref[r:r + 1, :] = block_sums[r]
            for h in range(ATTN_HEADS):
                qah_ref[r, h] = qa_ref[0, h, rows, :]
                kah_ref[r, h] = ka_ref[0, h, rows, :]
                vth_ref[r, h] = vt_ref[0, r // 2, h, :, half]


def _proj(h, ws, bias, layer, nb, nx):
    t, d = h.shape
    seq = nx * CHUNK
    tm = nb * CHUNK
    assert seq % tm == 0 and tm % ATTN_TQ == 0 and t == nb * seq + tm
    tpb = seq // tm
    n_main = nb * tpb
    x_tile = lambda g: jnp.maximum(g - 1, 0)
    resident = lambda w: pl.BlockSpec((None, w.shape[1], d), lambda g: (layer, 0, 0),
                                      pipeline_mode=pl.Buffered(1))
    whole = lambda shape: pl.BlockSpec(shape, lambda g: (0,) * len(shape))
    rows_of = lambda g: jnp.where(g == 0, n_main, g - 1)
    head_qk = (nb, ATTN_HEADS, CHUNK, LANES)
    head_vt = (nb, ATTN_HEADS, VT_ROWS, CHUNK)
    return pl.pallas_call(
        functools.partial(_proj_kernel, tiles_per_batch=tpb),
        grid=(n_main + 1,),
        in_specs=[pl.BlockSpec((tm, d), lambda g: (rows_of(g), 0))] + [resident(w) for w in ws]
        + [pl.BlockSpec((1, LANES), lambda g: (0, 0))],
        out_specs=[
            pl.BlockSpec((tm, D_REST), lambda g: (rows_of(g), 0)),
            pl.BlockSpec((1, ATTN_HEADS, tm, LANES),
                         lambda g: (x_tile(g) // tpb, 0, x_tile(g) % tpb, 0)),
            pl.BlockSpec((1, ATTN_HEADS, tm, LANES),
                         lambda g: (x_tile(g) // tpb, 0, x_tile(g) % tpb, 0)),
            pl.BlockSpec((1, tm // ATTN_TQ, ATTN_HEADS, VT_ROWS, ATTN_TQ),
                         lambda g: (x_tile(g) // tpb, x_tile(g) % tpb, 0, 0, 0)),
            whole(head_qk), whole(head_qk), whole(head_vt),
        ],
        out_shape=[
            jax.ShapeDtypeStruct((t, D_REST), F32),
            jax.ShapeDtypeStruct((nb, ATTN_HEADS, seq, LANES), BF16),
            jax.ShapeDtypeStruct((nb, ATTN_HEADS, seq, LANES), BF16),
            jax.ShapeDtypeStruct((nb, seq // ATTN_TQ, ATTN_HEADS, VT_ROWS, ATTN_TQ), BF16),
            jax.ShapeDtypeStruct(head_qk, BF16),
            jax.ShapeDtypeStruct(head_qk, BF16),
            jax.ShapeDtypeStruct(head_vt, BF16),
        ],
        scratch_shapes=[pltpu.VMEM((1, LANES), F32), pltpu.VMEM((nb, LANES), F32)],
        compiler_params=_params("arbitrary"),
        name="in_proj",
    )(h, *ws, bias)


def _softmax_step(st, m_old, mask):
    if mask is not None:
        st = jnp.where(mask, st, NEG_BIG)
    m_new = jnp.maximum(m_old, jnp.max(st, axis=0, keepdims=True))
    return m_new, jnp.exp2(m_old - m_new), jnp.exp2(st - m_new).astype(BF16)


def _write_heads(o_ref, accs):
    for pair in range(ATTN_HEADS // 2):
        halves = []
        for a in accs[2 * pair:2 * pair + 2]:
            halves.append(a[0:ATTN_HEAD_DIM, :] * (1.0 / a[ATTN_HEAD_DIM:ATTN_HEAD_DIM + 1, :]))
        o_ref[:, pair * LANES:(pair + 1) * LANES] = jnp.concatenate(halves, axis=0).T.astype(BF16)


def _attn_kernel(qa_ref, ka_ref, vt_ref, kah_ref, vth_ref, o_ref, acc_ref, st_ref, m_ref):
    i = pl.program_id(1)
    acc_ref[...] = jnp.zeros_like(acc_ref)
    m_ref[...] = jnp.full(m_ref.shape, NEG_BIG, F32)

    def step(segments, mask_from=None):
        total = sum(n for _, n, _ in segments)
        mask = None
        if mask_from is not None:
            key = lax.broadcasted_iota(jnp.int32, (total, ATTN_TQ), 0)
            qry = lax.broadcasted_iota(jnp.int32, (total, ATTN_TQ), 1)
            mask = key - mask_from <= qry

        def qk(h):
            r = 0
            for load_keys, n, _ in segments:
                st_ref[h, r:r + n, :] = lax.dot_general(load_keys(h), qa_ref[0, h], NT_DIMS,
                                                        preferred_element_type=F32)
                r += n

        def softmax_pv(h):
            m_new, rescale, p = _softmax_step(st_ref[h, 0:total, :], m_ref[h], mask)
            m_ref[h] = m_new
            acc = rescale * acc_ref[h]
            r = 0
            for _, _, vt_chunks in segments:
                for vt in vt_chunks(h):
                    n = vt.shape[1]
                    acc += jnp.dot(vt, p[r:r + n, :], preferred_element_type=F32)
                    r += n
            acc_ref[h] = acc

        for h in range(ATTN_HEADS + ATTN_QK_LEAD):
            if h < ATTN_HEADS:
                qk(h)
            if h >= ATTN_QK_LEAD:
                softmax_pv(h - ATTN_QK_LEAD)

    def x_keys(pair, npairs):
        start = pl.multiple_of(pair * ATTN_TQ, ATTN_TQ)
        return (lambda h: ka_ref[0, h, pl.ds(start, npairs * ATTN_TQ), :], npairs * ATTN_TQ,
                lambda h: tuple(vt_ref[0, pair + k, h] for k in range(npairs)))

    head_keys = (lambda h: kah_ref[0, h], CHUNK, lambda h: (vth_ref[0, h],))

    def quad_step(s, _):
        step([x_keys(2 * s, 2)])
        return 0

    lax.fori_loop(0, i // 2, quad_step, 0)

    @pl.when(i % 2 == 0)
    def _():
        step([head_keys, x_keys(i, 1)], mask_from=CHUNK)

    @pl.when(i % 2 == 1)
    def _():
        step([head_keys, x_keys(i - 1, 2)], mask_from=CHUNK + ATTN_TQ)

    _write_heads(o_ref.at[0], [acc_ref[h] for h in range(ATTN_HEADS)])


def _attention(qa, ka, vt, ka_h, vt_h):
    nb, _, seq, _ = qa.shape
    nq = seq // ATTN_TQ
    per_row = lambda shape: pl.BlockSpec((1,) + shape, lambda b, i: (b,) + (0,) * len(shape))
    return pl.pallas_call(
        _attn_kernel,
        grid=(nb, nq),
        in_specs=[
            pl.BlockSpec((1, ATTN_HEADS, ATTN_TQ, LANES), lambda b, i: (b, 0, i, 0)),
            per_row((ATTN_HEADS, seq, LANES)),
            per_row((nq, ATTN_HEADS, VT_ROWS, ATTN_TQ)),
            per_row((ATTN_HEADS, CHUNK, LANES)),
            per_row((ATTN_HEADS, VT_ROWS, CHUNK)),
        ],
        out_specs=pl.BlockSpec((1, ATTN_TQ, D_ATTN), lambda b, i: (b, i, 0)),
        out_shape=jax.ShapeDtypeStruct((nb, seq, D_ATTN), BF16),
        scratch_shapes=[pltpu.VMEM((ATTN_HEADS, VT_ROWS, ATTN_TQ), F32),
                        pltpu.VMEM((ATTN_HEADS, CHUNK + 2 * ATTN_TQ, ATTN_TQ), F32),
                        pltpu.VMEM((ATTN_HEADS, 1, ATTN_TQ), F32)],
        compiler_params=_params("parallel", "arbitrary"),
        name="fox_attention",
    )(qa, ka, vt, ka_h, vt_h)


def _head_attn_kernel(qah_ref, kah_ref, vth_ref, o_ref):
    key = lax.broadcasted_iota(jnp.int32, (CHUNK, CHUNK), 0)
    qry = lax.broadcasted_iota(jnp.int32, (CHUNK, CHUNK), 1)
    accs = []
    for h in range(ATTN_HEADS):
        st = lax.dot_general(kah_ref[0, h], qah_ref[0, h], NT_DIMS, preferred_element_type=F32)
        _, _, p = _softmax_step(st, jnp.full((1, CHUNK), NEG_BIG, F32), key <= qry)
        accs.append(jnp.dot(vth_ref[0, h], p, preferred_element_type=F32))
    _write_heads(o_ref, accs)


def _head_attention(qa_h, ka_h, vt_h):
    nb = qa_h.shape[0]
    per_row = lambda shape: pl.BlockSpec((1,) + shape, lambda b: (b,) + (0,) * len(shape))
    return pl.pallas_call(
        _head_attn_kernel,
        grid=(nb,),
        in_specs=[per_row((ATTN_HEADS, CHUNK, LANES)), per_row((ATTN_HEADS, CHUNK, LANES)),
                  per_row((ATTN_HEADS, VT_ROWS, CHUNK))],
        out_specs=pl.BlockSpec((CHUNK, D_ATTN), lambda b: (b, 0)),
        out_shape=jax.ShapeDtypeStruct((nb * CHUNK, D_ATTN), BF16),
        compiler_params=_params("parallel"),
        name="head_attention",
    )(qa_h, ka_h, vt_h)


def _expand_heads(v, expand_ref):
    parts = jnp.concatenate(_split3(v), axis=1).astype(BF16)
    return jnp.dot(parts, expand_ref[...], preferred_element_type=F32)


def _pool_chunk(c, u_ref, pw_ref, ps_ref, yb_ref, buf_ref):
    @pl.when(c == 0)
    def _():
        buf_ref[0:POOL_HALO, :] = jnp.zeros((POOL_HALO, D_POOL), F32)

    @pl.when(c > 0)
    def _():
        buf_ref[0:POOL_HALO, :] = buf_ref[CHUNK:CHUNK + POOL_HALO, :]

    buf_ref[POOL_HALO:POOL_HALO + CHUNK, :] = u_ref[...]

    @pl.when(c == 0)
    def _():
        buf_ref[POOL_HALO:POOL_HALO + PAD, :] = jnp.zeros((PAD, D_POOL), F32)

    seen = c * CHUNK - PAD + 1 + lax.broadcasted_iota(jnp.int32, (CHUNK, POOL_GROUP_DIM), 0)
    for g, w in enumerate(POOL_WINDOWS):
        lo, hi = g * POOL_GROUP_DIM, (g + 1) * POOL_GROUP_DIM
        assert w & (w - 1) == 0 and w <= POOL_HALO
        win = buf_ref[:, lo:hi]
        s = 1
        while s < w:
            win = win + pltpu.roll(win, s, axis=0)
            s *= 2
        win = win[POOL_HALO:, :]
        u = buf_ref[POOL_HALO:POOL_HALO + CHUNK, lo:hi]
        cnt = jnp.clip(seen, 1, w).astype(F32)
        diff = (win / cnt - u).astype(BF16)
        mixed = jnp.dot(diff, pw_ref[g], preferred_element_type=F32)
        yb_ref[:, lo:hi] = (mixed * ps_ref[:, lo:hi]).astype(BF16)


def _seq_mix_kernel(*refs, n_cast):
    (xbc_ref, z_ref, misc_ref, u_ref, cw_ref, cb_ref, dtb_ref, a_ref, dsk_ref, nw_ref, pw_ref,
     ps_ref, expand_ref) = refs[:13]
    cast_src = refs[13:13 + n_cast]
    yc_ref, yb_ref = refs[13 + n_cast:15 + n_cast]
    cast_dst = refs[15 + n_cast:15 + 2 * n_cast]
    state_ref, ext_ref, buf_ref = refs[15 + 2 * n_cast:]
    c = pl.program_id(1)
    for src_ref, dst_ref in zip(cast_src, cast_dst):
        dst_ref[...] = src_ref[...].astype(BF16)
    _pool_chunk(c, u_ref, pw_ref, ps_ref, yb_ref, buf_ref)

    @pl.when(c == 0)
    def _():
        state_ref[...] = jnp.zeros_like(state_ref)
        ext_ref[0:CONV_HALO, :] = jnp.zeros((CONV_HALO, D_CONV), F32)

    @pl.when(c > 0)
    def _():
        ext_ref[0:CONV_HALO, :] = ext_ref[CHUNK:CHUNK + CONV_HALO, :]

    ext_ref[CONV_HALO:CONV_HALO + CHUNK, :] = xbc_ref[...]

    @pl.when(c == 0)
    def _():
        ext_ref[CONV_HALO:CONV_HALO + PAD, :] = jnp.zeros((PAD, D_CONV), F32)

    conv = cb_ref[...]
    for k in range(CONV_K):
        off = CONV_HALO - (CONV_K - 1) + k
        conv = conv + cw_ref[k:k + 1, :] * ext_ref[off:off + CHUNK, :]
    xc = _silu(conv)
    xs = xc[:, :D_SSD]
    gn = SSD_GROUPS * SSD_STATE
    bm = xc[:, D_SSD:D_SSD + gn]
    cm = xc[:, D_SSD + gn:D_SSD + 2 * gn]

    row = lax.broadcasted_iota(jnp.int32, (CHUNK, CHUNK), 0)
    col = lax.broadcasted_iota(jnp.int32, (CHUNK, CHUNK), 1)
    dt = _softplus(misc_ref[...] + dtb_ref[...])
    dt = jnp.where(row >= jnp.where(c == 0, PAD, 0), dt, 0.0)
    a_cs = _cumsum_rows(dt * a_ref[...])
    a_cs_t = a_cs.T
    a_last = a_cs[CHUNK - 1:CHUNK, :]
    x_dt = xs * _expand_heads(dt, expand_ref)
    x_dt_b = x_dt.astype(BF16)
    decay_out = _expand_heads(jnp.exp2(a_cs), expand_ref)
    x_state = (x_dt * _expand_heads(jnp.exp2(a_last - a_cs), expand_ref)).astype(BF16)
    chunk_decay = decay_out[CHUNK - 1:CHUNK, :]

    causal = col <= row
    hpg = SSD_HEADS // SSD_GROUPS
    gw = hpg * SSD_HEAD_DIM
    for g in range(SSD_GROUPS):
        n0, n1 = g * SSD_STATE, (g + 1) * SSD_STATE
        cm_g = cm[:, n0:n1].astype(BF16)
        bm_g = bm[:, n0:n1]
        cb = lax.dot_general(cm_g, bm_g.astype(BF16), NT_DIMS, preferred_element_type=F32)
        st = state_ref[:, g * gw:(g + 1) * gw]
        y_off = jnp.dot(cm_g, st.astype(BF16), preferred_element_type=F32)
        y_g = y_off * decay_out[:, g * gw:(g + 1) * gw]
        diag = []
        for r in range(hpg):
            h = g * hpg + r
            lane = MISC_DT0 + h
            seg = a_cs[:, lane:lane + 1] - a_cs_t[lane:lane + 1, :]
            m = (cb * jnp.exp2(jnp.where(causal, seg, NEG_BIG))).astype(BF16)
            diag.append(jnp.dot(m, x_dt_b[:, h * SSD_HEAD_DIM:(h + 1) * SSD_HEAD_DIM],
                                preferred_element_type=F32))
        y_g = y_g + jnp.concatenate(diag, axis=1)
        new = jnp.dot(bm_g.T.astype(BF16), x_state[:, g * gw:(g + 1) * gw],
                      preferred_element_type=F32)
        state_ref[:, g * gw:(g + 1) * gw] = chunk_decay[:, g * gw:(g + 1) * gw] * st + new

        sl = slice(g * gw, (g + 1) * gw)
        y_g = y_g + xs[:, sl] * dsk_ref[:, sl]
        gy = y_g * _silu(z_ref[:, sl])
        ms = jnp.mean(gy * gy, axis=-1, keepdims=True)
        yc_ref[:, sl] = (gy * lax.rsqrt(ms + RMS_EPS) * nw_ref[:, sl]).astype(BF16)


def _seq_mix(rest, cw, cb, dtb, a_neg, dsk, nw, pw, ps, nb, nx, casts=()):
    t = rest.shape[0]
    steps = nb * (nx + 1)
    src = lax.broadcasted_iota(jnp.int32, (3 * LANES, D_SSD), 0) % LANES
    dst = lax.broadcasted_iota(jnp.int32, (3 * LANES, D_SSD), 1) // SSD_HEAD_DIM
    expand = (src == dst + MISC_DT0).astype(BF16)
    cast_in, cast_out, cast_shapes = [], [], []
    for arr, layer, depth in casts:
        r, cols = arr.shape[0] // depth, arr.shape[1]
        rb = next(k for k in range(16, r + 1, 16) if r % k == 0 and r // k <= steps)
        nblk = r // rb
        blk_of = lambda b, c, nblk=nblk: jnp.minimum(b * (nx + 1) + c, nblk - 1)
        cast_in.append(pl.BlockSpec((rb, cols),
                                    lambda b, c, o=layer * nblk, f=blk_of: (o + f(b, c), 0)))
        cast_out.append(pl.BlockSpec((rb, cols), lambda b, c, f=blk_of: (f(b, c), 0)))
        cast_shapes.append(jax.ShapeDtypeStruct((r, cols), BF16))
    vec = lambda n: pl.BlockSpec((1, n), lambda b, c: (0, 0))
    blk = lambda b, c: _phys_block(b, c, nb, nx)
    rows = lambda n, col0: pl.BlockSpec((CHUNK, n), lambda b, c: (blk(b, c), col0 // n))
    return pl.pallas_call(
        functools.partial(_seq_mix_kernel, n_cast=len(casts)),
        grid=(nb, nx + 1),
        in_specs=[
            rows(D_CONV, REST_XBC), rows(D_SSD, REST_Z), rows(LANES, REST_MISC),
            rows(D_POOL, REST_POOL),
            pl.BlockSpec((CONV_K, D_CONV), lambda b, c: (0, 0)),
            vec(D_CONV), vec(LANES), vec(LANES), vec(D_SSD), vec(D_SSD),
            pl.BlockSpec((POOL_GROUPS, POOL_GROUP_DIM, POOL_GROUP_DIM), lambda b, c: (0, 0, 0)),
            vec(D_POOL),
            pl.BlockSpec(expand.shape, lambda b, c: (0, 0)),
        ] + cast_in,
        out_specs=[rows(D_SSD, 0), rows(D_POOL, 0)] + cast_out,
        out_shape=[jax.ShapeDtypeStruct((t, D_SSD), BF16),
                   jax.ShapeDtypeStruct((t, D_POOL), BF16)] + cast_shapes,
        scratch_shapes=[pltpu.VMEM((SSD_STATE, D_SSD), F32),
                        pltpu.VMEM((CONV_HALO + CHUNK, D_CONV), F32),
                        pltpu.VMEM((POOL_HALO + CHUNK, D_POOL), F32)],
        compiler_params=_params("parallel", "arbitrary"),
        name="seq_mix",
    )(rest, rest, rest, rest, cw, cb, dtb, a_neg, dsk, nw, pw, ps, expand,
      *(a for a, _, _ in casts))


def _out_ln_kernel(ya_ref, yah_ref, yb_ref, yc_ref, h_ref, w_ref, g_ref, b_ref, o_ref, *,
                   n_main_tiles):
    tm = o_ref.shape[0]
    in_main = pl.program_id(0) < n_main_tiles
    for r in range(0, tm, tm // 2):
        rows = slice(r, r + tm // 2)
        ya = jnp.where(in_main, ya_ref[rows, :], yah_ref[rows, :])
        acc = jnp.dot(ya, w_ref[0:D_ATTN, :], preferred_element_type=F32)
        acc += jnp.dot(yb_ref[rows, :], w_ref[D_ATTN:D_ATTN + D_POOL, :],
                       preferred_element_type=F32)
        acc += jnp.dot(yc_ref[rows, :], w_ref[D_ATTN + D_POOL:, :], preferred_element_type=F32)
        o_ref[rows, :] = _layer_norm(ALPHA * h_ref[rows, :] + acc, g_ref[...], b_ref[...])


def _out_ln(ya, ya_h, yb, yc, h, w, g, b):
    t, d = h.shape
    tm = _pick_tile(ROW_TILES, ya.shape[0], ya_h.shape[0])
    n_main = ya.shape[0] // tm
    rows = lambda n: pl.BlockSpec((tm, n), lambda i: (i, 0))
    return pl.pallas_call(
        functools.partial(_out_ln_kernel, n_main_tiles=n_main),
        grid=(t // tm,),
        in_specs=[
            pl.BlockSpec((tm, D_ATTN), lambda i: (jnp.minimum(i, n_main - 1), 0)),
            pl.BlockSpec((tm, D_ATTN), lambda i: (jnp.maximum(i - n_main, 0), 0)),
            rows(D_POOL), rows(D_SSD), rows(d),
            pl.BlockSpec((d, d), lambda i: (0, 0)),
            pl.BlockSpec((1, d), lambda i: (0, 0)),
            pl.BlockSpec((1, d), lambda i: (0, 0)),
        ],
        out_specs=rows(d),
        out_shape=jax.ShapeDtypeStruct((t, d), F32),
        compiler_params=_params("parallel"),
        name="out_proj_ln",
    )(ya, ya_h, yb, yc, h, w, g, b)


def _lane_row(vals, offset):
    return jnp.zeros((1, LANES), F32).at[0, offset:offset + vals.shape[0]].set(vals.astype(F32))


def _proj_weight(w_in):
    wt = jnp.swapaxes(w_in, 1, 2)
    c = D_ATTN
    o = 3 * c + ATTN_HEADS
    f = wt[:, 3 * c:o]
    dt = wt[:, o + D_POOL + D_SSD + D_CONV:]
    pad = jnp.zeros((wt.shape[0], LANES - ATTN_HEADS - SSD_HEADS, wt.shape[2]), wt.dtype)
    segments = (wt[:, :c] * Q_SCALE,
                wt[:, c:3 * c],
                wt[:, o + D_POOL + D_SSD:o + D_POOL + D_SSD + D_CONV],
                wt[:, o:o + D_POOL],
                wt[:, o + D_POOL:o + D_POOL + D_SSD],
                jnp.concatenate([f, dt, pad], axis=1))
    return tuple(w.astype(BF16) for w in segments)


def _mixer(h, nb, nx, layer, w_proj, b_fgate, pool_w, pool_scale, conv_w, conv_b, dt_bias,
           a_log, d_skip, ssd_norm_w, w_out, ln_g, ln_b, casts):
    rest, qa, ka, vt, qa_h, ka_h, vt_h = _proj(h, w_proj, _lane_row(b_fgate, MISC_F0), layer,
                                               nb, nx)
    ya = _attention(qa, ka, vt, ka_h, vt_h).reshape(nb * nx * CHUNK, D_ATTN)
    ya_h = _head_attention(qa_h, ka_h, vt_h)
    yc, yb, *cast = _seq_mix(rest, conv_w, conv_b.reshape(1, D_CONV),
                             _lane_row(dt_bias, MISC_DT0),
                             _lane_row(-LOG2_E * jnp.exp(a_log.astype(F32)), MISC_DT0),
                             jnp.repeat(d_skip, SSD_HEAD_DIM).reshape(1, D_SSD),
                             ssd_norm_w.reshape(1, D_SSD), pool_w.astype(BF16),
                             pool_scale.reshape(1, D_POOL), nb, nx, casts + [w_out])
    return _out_ln(ya, ya_h, yb, yc, h, cast[-1], ln_g, ln_b), cast[:-1]


def kernel(x, meta, f1_gate, f1_up, f1_down, ln1_g, ln1_b, w_in, b_fgate, pool_w, pool_scale,
           conv_w, conv_b, dt_bias, a_log, d_skip, ssd_norm_w, w_out, ln2_g, ln2_b, f2_gate,
           f2_up, f2_down, ln3_g, ln3_b):
    nb, seq, d = x.shape
    assert d == D_MODEL and meta.shape == (N_META, D_MODEL) and seq % ATTN_TQ == 0
    nx = seq // CHUNK
    depth = f1_gate.shape[0]
    assert depth == DEPTH
    head = jnp.concatenate([jnp.zeros((PAD, d), x.dtype), meta.astype(x.dtype)], axis=0)
    heads = jnp.broadcast_to(head[None], (nb, CHUNK, d)).reshape(nb * CHUNK, d)
    row = lambda v: v.reshape(1, d)
    bf = lambda w: w.astype(BF16)
    w_proj = _proj_weight(w_in)
    flat = lambda w: w.reshape(depth * w.shape[1], w.shape[2])
    ffn1 = (bf(f1_gate[0]), bf(f1_up[0]), bf(f1_down[0]))
    h = x.reshape(nb * seq, d)
    for i in range(depth):
        h = _ffn_ln(h, *ffn1, row(ln1_g[i]), row(ln1_b[i]), tail=heads if i == 0 else None)
        pending = [f2_gate, f2_up, f2_down] + ([f1_gate, f1_up, f1_down] if i + 1 < depth else [])
        layers = [i] * 3 + [i + 1] * 3
        h, cast = _mixer(h, nb, nx, i, w_proj, b_fgate[i], pool_w[i], pool_scale[i], conv_w[i],
                         conv_b[i], dt_bias[i], a_log[i], d_skip[i], ssd_norm_w[i],
                         (flat(w_out), i, depth),
                         row(ln2_g[i]), row(ln2_b[i]),
                         [(flat(w), l, depth) for w, l in zip(pending, layers)])
        ffn2, ffn1 = cast[:3], (cast[3:] if i + 1 < depth else None)
        h = _ffn_ln(h, *ffn2, row(ln3_g[i]), row(ln3_b[i]),
                    rows=nb * seq if i == depth - 1 else None)
    return h.reshape(nb, seq, d)
```

```python
import functools

import jax
import jax.numpy as jnp
from jax import lax
from jax.experimental import pallas as pl
from jax.experimental.pallas import tpu as pltpu

F32 = jnp.float32
BF16 = jnp.bfloat16

D_MODEL = 2048
N_META = 16
CHUNK = 128
PAD = CHUNK - N_META
LANES = 128

ATTN_HEADS = 8
ATTN_HEAD_DIM = 64
D_ATTN = ATTN_HEADS * ATTN_HEAD_DIM
POOL_WINDOWS = (2, 4, 8, 16)
POOL_GROUPS = 4
D_POOL = 512
POOL_GROUP_DIM = D_POOL // POOL_GROUPS
POOL_HALO = 16
D_SSD = 1024
SSD_HEAD_DIM = 64
SSD_HEADS = D_SSD // SSD_HEAD_DIM
SSD_GROUPS = 2
SSD_STATE = 128
CONV_K = 4
CONV_HALO = 8
D_CONV = D_SSD + 2 * SSD_GROUPS * SSD_STATE
D_FF_TILE = 512
FFN_LN_STEPS = 4
ROW_TILES = (512, 384, 256, 128)
DEPTH = 2
ALPHA = (2 * DEPTH) ** 0.25
LN_EPS = 1e-5
RMS_EPS = 1e-5
NEG_BIG = -1e30

PROJ_TILE = 512
REST_XBC, REST_POOL, REST_Z, REST_MISC = 0, D_CONV, D_CONV + D_POOL, D_CONV + D_POOL + D_SSD
D_REST = REST_MISC + LANES
MISC_F0 = 0
MISC_DT0 = ATTN_HEADS

VMEM_LIMIT = 56 * 1024 * 1024


def _params(*sem):
    return pltpu.CompilerParams(dimension_semantics=sem, vmem_limit_bytes=VMEM_LIMIT)


def _pick_tile(candidates, *sizes):
    for c in candidates:
        if all(n % c == 0 for n in sizes):
            return c
    raise ValueError(f"no tile in {candidates} divides {sizes}")


def _phys_block(b, j, nb, nx):
    return jnp.where(j == 0, nb * nx + b, b * nx + j - 1)


def _layer_norm(y, g, b):
    mu = jnp.mean(y, axis=-1, keepdims=True)
    yc = y - mu
    var = jnp.mean(yc * yc, axis=-1, keepdims=True)
    return yc * lax.rsqrt(var + LN_EPS) * g + b


def _silu(x):
    half = 0.5 * x
    return half + half * jnp.tanh(half)


def _softplus(x):
    return jnp.maximum(x, 0.0) + jnp.log1p(jnp.exp(-jnp.abs(x)))


def _cumsum_rows(x):
    n = x.shape[0]
    row = lax.broadcasted_iota(jnp.int32, x.shape, 0)
    d = 1
    while d < n:
        x = x + jnp.where(row >= d, pltpu.roll(x, d, axis=0), 0.0)
        d *= 2
    return x


def _ffn_ln_kernel(*refs, n_main_tiles):
    if n_main_tiles is None:
        x_ref, wg_ref, wu_ref, wd_ref, g_ref, b_ref, o_ref, xb_ref, acc_ref, y_ref = refs
        read_x = lambda: x_ref[...]
    else:
        x_ref, tail_ref, wg_ref, wu_ref, wd_ref, g_ref, b_ref, o_ref, xb_ref, acc_ref, y_ref = refs
        in_main = pl.program_id(0) < n_main_tiles
        read_x = lambda: jnp.where(in_main, x_ref[...], tail_ref[...])
    i, f = pl.program_id(0), pl.program_id(1)
    n_tiles, last = pl.num_programs(0) - 1, pl.num_programs(1) - 1

    def down_proj(xb):
        gate = jnp.dot(xb, wg_ref[...], preferred_element_type=F32)
        up = jnp.dot(xb, wu_ref[...], preferred_element_type=F32)
        act = (_silu(gate) * up).astype(BF16)
        return jnp.dot(act, wd_ref[...], preferred_element_type=F32)

    def normalize_previous(part):
        q = o_ref.shape[0] // FFN_LN_STEPS
        rows = pl.ds(pl.multiple_of(part * q, q), q)
        o_ref[rows, :] = _layer_norm(y_ref[rows, :], g_ref[...], b_ref[...])

    @pl.when((i == 0) & (f == 0))
    def _():
        y_ref[...] = jnp.zeros_like(y_ref)

    @pl.when((f == 0) & (i < n_tiles))
    def _():
        normalize_previous(0)
        xb = read_x().astype(BF16)
        xb_ref[...] = xb
        acc_ref[...] = down_proj(xb)

    @pl.when((f > 0) & (f < FFN_LN_STEPS) & (i < n_tiles))
    def _():
        normalize_previous(f)
        acc_ref[...] += down_proj(xb_ref[...])

    @pl.when((f < FFN_LN_STEPS) & (i == n_tiles))
    def _():
        normalize_previous(f)

    @pl.when((f >= FFN_LN_STEPS) & (f < last) & (i < n_tiles))
    def _():
        acc_ref[...] += down_proj(xb_ref[...])

    @pl.when((f == last) & (i < n_tiles))
    def _():
        y_ref[...] = ALPHA * read_x() + 0.5 * (acc_ref[...] + down_proj(xb_ref[...]))


def _ffn_ln(h, wg, wu, wd, g, b, *, tail=None, rows=None):
    d = h.shape[1]
    ff = wg.shape[1]
    t = h.shape[0] + (0 if tail is None else tail.shape[0])
    rows = t if rows is None else rows
    tm = _pick_tile((768,) + ROW_TILES, rows,
                    *((h.shape[0], tail.shape[0]) if tail is not None else ()))
    tf = _pick_tile((D_FF_TILE, 256, 128), ff)
    n_tiles, nf = rows // tm, ff // tf
    assert nf > FFN_LN_STEPS and tm % (8 * FFN_LN_STEPS) == 0
    tile = lambda i: jnp.minimum(i, n_tiles - 1)
    col = lambda i, f: jnp.where(i == n_tiles, nf - 1, f)
    x_specs = [pl.BlockSpec((tm, d), lambda i, f: (tile(i), 0))]
    operands = [h]
    n_main = None
    if tail is not None:
        n_main = h.shape[0] // tm
        x_specs = [pl.BlockSpec((tm, d), lambda i, f: (jnp.minimum(i, n_main - 1), 0)),
                   pl.BlockSpec((tm, d), lambda i, f: (jnp.maximum(tile(i) - n_main, 0), 0))]
        operands = [h, tail]
    return pl.pallas_call(
        functools.partial(_ffn_ln_kernel, n_main_tiles=n_main),
        grid=(n_tiles + 1, nf),
        in_specs=x_specs + [
            pl.BlockSpec((d, tf), lambda i, f: (0, col(i, f))),
            pl.BlockSpec((d, tf), lambda i, f: (0, col(i, f))),
            pl.BlockSpec((tf, d), lambda i, f: (col(i, f), 0)),
            pl.BlockSpec((1, d), lambda i, f: (0, 0)),
            pl.BlockSpec((1, d), lambda i, f: (0, 0)),
        ],
        out_specs=pl.BlockSpec((tm, d), lambda i, f: (jnp.maximum(i - 1, 0), 0)),
        out_shape=jax.ShapeDtypeStruct((rows, d), F32),
        scratch_shapes=[pltpu.VMEM((tm, d), BF16), pltpu.VMEM((tm, d), F32),
                        pltpu.VMEM((tm, d), F32)],
        compiler_params=_params("arbitrary", "arbitrary"),
        name="ffn_ln",
    )(*operands, wg, wu, wd, g, b)


LOG2_E = 1.4426950408889634
Q_SCALE = LOG2_E * ATTN_HEAD_DIM ** -0.5
AUG_Q = ATTN_HEAD_DIM
VT_ROWS = ATTN_HEAD_DIM + 16
ATTN_TQ = 2 * CHUNK
ATTN_QK_LEAD = ATTN_HEADS
NT_DIMS = (((1,), (1,)), ((), ()))


def _split3(c):
    hi = c.astype(BF16).astype(F32)
    mid = (c - hi).astype(BF16).astype(F32)
    lo = (c - hi - mid).astype(BF16).astype(F32)
    return hi, mid, lo


def _attn_operands(misc, q, k, v, bias, carry, n_pad):
    row = lax.broadcasted_iota(jnp.int32, (CHUNK, LANES), 0)
    lane = lax.broadcasted_iota(jnp.int32, (CHUNK, LANES), 1)
    is_pad = row < n_pad
    x = misc + bias
    log_f = jnp.minimum(x, 0.0) - jnp.log1p(jnp.exp(-jnp.abs(x)))
    c = _cumsum_rows(jnp.where(is_pad, 0.0, log_f)) + carry
    parts = _split3(c * LOG2_E)
    k_pad = jnp.where(lane == AUG_Q + 3, NEG_BIG, 0.0)
    q_aug, k_aug, v_t = [], [], []
    for pair in range(ATTN_HEADS // 2):
        q2 = q[:, pair * LANES:(pair + 1) * LANES]
        k2 = k[:, pair * LANES:(pair + 1) * LANES]
        v2t = v[:, pair * LANES:(pair + 1) * LANES].T
        for sub in range(2):
            h = 2 * pair + sub
            qh = q2 if sub == 0 else pltpu.roll(q2, ATTN_HEAD_DIM, axis=1)
            kh = k2 if sub == 0 else pltpu.roll(k2, ATTN_HEAD_DIM, axis=1)
            q_extra = jnp.where((lane >= AUG_Q + 3) & (lane < AUG_Q + 6), 1.0, 0.0)
            k_extra = jnp.where((lane >= AUG_Q) & (lane < AUG_Q + 3), 1.0, 0.0)
            for n, part in enumerate(parts):
                col = jnp.broadcast_to(part[:, h:h + 1], (CHUNK, LANES))
                q_extra = jnp.where(lane == AUG_Q + n, col, q_extra)
                k_extra = jnp.where(lane == AUG_Q + 3 + n, -col, k_extra)
            q_aug.append(jnp.where(lane < AUG_Q, qh, q_extra).astype(BF16))
            k_aug.append(jnp.where(is_pad, k_pad, jnp.where(lane < AUG_Q, kh, k_extra)).astype(BF16))
            v_t.append(v2t[sub * ATTN_HEAD_DIM:(sub + 1) * ATTN_HEAD_DIM, :].astype(BF16))
    return q_aug, k_aug, v_t, c[CHUNK - 1:CHUNK, :]


def _proj_kernel(x_ref, wq_ref, wkv_ref, wxbc_ref, wpool_ref, wz_ref, wmisc_ref, bf_ref,
                 rest_ref, qa_ref, ka_ref, vt_ref, qah_ref, kah_ref, vth_ref,
                 carry_ref, head_carry_ref, *, tiles_per_batch):
    g = pl.program_id(0)
    is_head = g == 0
    blocks = x_ref.shape[0] // CHUNK

    @pl.when(is_head)
    def _():
        carry_ref[...] = jnp.zeros_like(carry_ref)
        head_carry_ref[...] = jnp.zeros_like(head_carry_ref)

    xb = x_ref[...].astype(BF16)
    proj = lambda w: lax.dot_general(xb, w, NT_DIMS, preferred_element_type=F32)
    q = proj(wq_ref[...])
    k = proj(wkv_ref[0:D_ATTN, :])
    v = proj(wkv_ref[D_ATTN:2 * D_ATTN, :])
    misc = proj(wmisc_ref[...])
    rest_ref[:, REST_MISC:REST_MISC + LANES] = misc

    tile = jnp.maximum(g - 1, 0)
    first_of_row = tile % tiles_per_batch == 0
    carry = jnp.where(first_of_row, head_carry_ref[pl.ds(tile // tiles_per_batch, 1), :],
                      carry_ref[...])
    n_pad = jnp.where(is_head, PAD, 0)
    ones_row = jnp.where(lax.broadcasted_iota(jnp.int32, (VT_ROWS - ATTN_HEAD_DIM, CHUNK), 0) == 0,
                         1.0, 0.0).astype(BF16)
    block_sums = []
    for r in range(blocks):
        rows = slice(r * CHUNK, (r + 1) * CHUNK)
        half = slice((r % 2) * CHUNK, (r % 2 + 1) * CHUNK)
        q_aug, k_aug, v_t, carry = _attn_operands(
            misc[rows], q[rows], k[rows], v[rows], bf_ref[...],
            jnp.where(is_head, 0.0, carry), n_pad)
        block_sums.append(carry)
        for h in range(ATTN_HEADS):
            qa_ref[0, h, rows, :] = q_aug[h]
            ka_ref[0, h, rows, :] = k_aug[h]
            vt_ref[0, r // 2, h, 0:ATTN_HEAD_DIM, half] = v_t[h]
            vt_ref[0, r // 2, h, ATTN_HEAD_DIM:VT_ROWS, half] = ones_row
    carry_ref[...] = carry

    for w_ref, col0 in ((wxbc_ref, REST_XBC), (wpool_ref, REST_POOL), (wz_ref, REST_Z)):
        for c0 in range(0, w_ref.shape[0], PROJ_TILE):
            rest_ref[:, col0 + c0:col0 + c0 + PROJ_TILE] = proj(w_ref[c0:c0 + PROJ_TILE, :])

    @pl.when(is_head)
    def _():
        for r in range(blocks):
            rows = slice(r * CHUNK, (r + 1) * CHUNK)
            half = slice((r % 2) * CHUNK, (r % 2 + 1) * CHUNK)
            head_carry_ref[r:r + 1, :] = block_sums[r]
            for h in range(ATTN_HEADS):
                qah_ref[r, h] = qa_ref[0, h, rows, :]
                kah_ref[r, h] = ka_ref[0, h, rows, :]
                vth_ref[r, h] = vt_ref[0, r // 2, h, :, half]


def _proj(h, ws, bias, layer, nb, nx):
    t, d = h.shape
    seq = nx * CHUNK
    tm = nb * CHUNK
    assert seq % tm == 0 and tm % ATTN_TQ == 0 and t == nb * seq + tm
    tpb = seq // tm
    n_main = nb * tpb
    x_tile = lambda g: jnp.maximum(g - 1, 0)
    resident = lambda w: pl.BlockSpec((None, w.shape[1], d), lambda g: (layer, 0, 0),
                                      pipeline_mode=pl.Buffered(1))
    whole = lambda shape: pl.BlockSpec(shape, lambda g: (0,) * len(shape))
    rows_of = lambda g: jnp.where(g == 0, n_main, g - 1)
    head_qk = (nb, ATTN_HEADS, CHUNK, LANES)
    head_vt = (nb, ATTN_HEADS, VT_ROWS, CHUNK)
    return pl.pallas_call(
        functools.partial(_proj_kernel, tiles_per_batch=tpb),
        grid=(n_main + 1,),
        in_specs=[pl.BlockSpec((tm, d), lambda g: (rows_of(g), 0))] + [resident(w) for w in ws]
        + [pl.BlockSpec((1, LANES), lambda g: (0, 0))],
        out_specs=[
            pl.BlockSpec((tm, D_REST), lambda g: (rows_of(g), 0)),
            pl.BlockSpec((1, ATTN_HEADS, tm, LANES),
                         lambda g: (x_tile(g) // tpb, 0, x_tile(g) % tpb, 0)),
            pl.BlockSpec((1, ATTN_HEADS, tm, LANES),
                         lambda g: (x_tile(g) // tpb, 0, x_tile(g) % tpb, 0)),
            pl.BlockSpec((1, tm // ATTN_TQ, ATTN_HEADS, VT_ROWS, ATTN_TQ),
                         lambda g: (x_tile(g) // tpb, x_tile(g) % tpb, 0, 0, 0)),
            whole(head_qk), whole(head_qk), whole(head_vt),
        ],
        out_shape=[
            jax.ShapeDtypeStruct((t, D_REST), F32),
            jax.ShapeDtypeStruct((nb, ATTN_HEADS, seq, LANES), BF16),
            jax.ShapeDtypeStruct((nb, ATTN_HEADS, seq, LANES), BF16),
            jax.ShapeDtypeStruct((nb, seq // ATTN_TQ, ATTN_HEADS, VT_ROWS, ATTN_TQ), BF16),
            jax.ShapeDtypeStruct(head_qk, BF16),
            jax.ShapeDtypeStruct(head_qk, BF16),
            jax.ShapeDtypeStruct(head_vt, BF16),
        ],
        scratch_shapes=[pltpu.VMEM((1, LANES), F32), pltpu.VMEM((nb, LANES), F32)],
        compiler_params=_params("arbitrary"),
        name="in_proj",
    )(h, *ws, bias)


def _softmax_step(st, m_old, mask):
    if mask is not None:
        st = jnp.where(mask, st, NEG_BIG)
    m_new = jnp.maximum(m_old, jnp.max(st, axis=0, keepdims=True))
    return m_new, jnp.exp2(m_old - m_new), jnp.exp2(st - m_new).astype(BF16)


def _write_heads(o_ref, accs):
    for pair in range(ATTN_HEADS // 2):
        halves = []
        for a in accs[2 * pair:2 * pair + 2]:
            halves.append(a[0:ATTN_HEAD_DIM, :] * (1.0 / a[ATTN_HEAD_DIM:ATTN_HEAD_DIM + 1, :]))
        o_ref[:, pair * LANES:(pair + 1) * LANES] = jnp.concatenate(halves, axis=0).T.astype(BF16)


def _attn_kernel(qa_ref, ka_ref, vt_ref, kah_ref, vth_ref, o_ref, acc_ref, st_ref, m_ref):
    i = pl.program_id(1)
    acc_ref[...] = jnp.zeros_like(acc_ref)
    m_ref[...] = jnp.full(m_ref.shape, NEG_BIG, F32)

    def step(segments, mask_from=None):
        total = sum(n for _, n, _ in segments)
        mask = None
        if mask_from is not None:
            key = lax.broadcasted_iota(jnp.int32, (total, ATTN_TQ), 0)
            qry = lax.broadcasted_iota(jnp.int32, (total, ATTN_TQ), 1)
            mask = key - mask_from <= qry

        def qk(h):
            r = 0
            for load_keys, n, _ in segments:
                st_ref[h, r:r + n, :] = lax.dot_general(load_keys(h), qa_ref[0, h], NT_DIMS,
                                                        preferred_element_type=F32)
                r += n

        def softmax_pv(h):
            m_new, rescale, p = _softmax_step(st_ref[h, 0:total, :], m_ref[h], mask)
            m_ref[h] = m_new
            acc = rescale * acc_ref[h]
            r = 0
            for _, _, vt_chunks in segments:
                for vt in vt_chunks(h):
                    n = vt.shape[1]
                    acc += jnp.dot(vt, p[r:r + n, :], preferred_element_type=F32)
                    r += n
            acc_ref[h] = acc

        for h in range(ATTN_HEADS + ATTN_QK_LEAD):
            if h < ATTN_HEADS:
                qk(h)
            if h >= ATTN_QK_LEAD:
                softmax_pv(h - ATTN_QK_LEAD)

    def x_keys(pair, npairs):
        start = pl.multiple_of(pair * ATTN_TQ, ATTN_TQ)
        return (lambda h: ka_ref[0, h, pl.ds(start, npairs * ATTN_TQ), :], npairs * ATTN_TQ,
                lambda h: tuple(vt_ref[0, pair + k, h] for k in range(npairs)))

    head_keys = (lambda h: kah_ref[0, h], CHUNK, lambda h: (vth_ref[0, h],))

    def quad_step(s, _):
        step([x_keys(2 * s, 2)])
        return 0

    lax.fori_loop(0, i // 2, quad_step, 0)

    @pl.when(i % 2 == 0)
    def _():
        step([head_keys, x_keys(i, 1)], mask_from=CHUNK)

    @pl.when(i % 2 == 1)
    def _():
        step([head_keys, x_keys(i - 1, 2)], mask_from=CHUNK + ATTN_TQ)

    _write_heads(o_ref.at[0], [acc_ref[h] for h in range(ATTN_HEADS)])


def _attention(qa, ka, vt, ka_h, vt_h):
    nb, _, seq, _ = qa.shape
    nq = seq // ATTN_TQ
    per_row = lambda shape: pl.BlockSpec((1,) + shape, lambda b, i: (b,) + (0,) * len(shape))
    return pl.pallas_call(
        _attn_kernel,
        grid=(nb, nq),
        in_specs=[
            pl.BlockSpec((1, ATTN_HEADS, ATTN_TQ, LANES), lambda b, i: (b, 0, i, 0)),
            per_row((ATTN_HEADS, seq, LANES)),
            per_row((nq, ATTN_HEADS, VT_ROWS, ATTN_TQ)),
            per_row((ATTN_HEADS, CHUNK, LANES)),
            per_row((ATTN_HEADS, VT_ROWS, CHUNK)),
        ],
        out_specs=pl.BlockSpec((1, ATTN_TQ, D_ATTN), lambda b, i: (b, i, 0)),
        out_shape=jax.ShapeDtypeStruct((nb, seq, D_ATTN), BF16),
        scratch_shapes=[pltpu.VMEM((ATTN_HEADS, VT_ROWS, ATTN_TQ), F32),
                        pltpu.VMEM((ATTN_HEADS, CHUNK + 2 * ATTN_TQ, ATTN_TQ), F32),
                        pltpu.VMEM((ATTN_HEADS, 1, ATTN_TQ), F32)],
        compiler_params=_params("parallel", "arbitrary"),
        name="fox_attention",
    )(qa, ka, vt, ka_h, vt_h)


def _head_attn_kernel(qah_ref, kah_ref, vth_ref, o_ref):
    key = lax.broadcasted_iota(jnp.int32, (CHUNK, CHUNK), 0)
    qry = lax.broadcasted_iota(jnp.int32, (CHUNK, CHUNK), 1)
    accs = []
    for h in range(ATTN_HEADS):
        st = lax.dot_general(kah_ref[0, h], qah_ref[0, h], NT_DIMS, preferred_element_type=F32)
        _, _, p = _softmax_step(st, jnp.full((1, CHUNK), NEG_BIG, F32), key <= qry)
        accs.append(jnp.dot(vth_ref[0, h], p, preferred_element_type=F32))
    _write_heads(o_ref, accs)


def _head_attention(qa_h, ka_h, vt_h):
    nb = qa_h.shape[0]
    per_row = lambda shape: pl.BlockSpec((1,) + shape, lambda b: (b,) + (0,) * len(shape))
    return pl.pallas_call(
        _head_attn_kernel,
        grid=(nb,),
        in_specs=[per_row((ATTN_HEADS, CHUNK, LANES)), per_row((ATTN_HEADS, CHUNK, LANES)),
                  per_row((ATTN_HEADS, VT_ROWS, CHUNK))],
        out_specs=pl.BlockSpec((CHUNK, D_ATTN), lambda b: (b, 0)),
        out_shape=jax.ShapeDtypeStruct((nb * CHUNK, D_ATTN), BF16),
        compiler_params=_params("parallel"),
        name="head_attention",
    )(qa_h, ka_h, vt_h)


def _expand_heads(v, expand_ref):
    parts = jnp.concatenate(_split3(v), axis=1).astype(BF16)
    return jnp.dot(parts, expand_ref[...], preferred_element_type=F32)


def _pool_chunk(c, u_ref, pw_ref, ps_ref, yb_ref, buf_ref):
    @pl.when(c == 0)
    def _():
        buf_ref[0:POOL_HALO, :] = jnp.zeros((POOL_HALO, D_POOL), F32)

    @pl.when(c > 0)
    def _():
        buf_ref[0:POOL_HALO, :] = buf_ref[CHUNK:CHUNK + POOL_HALO, :]

    buf_ref[POOL_HALO:POOL_HALO + CHUNK, :] = u_ref[...]

    @pl.when(c == 0)
    def _():
        buf_ref[POOL_HALO:POOL_HALO + PAD, :] = jnp.zeros((PAD, D_POOL), F32)

    seen = c * CHUNK - PAD + 1 + lax.broadcasted_iota(jnp.int32, (CHUNK, POOL_GROUP_DIM), 0)
    for g, w in enumerate(POOL_WINDOWS):
        lo, hi = g * POOL_GROUP_DIM, (g + 1) * POOL_GROUP_DIM
        assert w & (w - 1) == 0 and w <= POOL_HALO
        win = buf_ref[:, lo:hi]
        s = 1
        while s < w:
            win = win + pltpu.roll(win, s, axis=0)
            s *= 2
        win = win[POOL_HALO:, :]
        u = buf_ref[POOL_HALO:POOL_HALO + CHUNK, lo:hi]
        cnt = jnp.clip(seen, 1, w).astype(F32)
        diff = (win / cnt - u).astype(BF16)
        mixed = jnp.dot(diff, pw_ref[g], preferred_element_type=F32)
        yb_ref[:, lo:hi] = (mixed * ps_ref[:, lo:hi]).astype(BF16)


def _seq_mix_kernel(*refs, n_cast):
    (xbc_ref, z_ref, misc_ref, u_ref, cw_ref, cb_ref, dtb_ref, a_ref, dsk_ref, nw_ref, pw_ref,
     ps_ref, expand_ref) = refs[:13]
    cast_src = refs[13:13 + n_cast]
    yc_ref, yb_ref = refs[13 + n_cast:15 + n_cast]
    cast_dst = refs[15 + n_cast:15 + 2 * n_cast]
    state_ref, ext_ref, buf_ref = refs[15 + 2 * n_cast:]
    c = pl.program_id(1)
    for src_ref, dst_ref in zip(cast_src, cast_dst):
        dst_ref[...] = src_ref[...].astype(BF16)
    _pool_chunk(c, u_ref, pw_ref, ps_ref, yb_ref, buf_ref)

    @pl.when(c == 0)
    def _():
        state_ref[...] = jnp.zeros_like(state_ref)
        ext_ref[0:CONV_HALO, :] = jnp.zeros((CONV_HALO, D_CONV), F32)

    @pl.when(c > 0)
    def _():
        ext_ref[0:CONV_HALO, :] = ext_ref[CHUNK:CHUNK + CONV_HALO, :]

    ext_ref[CONV_HALO:CONV_HALO + CHUNK, :] = xbc_ref[...]

    @pl.when(c == 0)
    def _():
        ext_ref[CONV_HALO:CONV_HALO + PAD, :] = jnp.zeros((PAD, D_CONV), F32)

    conv = cb_ref[...]
    for k in range(CONV_K):
        off = CONV_HALO - (CONV_K - 1) + k
        conv = conv + cw_ref[k:k + 1, :] * ext_ref[off:off + CHUNK, :]
    xc = _silu(conv)
    xs = xc[:, :D_SSD]
    gn = SSD_GROUPS * SSD_STATE
    bm = xc[:, D_SSD:D_SSD + gn]
    cm = xc[:, D_SSD + gn:D_SSD + 2 * gn]

    row = lax.broadcasted_iota(jnp.int32, (CHUNK, CHUNK), 0)
    col = lax.broadcasted_iota(jnp.int32, (CHUNK, CHUNK), 1)
    dt = _softplus(misc_ref[...] + dtb_ref[...])
    dt = jnp.where(row >= jnp.where(c == 0, PAD, 0), dt, 0.0)
    a_cs = _cumsum_rows(dt * a_ref[...])
    a_cs_t = a_cs.T
    a_last = a_cs[CHUNK - 1:CHUNK, :]
    x_dt = xs * _expand_heads(dt, expand_ref)
    x_dt_b = x_dt.astype(BF16)
    decay_out = _expand_heads(jnp.exp2(a_cs), expand_ref)
    x_state = (x_dt * _expand_heads(jnp.exp2(a_last - a_cs), expand_ref)).astype(BF16)
    chunk_decay = decay_out[CHUNK - 1:CHUNK, :]

    causal = col <= row
    hpg = SSD_HEADS // SSD_GROUPS
    gw = hpg * SSD_HEAD_DIM
    for g in range(SSD_GROUPS):
        n0, n1 = g * SSD_STATE, (g + 1) * SSD_STATE
        cm_g = cm[:, n0:n1].astype(BF16)
        bm_g = bm[:, n0:n1]
        cb = lax.dot_general(cm_g, bm_g.astype(BF16), NT_DIMS, preferred_element_type=F32)
        st = state_ref[:, g * gw:(g + 1) * gw]
        y_off = jnp.dot(cm_g, st.astype(BF16), preferred_element_type=F32)
        y_g = y_off * decay_out[:, g * gw:(g + 1) * gw]
        diag = []
        for r in range(hpg):
            h = g * hpg + r
            lane = MISC_DT0 + h
            seg = a_cs[:, lane:lane + 1] - a_cs_t[lane:lane + 1, :]
            m = (cb * jnp.exp2(jnp.where(causal, seg, NEG_BIG))).astype(BF16)
            diag.append(jnp.dot(m, x_dt_b[:, h * SSD_HEAD_DIM:(h + 1) * SSD_HEAD_DIM],
                                preferred_element_type=F32))
        y_g = y_g + jnp.concatenate(diag, axis=1)
        new = jnp.dot(bm_g.T.astype(BF16), x_state[:, g * gw:(g + 1) * gw],
                      preferred_element_type=F32)
        state_ref[:, g * gw:(g + 1) * gw] = chunk_decay[:, g * gw:(g + 1) * gw] * st + new

        sl = slice(g * gw, (g + 1) * gw)
        y_g = y_g + xs[:, sl] * dsk_ref[:, sl]
        gy = y_g * _silu(z_ref[:, sl])
        ms = jnp.mean(gy * gy, axis=-1, keepdims=True)
        yc_ref[:, sl] = (gy * lax.rsqrt(ms + RMS_EPS) * nw_ref[:, sl]).astype(BF16)


def _seq_mix(rest, cw, cb, dtb, a_neg, dsk, nw, pw, ps, nb, nx, casts=()):
    t = rest.shape[0]
    steps = nb * (nx + 1)
    src = lax.broadcasted_iota(jnp.int32, (3 * LANES, D_SSD), 0) % LANES
    dst = lax.broadcasted_iota(jnp.int32, (3 * LANES, D_SSD), 1) // SSD_HEAD_DIM
    expand = (src == dst + MISC_DT0).astype(BF16)
    cast_in, cast_out, cast_shapes = [], [], []
    for arr, layer, depth in casts:
        r, cols = arr.shape[0] // depth, arr.shape[1]
        rb = next(k for k in range(16, r + 1, 16) if r % k == 0 and r // k <= steps)
        nblk = r // rb
        blk_of = lambda b, c, nblk=nblk: jnp.minimum(b * (nx + 1) + c, nblk - 1)
        cast_in.append(pl.BlockSpec((rb, cols),
                                    lambda b, c, o=layer * nblk, f=blk_of: (o + f(b, c), 0)))
        cast_out.append(pl.BlockSpec((rb, cols), lambda b, c, f=blk_of: (f(b, c), 0)))
        cast_shapes.append(jax.ShapeDtypeStruct((r, cols), BF16))
    vec = lambda n: pl.BlockSpec((1, n), lambda b, c: (0, 0))
    blk = lambda b, c: _phys_block(b, c, nb, nx)
    rows = lambda n, col0: pl.BlockSpec((CHUNK, n), lambda b, c: (blk(b, c), col0 // n))
    return pl.pallas_call(
        functools.partial(_seq_mix_kernel, n_cast=len(casts)),
        grid=(nb, nx + 1),
        in_specs=[
            rows(D_CONV, REST_XBC), rows(D_SSD, REST_Z), rows(LANES, REST_MISC),
            rows(D_POOL, REST_POOL),
            pl.BlockSpec((CONV_K, D_CONV), lambda b, c: (0, 0)),
            vec(D_CONV), vec(LANES), vec(LANES), vec(D_SSD), vec(D_SSD),
            pl.BlockSpec((POOL_GROUPS, POOL_GROUP_DIM, POOL_GROUP_DIM), lambda b, c: (0, 0, 0)),
            vec(D_POOL),
            pl.BlockSpec(expand.shape, lambda b, c: (0, 0)),
        ] + cast_in,
        out_specs=[rows(D_SSD, 0), rows(D_POOL, 0)] + cast_out,
        out_shape=[jax.ShapeDtypeStruct((t, D_SSD), BF16),
                   jax.ShapeDtypeStruct((t, D_POOL), BF16)] + cast_shapes,
        scratch_shapes=[pltpu.VMEM((SSD_STATE, D_SSD), F32),
                        pltpu.VMEM((CONV_HALO + CHUNK, D_CONV), F32),
                        pltpu.VMEM((POOL_HALO + CHUNK, D_POOL), F32)],
        compiler_params=_params("parallel", "arbitrary"),
        name="seq_mix",
    )(rest, rest, rest, rest, cw, cb, dtb, a_neg, dsk, nw, pw, ps, expand,
      *(a for a, _, _ in casts))


def _out_ln_kernel(ya_ref, yah_ref, yb_ref, yc_ref, h_ref, w_ref, g_ref, b_ref, o_ref, *,
                   n_main_tiles):
    tm = o_ref.shape[0]
    in_main = pl.program_id(0) < n_main_tiles
    for r in range(0, tm, tm // 2):
        rows = slice(r, r + tm // 2)
        ya = jnp.where(in_main, ya_ref[rows, :], yah_ref[rows, :])
        acc = jnp.dot(ya, w_ref[0:D_ATTN, :], preferred_element_type=F32)
        acc += jnp.dot(yb_ref[rows, :], w_ref[D_ATTN:D_ATTN + D_POOL, :],
                       preferred_element_type=F32)
        acc += jnp.dot(yc_ref[rows, :], w_ref[D_ATTN + D_POOL:, :], preferred_element_type=F32)
        o_ref[rows, :] = _layer_norm(ALPHA * h_ref[rows, :] + acc, g_ref[...], b_ref[...])


def _out_ln(ya, ya_h, yb, yc, h, w, g, b):
    t, d = h.shape
    tm = _pick_tile(ROW_TILES, ya.shape[0], ya_h.shape[0])
    n_main = ya.shape[0] // tm
    rows = lambda n: pl.BlockSpec((tm, n), lambda i: (i, 0))
    return pl.pallas_call(
        functools.partial(_out_ln_kernel, n_main_tiles=n_main),
        grid=(t // tm,),
        in_specs=[
            pl.BlockSpec((tm, D_ATTN), lambda i: (jnp.minimum(i, n_main - 1), 0)),
            pl.BlockSpec((tm, D_ATTN), lambda i: (jnp.maximum(i - n_main, 0), 0)),
            rows(D_POOL), rows(D_SSD), rows(d),
            pl.BlockSpec((d, d), lambda i: (0, 0)),
            pl.BlockSpec((1, d), lambda i: (0, 0)),
            pl.BlockSpec((1, d), lambda i: (0, 0)),
        ],
        out_specs=rows(d),
        out_shape=jax.ShapeDtypeStruct((t, d), F32),
        compiler_params=_params("parallel"),
        name="out_proj_ln",
    )(ya, ya_h, yb, yc, h, w, g, b)


def _lane_row(vals, offset):
    return jnp.zeros((1, LANES), F32).at[0, offset:offset + vals.shape[0]].set(vals.astype(F32))


def _proj_weight(w_in):
    wt = jnp.swapaxes(w_in, 1, 2)
    c = D_ATTN
    o = 3 * c + ATTN_HEADS
    f = wt[:, 3 * c:o]
    dt = wt[:, o + D_POOL + D_SSD + D_CONV:]
    pad = jnp.zeros((wt.shape[0], LANES - ATTN_HEADS - SSD_HEADS, wt.shape[2]), wt.dtype)
    segments = (wt[:, :c] * Q_SCALE,
                wt[:, c:3 * c],
                wt[:, o + D_POOL + D_SSD:o + D_POOL + D_SSD + D_CONV],
                wt[:, o:o + D_POOL],
                wt[:, o + D_POOL:o + D_POOL + D_SSD],
                jnp.concatenate([f, dt, pad], axis=1))
    return tuple(w.astype(BF16) for w in segments)


def _mixer(h, nb, nx, layer, w_proj, b_fgate, pool_w, pool_scale, conv_w, conv_b, dt_bias,
           a_log, d_skip, ssd_norm_w, w_out, ln_g, ln_b, casts):
    rest, qa, ka, vt, qa_h, ka_h, vt_h = _proj(h, w_proj, _lane_row(b_fgate, MISC_F0), layer,
                                               nb, nx)
    ya = _attention(qa, ka, vt, ka_h, vt_h).reshape(nb * nx * CHUNK, D_ATTN)
    ya_h = _head_attention(qa_h, ka_h, vt_h)
    yc, yb, *cast = _seq_mix(rest, conv_w, conv_b.reshape(1, D_CONV),
                             _lane_row(dt_bias, MISC_DT0),
                             _lane_row(-LOG2_E * jnp.exp(a_log.astype(F32)), MISC_DT0),
                             jnp.repeat(d_skip, SSD_HEAD_DIM).reshape(1, D_SSD),
                             ssd_norm_w.reshape(1, D_SSD), pool_w.astype(BF16),
                             pool_scale.reshape(1, D_POOL), nb, nx, casts + [w_out])
    return _out_ln(ya, ya_h, yb, yc, h, cast[-1], ln_g, ln_b), cast[:-1]


def kernel(x, meta, f1_gate, f1_up, f1_down, ln1_g, ln1_b, w_in, b_fgate, pool_w, pool_scale,
           conv_w, conv_b, dt_bias, a_log, d_skip, ssd_norm_w, w_out, ln2_g, ln2_b, f2_gate,
           f2_up, f2_down, ln3_g, ln3_b):
    nb, seq, d = x.shape
    assert d == D_MODEL and meta.shape == (N_META, D_MODEL) and seq % ATTN_TQ == 0
    nx = seq // CHUNK
    depth = f1_gate.shape[0]
    assert depth == DEPTH
    head = jnp.concatenate([jnp.zeros((PAD, d), x.dtype), meta.astype(x.dtype)], axis=0)
    heads = jnp.broadcast_to(head[None], (nb, CHUNK, d)).reshape(nb * CHUNK, d)
    row = lambda v: v.reshape(1, d)
    bf = lambda w: w.astype(BF16)
    w_proj = _proj_weight(w_in)
    flat = lambda w: w.reshape(depth * w.shape[1], w.shape[2])
    ffn1 = (bf(f1_gate[0]), bf(f1_up[0]), bf(f1_down[0]))
    h = x.reshape(nb * seq, d)
    for i in range(depth):
        h = _ffn_ln(h, *ffn1, row(ln1_g[i]), row(ln1_b[i]), tail=heads if i == 0 else None)
        pending = [f2_gate, f2_up, f2_down] + ([f1_gate, f1_up, f1_down] if i + 1 < depth else [])
        layers = [i] * 3 + [i + 1] * 3
        h, cast = _mixer(h, nb, nx, i, w_proj, b_fgate[i], pool_w[i], pool_scale[i], conv_w[i],
                         conv_b[i], dt_bias[i], a_log[i], d_skip[i], ssd_norm_w[i],
                         (flat(w_out), i, depth),
                         row(ln2_g[i]), row(ln2_b[i]),
                         [(flat(w), l, depth) for w, l in zip(pending, layers)])
        ffn2, ffn1 = cast[:3], (cast[3:] if i + 1 < depth else None)
        h = _ffn_ln(h, *ffn2, row(ln3_g[i]), row(ln3_b[i]),
                    rows=nb * seq if i == depth - 1 else None)
    return h.reshape(nb, seq, d)
```

```python
import functools

import jax
import jax.numpy as jnp
from jax import lax
from jax.experimental import pallas as pl
from jax.experimental.pallas import tpu as pltpu

F32 = jnp.float32
BF16 = jnp.bfloat16

D_MODEL = 2048
N_META = 16
CHUNK = 128
PAD = CHUNK - N_META
LANES = 128

ATTN_HEADS = 8
ATTN_HEAD_DIM = 64
D_ATTN = ATTN_HEADS * ATTN_HEAD_DIM
POOL_WINDOWS = (2, 4, 8, 16)
POOL_GROUPS = 4
D_POOL = 512
POOL_GROUP_DIM = D_POOL // POOL_GROUPS
POOL_HALO = 16
D_SSD = 1024
SSD_HEAD_DIM = 64
SSD_HEADS = D_SSD // SSD_HEAD_DIM
SSD_GROUPS = 2
SSD_STATE = 128
CONV_K = 4
CONV_HALO = 8
D_CONV = D_SSD + 2 * SSD_GROUPS * SSD_STATE
D_FF_TILE = 512
FFN_LN_STEPS = 8
ROW_TILES = (512, 384, 256, 128)
DEPTH = 2
ALPHA = (2 * DEPTH) ** 0.25
LN_EPS = 1e-5
RMS_EPS = 1e-5
NEG_BIG = -1e30

PROJ_TILE = 512
REST_XBC, REST_POOL, REST_Z, REST_MISC = 0, D_CONV, D_CONV + D_POOL, D_CONV + D_POOL + D_SSD
D_REST = REST_MISC + LANES
MISC_F0 = 0
MISC_DT0 = ATTN_HEADS

VMEM_LIMIT = 56 * 1024 * 1024


def _params(*sem):
    return pltpu.CompilerParams(dimension_semantics=sem, vmem_limit_bytes=VMEM_LIMIT)


def _pick_tile(candidates, *sizes):
    for c in candidates:
        if all(n % c == 0 for n in sizes):
            return c
    raise ValueError(f"no tile in {candidates} divides {sizes}")


def _phys_block(b, j, nb, nx):
    return jnp.where(j == 0, nb * nx + b, b * nx + j - 1)


def _layer_norm(y, g, b):
    mu = jnp.mean(y, axis=-1, keepdims=True)
    yc = y - mu
    var = jnp.mean(yc * yc, axis=-1, keepdims=True)
    return yc * lax.rsqrt(var + LN_EPS) * g + b


def _silu(x):
    half = 0.5 * x
    return half + half * jnp.tanh(half)


def _softplus(x):
    return jnp.maximum(x, 0.0) + jnp.log1p(jnp.exp(-jnp.abs(x)))


def _cumsum_rows(x):
    n = x.shape[0]
    row = lax.broadcasted_iota(jnp.int32, x.shape, 0)
    d = 1
    while d < n:
        x = x + jnp.where(row >= d, pltpu.roll(x, d, axis=0), 0.0)
        d *= 2
    return x


def _ffn_ln_kernel(*refs, n_main_tiles):
    if n_main_tiles is None:
        x_ref, wg_ref, wu_ref, wd_ref, g_ref, b_ref, o_ref, xb_ref, acc_ref, y_ref = refs
        read_x = lambda: x_ref[...]
    else:
        x_ref, tail_ref, wg_ref, wu_ref, wd_ref, g_ref, b_ref, o_ref, xb_ref, acc_ref, y_ref = refs
        in_main = pl.program_id(0) < n_main_tiles
        read_x = lambda: jnp.where(in_main, x_ref[...], tail_ref[...])
    i, f = pl.program_id(0), pl.program_id(1)
    n_tiles, last = pl.num_programs(0) - 1, pl.num_programs(1) - 1

    def down_proj(xb):
        gate = jnp.dot(xb, wg_ref[...], preferred_element_type=F32)
        up = jnp.dot(xb, wu_ref[...], preferred_element_type=F32)
        act = (_silu(gate) * up).astype(BF16)
        return jnp.dot(act, wd_ref[...], preferred_element_type=F32)

    def normalize_previous(part):
        q = o_ref.shape[0] // FFN_LN_STEPS
        rows = pl.ds(pl.multiple_of(part * q, q), q)
        o_ref[rows, :] = _layer_norm(y_ref[rows, :], g_ref[...], b_ref[...])

    @pl.when((i == 0) & (f == 0))
    def _():
        y_ref[...] = jnp.zeros_like(y_ref)

    @pl.when((f == 0) & (i < n_tiles))
    def _():
        normalize_previous(0)
        xb = read_x().astype(BF16)
        xb_ref[...] = xb
        acc_ref[...] = down_proj(xb)

    @pl.when((f > 0) & (f < FFN_LN_STEPS) & (i < n_tiles))
    def _():
        normalize_previous(f)
        acc_ref[...] += down_proj(xb_ref[...])

    @pl.when((f < FFN_LN_STEPS) & (i == n_tiles))
    def _():
        normalize_previous(f)

    @pl.when((f >= FFN_LN_STEPS) & (f < last) & (i < n_tiles))
    def _():
        acc_ref[...] += down_proj(xb_ref[...])

    @pl.when((f == last) & (i < n_tiles))
    def _():
        y_ref[...] = ALPHA * read_x() + 0.5 * (acc_ref[...] + down_proj(xb_ref[...]))


def _ffn_ln(h, wg, wu, wd, g, b, *, tail=None, rows=None):
    d = h.shape[1]
    ff = wg.shape[1]
    t = h.shape[0] + (0 if tail is None else tail.shape[0])
    rows = t if rows is None else rows
    tm = _pick_tile((768,) + ROW_TILES, rows,
                    *((h.shape[0], tail.shape[0]) if tail is not None else ()))
    tf = _pick_tile((D_FF_TILE, 256, 128), ff)
    n_tiles, nf = rows // tm, ff // tf
    assert nf > FFN_LN_STEPS and tm % (8 * FFN_LN_STEPS) == 0
    tile = lambda i: jnp.minimum(i, n_tiles - 1)
    col = lambda i, f: jnp.where(i == n_tiles, nf - 1, f)
    x_specs = [pl.BlockSpec((tm, d), lambda i, f: (tile(i), 0))]
    operands = [h]
    n_main = None
    if tail is not None:
        n_main = h.shape[0] // tm
        x_specs = [pl.BlockSpec((tm, d), lambda i, f: (jnp.minimum(i, n_main - 1), 0)),
                   pl.BlockSpec((tm, d), lambda i, f: (jnp.maximum(tile(i) - n_main, 0), 0))]
        operands = [h, tail]
    return pl.pallas_call(
        functools.partial(_ffn_ln_kernel, n_main_tiles=n_main),
        grid=(n_tiles + 1, nf),
        in_specs=x_specs + [
            pl.BlockSpec((d, tf), lambda i, f: (0, col(i, f))),
            pl.BlockSpec((d, tf), lambda i, f: (0, col(i, f))),
            pl.BlockSpec((tf, d), lambda i, f: (col(i, f), 0)),
            pl.BlockSpec((1, d), lambda i, f: (0, 0)),
            pl.BlockSpec((1, d), lambda i, f: (0, 0)),
        ],
        out_specs=pl.BlockSpec((tm, d), lambda i, f: (jnp.maximum(i - 1, 0), 0)),
        out_shape=jax.ShapeDtypeStruct((rows, d), F32),
        scratch_shapes=[pltpu.VMEM((tm, d), BF16), pltpu.VMEM((tm, d), F32),
                        pltpu.VMEM((tm, d), F32)],
        compiler_params=_params("arbitrary", "arbitrary"),
        name="ffn_ln",
    )(*operands, wg, wu, wd, g, b)


LOG2_E = 1.4426950408889634
Q_SCALE = LOG2_E * ATTN_HEAD_DIM ** -0.5
AUG_Q = ATTN_HEAD_DIM
VT_ROWS = ATTN_HEAD_DIM + 16
ATTN_TQ = 2 * CHUNK
ATTN_QK_LEAD = ATTN_HEADS
NT_DIMS = (((1,), (1,)), ((), ()))


def _split3(c):
    hi = c.astype(BF16).astype(F32)
    mid = (c - hi).astype(BF16).astype(F32)
    lo = (c - hi - mid).astype(BF16).astype(F32)
    return hi, mid, lo


def _attn_operands(misc, q, k, v, bias, carry, n_pad):
    row = lax.broadcasted_iota(jnp.int32, (CHUNK, LANES), 0)
    lane = lax.broadcasted_iota(jnp.int32, (CHUNK, LANES), 1)
    is_pad = row < n_pad
    x = misc + bias
    log_f = jnp.minimum(x, 0.0) - jnp.log1p(jnp.exp(-jnp.abs(x)))
    c = _cumsum_rows(jnp.where(is_pad, 0.0, log_f)) + carry
    parts = _split3(c * LOG2_E)
    k_pad = jnp.where(lane == AUG_Q + 3, NEG_BIG, 0.0)
    q_aug, k_aug, v_t = [], [], []
    for pair in range(ATTN_HEADS // 2):
        q2 = q[:, pair * LANES:(pair + 1) * LANES]
        k2 = k[:, pair * LANES:(pair + 1) * LANES]
        v2t = v[:, pair * LANES:(pair + 1) * LANES].T
        for sub in range(2):
            h = 2 * pair + sub
            qh = q2 if sub == 0 else pltpu.roll(q2, ATTN_HEAD_DIM, axis=1)
            kh = k2 if sub == 0 else pltpu.roll(k2, ATTN_HEAD_DIM, axis=1)
            q_extra = jnp.where((lane >= AUG_Q + 3) & (lane < AUG_Q + 6), 1.0, 0.0)
            k_extra = jnp.where((lane >= AUG_Q) & (lane < AUG_Q + 3), 1.0, 0.0)
            for n, part in enumerate(parts):
                col = jnp.broadcast_to(part[:, h:h + 1], (CHUNK, LANES))
                q_extra = jnp.where(lane == AUG_Q + n, col, q_extra)
                k_extra = jnp.where(lane == AUG_Q + 3 + n, -col, k_extra)
            q_aug.append(jnp.where(lane < AUG_Q, qh, q_extra).astype(BF16))
            k_aug.append(jnp.where(is_pad, k_pad, jnp.where(lane < AUG_Q, kh, k_extra)).astype(BF16))
            v_t.append(v2t[sub * ATTN_HEAD_DIM:(sub + 1) * ATTN_HEAD_DIM, :].astype(BF16))
    return q_aug, k_aug, v_t, c[CHUNK - 1:CHUNK, :]


def _proj_kernel(x_ref, wq_ref, wkv_ref, wxbc_ref, wpool_ref, wz_ref, wmisc_ref, bf_ref,
                 rest_ref, qa_ref, ka_ref, vt_ref, qah_ref, kah_ref, vth_ref,
                 carry_ref, head_carry_ref, *, tiles_per_batch):
    g = pl.program_id(0)
    is_head = g == 0
    blocks = x_ref.shape[0] // CHUNK

    @pl.when(is_head)
    def _():
        carry_ref[...] = jnp.zeros_like(carry_ref)
        head_carry_ref[...] = jnp.zeros_like(head_carry_ref)

    xb = x_ref[...].astype(BF16)
    proj = lambda w: lax.dot_general(xb, w, NT_DIMS, preferred_element_type=F32)
    q = proj(wq_ref[...])
    k = proj(wkv_ref[0:D_ATTN, :])
    v = proj(wkv_ref[D_ATTN:2 * D_ATTN, :])
    misc = proj(wmisc_ref[...])
    rest_ref[:, REST_MISC:REST_MISC + LANES] = misc

    tile = jnp.maximum(g - 1, 0)
    first_of_row = tile % tiles_per_batch == 0
    carry = jnp.where(first_of_row, head_carry_ref[pl.ds(tile // tiles_per_batch, 1), :],
                      carry_ref[...])
    n_pad = jnp.where(is_head, PAD, 0)
    ones_row = jnp.where(lax.broadcasted_iota(jnp.int32, (VT_ROWS - ATTN_HEAD_DIM, CHUNK), 0) == 0,
                         1.0, 0.0).astype(BF16)
    block_sums = []
    for r in range(blocks):
        rows = slice(r * CHUNK, (r + 1) * CHUNK)
        half = slice((r % 2) * CHUNK, (r % 2 + 1) * CHUNK)
        q_aug, k_aug, v_t, carry = _attn_operands(
            misc[rows], q[rows], k[rows], v[rows], bf_ref[...],
            jnp.where(is_head, 0.0, carry), n_pad)
        block_sums.append(carry)
        for h in range(ATTN_HEADS):
            qa_ref[0, h, rows, :] = q_aug[h]
            ka_ref[0, h, rows, :] = k_aug[h]
            vt_ref[0, r // 2, h, 0:ATTN_HEAD_DIM, half] = v_t[h]
            vt_ref[0, r // 2, h, ATTN_HEAD_DIM:VT_ROWS, half] = ones_row
    carry_ref[...] = carry

    for w_ref, col0 in ((wxbc_ref, REST_XBC), (wpool_ref, REST_POOL), (wz_ref, REST_Z)):
        for c0 in range(0, w_ref.shape[0], PROJ_TILE):
            rest_ref[:, col0 + c0:col0 + c0 + PROJ_TILE] = proj(w_ref[c0:c0 + PROJ_TILE, :])

    @pl.when(is_head)
    def _():
        for r in range(blocks):
            rows = slice(r * CHUNK, (r + 1) * CHUNK)
            half = slice((r % 2) * CHUNK, (r % 2 + 1) * CHUNK)
            head_carry_ref[r:r + 1, :] = block_sums[r]
            for h in range(ATTN_HEADS):
                qah_ref[r, h] = qa_ref[0, h, rows, :]
                kah_ref[r, h] = ka_ref[0, h, rows, :]
                vth_ref[r, h] = vt_ref[0, r // 2, h, :, half]


def _proj(h, ws, bias, layer, nb, nx):
    t, d = h.shape
    seq = nx * CHUNK
    tm = nb * CHUNK
    assert seq % tm == 0 and tm % ATTN_TQ == 0 and t == nb * seq + tm
    tpb = seq // tm
    n_main = nb * tpb
    x_tile = lambda g: jnp.maximum(g - 1, 0)
    resident = lambda w: pl.BlockSpec((None, w.shape[1], d), lambda g: (layer, 0, 0),
                                      pipeline_mode=pl.Buffered(1))
    whole = lambda shape: pl.BlockSpec(shape, lambda g: (0,) * len(shape))
    rows_of = lambda g: jnp.where(g == 0, n_main, g - 1)
    head_qk = (nb, ATTN_HEADS, CHUNK, LANES)
    head_vt = (nb, ATTN_HEADS, VT_ROWS, CHUNK)
    return pl.pallas_call(
        functools.partial(_proj_kernel, tiles_per_batch=tpb),
        grid=(n_main + 1,),
        in_specs=[pl.BlockSpec((tm, d), lambda g: (rows_of(g), 0))] + [resident(w) for w in ws]
        + [pl.BlockSpec((1, LANES), lambda g: (0, 0))],
        out_specs=[
            pl.BlockSpec((tm, D_REST), lambda g: (rows_of(g), 0)),
            pl.BlockSpec((1, ATTN_HEADS, tm, LANES),
                         lambda g: (x_tile(g) // tpb, 0, x_tile(g) % tpb, 0)),
            pl.BlockSpec((1, ATTN_HEADS, tm, LANES),
                         lambda g: (x_tile(g) // tpb, 0, x_tile(g) % tpb, 0)),
            pl.BlockSpec((1, tm // ATTN_TQ, ATTN_HEADS, VT_ROWS, ATTN_TQ),
                         lambda g: (x_tile(g) // tpb, x_tile(g) % tpb, 0, 0, 0)),
            whole(head_qk), whole(head_qk), whole(head_vt),
        ],
        out_shape=[
            jax.ShapeDtypeStruct((t, D_REST), F32),
            jax.ShapeDtypeStruct((nb, ATTN_HEADS, seq, LANES), BF16),
            jax.ShapeDtypeStruct((nb, ATTN_HEADS, seq, LANES), BF16),
            jax.ShapeDtypeStruct((nb, seq // ATTN_TQ, ATTN_HEADS, VT_ROWS, ATTN_TQ), BF16),
            jax.ShapeDtypeStruct(head_qk, BF16),
            jax.ShapeDtypeStruct(head_qk, BF16),
            jax.ShapeDtypeStruct(head_vt, BF16),
        ],
        scratch_shapes=[pltpu.VMEM((1, LANES), F32), pltpu.VMEM((nb, LANES), F32)],
        compiler_params=_params("arbitrary"),
        name="in_proj",
    )(h, *ws, bias)


def _softmax_step(st, m_old, mask):
    if mask is not None:
        st = jnp.where(mask, st, NEG_BIG)
    m_new = jnp.maximum(m_old, jnp.max(st, axis=0, keepdims=True))
    return m_new, jnp.exp2(m_old - m_new), jnp.exp2(st - m_new).astype(BF16)


def _write_heads(o_ref, accs):
    for pair in range(ATTN_HEADS // 2):
        halves = []
        for a in accs[2 * pair:2 * pair + 2]:
            halves.append(a[0:ATTN_HEAD_DIM, :] * (1.0 / a[ATTN_HEAD_DIM:ATTN_HEAD_DIM + 1, :]))
        o_ref[:, pair * LANES:(pair + 1) * LANES] = jnp.concatenate(halves, axis=0).T.astype(BF16)


def _attn_kernel(qa_ref, ka_ref, vt_ref, kah_ref, vth_ref, o_ref, acc_ref, st_ref, m_ref):
    i = pl.program_id(1)
    acc_ref[...] = jnp.zeros_like(acc_ref)
    m_ref[...] = jnp.full(m_ref.shape, NEG_BIG, F32)

    def step(segments, mask_from=None):
        total = sum(n for _, n, _ in segments)
        mask = None
        if mask_from is not None:
            key = lax.broadcasted_iota(jnp.int32, (total, ATTN_TQ), 0)
            qry = lax.broadcasted_iota(jnp.int32, (total, ATTN_TQ), 1)
            mask = key - mask_from <= qry

        def qk(h):
            r = 0
            for load_keys, n, _ in segments:
                st_ref[h, r:r + n, :] = lax.dot_general(load_keys(h), qa_ref[0, h], NT_DIMS,
                                                        preferred_element_type=F32)
                r += n

        def softmax_pv(h):
            m_new, rescale, p = _softmax_step(st_ref[h, 0:total, :], m_ref[h], mask)
            m_ref[h] = m_new
            acc = rescale * acc_ref[h]
            r = 0
            for _, _, vt_chunks in segments:
                for vt in vt_chunks(h):
                    n = vt.shape[1]
                    acc += jnp.dot(vt, p[r:r + n, :], preferred_element_type=F32)
                    r += n
            acc_ref[h] = acc

        for h in range(ATTN_HEADS + ATTN_QK_LEAD):
            if h < ATTN_HEADS:
                qk(h)
            if h >= ATTN_QK_LEAD:
                softmax_pv(h - ATTN_QK_LEAD)

    def x_keys(pair, npairs):
        start = pl.multiple_of(pair * ATTN_TQ, ATTN_TQ)
        return (lambda h: ka_ref[0, h, pl.ds(start, npairs * ATTN_TQ), :], npairs * ATTN_TQ,
                lambda h: tuple(vt_ref[0, pair + k, h] for k in range(npairs)))

    head_keys = (lambda h: kah_ref[0, h], CHUNK, lambda h: (vth_ref[0, h],))

    def quad_step(s, _):
        step([x_keys(2 * s, 2)])
        return 0

    lax.fori_loop(0, i // 2, quad_step, 0)

    @pl.when(i % 2 == 0)
    def _():
        step([head_keys, x_keys(i, 1)], mask_from=CHUNK)

    @pl.when(i % 2 == 1)
    def _():
        step([head_keys, x_keys(i - 1, 2)], mask_from=CHUNK + ATTN_TQ)

    _write_heads(o_ref.at[0], [acc_ref[h] for h in range(ATTN_HEADS)])


def _attention(qa, ka, vt, ka_h, vt_h):
    nb, _, seq, _ = qa.shape
    nq = seq // ATTN_TQ
    per_row = lambda shape: pl.BlockSpec((1,) + shape, lambda b, i: (b,) + (0,) * len(shape))
    return pl.pallas_call(
        _attn_kernel,
        grid=(nb, nq),
        in_specs=[
            pl.BlockSpec((1, ATTN_HEADS, ATTN_TQ, LANES), lambda b, i: (b, 0, i, 0)),
            per_row((ATTN_HEADS, seq, LANES)),
            per_row((nq, ATTN_HEADS, VT_ROWS, ATTN_TQ)),
            per_row((ATTN_HEADS, CHUNK, LANES)),
            per_row((ATTN_HEADS, VT_ROWS, CHUNK)),
        ],
        out_specs=pl.BlockSpec((1, ATTN_TQ, D_ATTN), lambda b, i: (b, i, 0)),
        out_shape=jax.ShapeDtypeStruct((nb, seq, D_ATTN), BF16),
        scratch_shapes=[pltpu.VMEM((ATTN_HEADS, VT_ROWS, ATTN_TQ), F32),
                        pltpu.VMEM((ATTN_HEADS, CHUNK + 2 * ATTN_TQ, ATTN_TQ), F32),
                        pltpu.VMEM((ATTN_HEADS, 1, ATTN_TQ), F32)],
        compiler_params=_params("parallel", "arbitrary"),
        name="fox_attention",
    )(qa, ka, vt, ka_h, vt_h)


def _head_attn_kernel(qah_ref, kah_ref, vth_ref, o_ref):
    key = lax.broadcasted_iota(jnp.int32, (CHUNK, CHUNK), 0)
    qry = lax.broadcasted_iota(jnp.int32, (CHUNK, CHUNK), 1)
    accs = []
    for h in range(ATTN_HEADS):
        st = lax.dot_general(kah_ref[0, h], qah_ref[0, h], NT_DIMS, preferred_element_type=F32)
        _, _, p = _softmax_step(st, jnp.full((1, CHUNK), NEG_BIG, F32), key <= qry)
        accs.append(jnp.dot(vth_ref[0, h], p, preferred_element_type=F32))
    _write_heads(o_ref, accs)


def _head_attention(qa_h, ka_h, vt_h):
    nb = qa_h.shape[0]
    per_row = lambda shape: pl.BlockSpec((1,) + shape, lambda b: (b,) + (0,) * len(shape))
    return pl.pallas_call(
        _head_attn_kernel,
        grid=(nb,),
        in_specs=[per_row((ATTN_HEADS, CHUNK, LANES)), per_row((ATTN_HEADS, CHUNK, LANES)),
                  per_row((ATTN_HEADS, VT_ROWS, CHUNK))],
        out_specs=pl.BlockSpec((CHUNK, D_ATTN), lambda b: (b, 0)),
        out_shape=jax.ShapeDtypeStruct((nb * CHUNK, D_ATTN), BF16),
        compiler_params=_params("parallel"),
        name="head_attention",
    )(qa_h, ka_h, vt_h)


def _expand_heads(v, expand_ref):
    parts = jnp.concatenate(_split3(v), axis=1).astype(BF16)
    return jnp.dot(parts, expand_ref[...], preferred_element_type=F32)


def _pool_chunk(c, u_ref, pw_ref, ps_ref, yb_ref, buf_ref):
    @pl.when(c == 0)
    def _():
        buf_ref[0:POOL_HALO, :] = jnp.zeros((POOL_HALO, D_POOL), F32)

    @pl.when(c > 0)
    def _():
        buf_ref[0:POOL_HALO, :] = buf_ref[CHUNK:CHUNK + POOL_HALO, :]

    buf_ref[POOL_HALO:POOL_HALO + CHUNK, :] = u_ref[...]

    @pl.when(c == 0)
    def _():
        buf_ref[POOL_HALO:POOL_HALO + PAD, :] = jnp.zeros((PAD, D_POOL), F32)

    seen = c * CHUNK - PAD + 1 + lax.broadcasted_iota(jnp.int32, (CHUNK, POOL_GROUP_DIM), 0)
    for g, w in enumerate(POOL_WINDOWS):
        lo, hi = g * POOL_GROUP_DIM, (g + 1) * POOL_GROUP_DIM
        assert w & (w - 1) == 0 and w <= POOL_HALO
        win = buf_ref[:, lo:hi]
        s = 1
        while s < w:
            win = win + pltpu.roll(win, s, axis=0)
            s *= 2
        win = win[POOL_HALO:, :]
        u = buf_ref[POOL_HALO:POOL_HALO + CHUNK, lo:hi]
        cnt = jnp.clip(seen, 1, w).astype(F32)
        diff = (win / cnt - u).astype(BF16)
        mixed = jnp.dot(diff, pw_ref[g], preferred_element_type=F32)
        yb_ref[:, lo:hi] = (mixed * ps_ref[:, lo:hi]).astype(BF16)


def _seq_mix_kernel(*refs, n_cast):
    (xbc_ref, z_ref, misc_ref, u_ref, cw_ref, cb_ref, dtb_ref, a_ref, dsk_ref, nw_ref, pw_ref,
     ps_ref, expand_ref) = refs[:13]
    cast_src = refs[13:13 + n_cast]
    yc_ref, yb_ref = refs[13 + n_cast:15 + n_cast]
    cast_dst = refs[15 + n_cast:15 + 2 * n_cast]
    state_ref, ext_ref, buf_ref = refs[15 + 2 * n_cast:]
    c = pl.program_id(1)
    for src_ref, dst_ref in zip(cast_src, cast_dst):
        dst_ref[...] = src_ref[...].astype(BF16)
    _pool_chunk(c, u_ref, pw_ref, ps_ref, yb_ref, buf_ref)

    @pl.when(c == 0)
    def _():
        state_ref[...] = jnp.zeros_like(state_ref)
        ext_ref[0:CONV_HALO, :] = jnp.zeros((CONV_HALO, D_CONV), F32)

    @pl.when(c > 0)
    def _():
        ext_ref[0:CONV_HALO, :] = ext_ref[CHUNK:CHUNK + CONV_HALO, :]

    ext_ref[CONV_HALO:CONV_HALO + CHUNK, :] = xbc_ref[...]

    @pl.when(c == 0)
    def _():
        ext_ref[CONV_HALO:CONV_HALO + PAD, :] = jnp.zeros((PAD, D_CONV), F32)

    conv = cb_ref[...]
    for k in range(CONV_K):
        off = CONV_HALO - (CONV_K - 1) + k
        conv = conv + cw_ref[k:k + 1, :] * ext_ref[off:off + CHUNK, :]
    xc = _silu(conv)
    xs = xc[:, :D_SSD]
    gn = SSD_GROUPS * SSD_STATE
    bm = xc[:, D_SSD:D_SSD + gn]
    cm = xc[:, D_SSD + gn:D_SSD + 2 * gn]

    row = lax.broadcasted_iota(jnp.int32, (CHUNK, CHUNK), 0)
    col = lax.broadcasted_iota(jnp.int32, (CHUNK, CHUNK), 1)
    dt = _softplus(misc_ref[...] + dtb_ref[...])
    dt = jnp.where(row >= jnp.where(c == 0, PAD, 0), dt, 0.0)
    a_cs = _cumsum_rows(dt * a_ref[...])
    a_cs_t = a_cs.T
    a_last = a_cs[CHUNK - 1:CHUNK, :]
    x_dt = xs * _expand_heads(dt, expand_ref)
    x_dt_b = x_dt.astype(BF16)
    decay_out = _expand_heads(jnp.exp2(a_cs), expand_ref)
    x_state = (x_dt * _expand_heads(jnp.exp2(a_last - a_cs), expand_ref)).astype(BF16)
    chunk_decay = decay_out[CHUNK - 1:CHUNK, :]

    causal = col <= row
    hpg = SSD_HEADS // SSD_GROUPS
    gw = hpg * SSD_HEAD_DIM
    for g in range(SSD_GROUPS):
        n0, n1 = g * SSD_STATE, (g + 1) * SSD_STATE
        cm_g = cm[:, n0:n1].astype(BF16)
        bm_g = bm[:, n0:n1]
        cb = lax.dot_general(cm_g, bm_g.astype(BF16), NT_DIMS, preferred_element_type=F32)
        st = state_ref[:, g * gw:(g + 1) * gw]
        y_off = jnp.dot(cm_g, st.astype(BF16), preferred_element_type=F32)
        y_g = y_off * decay_out[:, g * gw:(g + 1) * gw]
        diag = []
        for r in range(hpg):
            h = g * hpg + r
            lane = MISC_DT0 + h
            seg = a_cs[:, lane:lane + 1] - a_cs_t[lane:lane + 1, :]
            m = (cb * jnp.exp2(jnp.where(causal, seg, NEG_BIG))).astype(BF16)
            diag.append(jnp.dot(m, x_dt_b[:, h * SSD_HEAD_DIM:(h + 1) * SSD_HEAD_DIM],
                                preferred_element_type=F32))
        y_g = y_g + jnp.concatenate(diag, axis=1)
        new = jnp.dot(bm_g.T.astype(BF16), x_state[:, g * gw:(g + 1) * gw],
                      preferred_element_type=F32)
        state_ref[:, g * gw:(g + 1) * gw] = chunk_decay[:, g * gw:(g + 1) * gw] * st + new

        sl = slice(g * gw, (g + 1) * gw)
        y_g = y_g + xs[:, sl] * dsk_ref[:, sl]
        gy = y_g * _silu(z_ref[:, sl])
        ms = jnp.mean(gy * gy, axis=-1, keepdims=True)
        yc_ref[:, sl] = (gy * lax.rsqrt(ms + RMS_EPS) * nw_ref[:, sl]).astype(BF16)


def _seq_mix(rest, cw, cb, dtb, a_neg, dsk, nw, pw, ps, nb, nx, casts=()):
    t = rest.shape[0]
    steps = nb * (nx + 1)
    src = lax.broadcasted_iota(jnp.int32, (3 * LANES, D_SSD), 0) % LANES
    dst = lax.broadcasted_iota(jnp.int32, (3 * LANES, D_SSD), 1) // SSD_HEAD_DIM
    expand = (src == dst + MISC_DT0).astype(BF16)
    cast_in, cast_out, cast_shapes = [], [], []
    for arr, layer, depth in casts:
        r, cols = arr.shape[0] // depth, arr.shape[1]
        rb = next(k for k in range(16, r + 1, 16) if r % k == 0 and r // k <= steps)
        nblk = r // rb
        blk_of = lambda b, c, nblk=nblk: jnp.minimum(b * (nx + 1) + c, nblk - 1)
        cast_in.append(pl.BlockSpec((rb, cols),
                                    lambda b, c, o=layer * nblk, f=blk_of: (o + f(b, c), 0)))
        cast_out.append(pl.BlockSpec((rb, cols), lambda b, c, f=blk_of: (f(b, c), 0)))
        cast_shapes.append(jax.ShapeDtypeStruct((r, cols), BF16))
    vec = lambda n: pl.BlockSpec((1, n), lambda b, c: (0, 0))
    blk = lambda b, c: _phys_block(b, c, nb, nx)
    rows = lambda n, col0: pl.BlockSpec((CHUNK, n), lambda b, c: (blk(b, c), col0 // n))
    return pl.pallas_call(
        functools.partial(_seq_mix_kernel, n_cast=len(casts)),
        grid=(nb, nx + 1),
        in_specs=[
            rows(D_CONV, REST_XBC), rows(D_SSD, REST_Z), rows(LANES, REST_MISC),
            rows(D_POOL, REST_POOL),
            pl.BlockSpec((CONV_K, D_CONV), lambda b, c: (0, 0)),
            vec(D_CONV), vec(LANES), vec(LANES), vec(D_SSD), vec(D_SSD),
            pl.BlockSpec((POOL_GROUPS, POOL_GROUP_DIM, POOL_GROUP_DIM), lambda b, c: (0, 0, 0)),
            vec(D_POOL),
            pl.BlockSpec(expand.shape, lambda b, c: (0, 0)),
        ] + cast_in,
        out_specs=[rows(D_SSD, 0), rows(D_POOL, 0)] + cast_out,
        out_shape=[jax.ShapeDtypeStruct((t, D_SSD), BF16),
                   jax.ShapeDtypeStruct((t, D_POOL), BF16)] + cast_shapes,
        scratch_shapes=[pltpu.VMEM((SSD_STATE, D_SSD), F32),
                        pltpu.VMEM((CONV_HALO + CHUNK, D_CONV), F32),
                        pltpu.VMEM((POOL_HALO + CHUNK, D_POOL), F32)],
        compiler_params=_params("parallel", "arbitrary"),
        name="seq_mix",
    )(rest, rest, rest, rest, cw, cb, dtb, a_neg, dsk, nw, pw, ps, expand,
      *(a for a, _, _ in casts))


def _out_ln_kernel(ya_ref, yah_ref, yb_ref, yc_ref, h_ref, w_ref, g_ref, b_ref, o_ref, *,
                   n_main_tiles):
    tm = o_ref.shape[0]
    in_main = pl.program_id(0) < n_main_tiles
    for r in range(0, tm, tm // 2):
        rows = slice(r, r + tm // 2)
        ya = jnp.where(in_main, ya_ref[rows, :], yah_ref[rows, :])
        acc = jnp.dot(ya, w_ref[0:D_ATTN, :], preferred_element_type=F32)
        acc += jnp.dot(yb_ref[rows, :], w_ref[D_ATTN:D_ATTN + D_POOL, :],
                       preferred_element_type=F32)
        acc += jnp.dot(yc_ref[rows, :], w_ref[D_ATTN + D_POOL:, :], preferred_element_type=F32)
        o_ref[rows, :] = _layer_norm(ALPHA * h_ref[rows, :] + acc, g_ref[...], b_ref[...])


def _out_ln(ya, ya_h, yb, yc, h, w, g, b):
    t, d = h.shape
    tm = _pick_tile(ROW_TILES, ya.shape[0], ya_h.shape[0])
    n_main = ya.shape[0] // tm
    rows = lambda n: pl.BlockSpec((tm, n), lambda i: (i, 0))
    return pl.pallas_call(
        functools.partial(_out_ln_kernel, n_main_tiles=n_main),
        grid=(t // tm,),
        in_specs=[
            pl.BlockSpec((tm, D_ATTN), lambda i: (jnp.minimum(i, n_main - 1), 0)),
            pl.BlockSpec((tm, D_ATTN), lambda i: (jnp.maximum(i - n_main, 0), 0)),
            rows(D_POOL), rows(D_SSD), rows(d),
            pl.BlockSpec((d, d), lambda i: (0, 0)),
            pl.BlockSpec((1, d), lambda i: (0, 0)),
            pl.BlockSpec((1, d), lambda i: (0, 0)),
        ],
        out_specs=rows(d),
        out_shape=jax.ShapeDtypeStruct((t, d), F32),
        compiler_params=_params("parallel"),
        name="out_proj_ln",
    )(ya, ya_h, yb, yc, h, w, g, b)


def _lane_row(vals, offset):
    return jnp.zeros((1, LANES), F32).at[0, offset:offset + vals.shape[0]].set(vals.astype(F32))


def _proj_weight(w_in):
    wt = jnp.swapaxes(w_in, 1, 2)
    c = D_ATTN
    o = 3 * c + ATTN_HEADS
    f = wt[:, 3 * c:o]
    dt = wt[:, o + D_POOL + D_SSD + D_CONV:]
    pad = jnp.zeros((wt.shape[0], LANES - ATTN_HEADS - SSD_HEADS, wt.shape[2]), wt.dtype)
    segments = (wt[:, :c] * Q_SCALE,
                wt[:, c:3 * c],
                wt[:, o + D_POOL + D_SSD:o + D_POOL + D_SSD + D_CONV],
                wt[:, o:o + D_POOL],
                wt[:, o + D_POOL:o + D_POOL + D_SSD],
                jnp.concatenate([f, dt, pad], axis=1))
    return tuple(w.astype(BF16) for w in segments)


def _mixer(h, nb, nx, layer, w_proj, b_fgate, pool_w, pool_scale, conv_w, conv_b, dt_bias,
           a_log, d_skip, ssd_norm_w, w_out, ln_g, ln_b, casts):
    rest, qa, ka, vt, qa_h, ka_h, vt_h = _proj(h, w_proj, _lane_row(b_fgate, MISC_F0), layer,
                                               nb, nx)
    ya = _attention(qa, ka, vt, ka_h, vt_h).reshape(nb * nx * CHUNK, D_ATTN)
    ya_h = _head_attention(qa_h, ka_h, vt_h)
    yc, yb, *cast = _seq_mix(rest, conv_w, conv_b.reshape(1, D_CONV),
                             _lane_row(dt_bias, MISC_DT0),
                             _lane_row(-LOG2_E * jnp.exp(a_log.astype(F32)), MISC_DT0),
                             jnp.repeat(d_skip, SSD_HEAD_DIM).reshape(1, D_SSD),
                             ssd_norm_w.reshape(1, D_SSD), pool_w.astype(BF16),
                             pool_scale.reshape(1, D_POOL), nb, nx, casts + [w_out])
    return _out_ln(ya, ya_h, yb, yc, h, cast[-1], ln_g, ln_b), cast[:-1]


def kernel(x, meta, f1_gate, f1_up, f1_down, ln1_g, ln1_b, w_in, b_fgate, pool_w, pool_scale,
           conv_w, conv_b, dt_bias, a_log, d_skip, ssd_norm_w, w_out, ln2_g, ln2_b, f2_gate,
           f2_up, f2_down, ln3_g, ln3_b):
    nb, seq, d = x.shape
    assert d == D_MODEL and meta.shape == (N_META, D_MODEL) and seq % ATTN_TQ == 0
    nx = seq // CHUNK
    depth = f1_gate.shape[0]
    assert depth == DEPTH
    head = jnp.concatenate([jnp.zeros((PAD, d), x.dtype), meta.astype(x.dtype)], axis=0)
    heads = jnp.broadcast_to(head[None], (nb, CHUNK, d)).reshape(nb * CHUNK, d)
    row = lambda v: v.reshape(1, d)
    bf = lambda w: w.astype(BF16)
    w_proj = _proj_weight(w_in)
    flat = lambda w: w.reshape(depth * w.shape[1], w.shape[2])
    ffn1 = (bf(f1_gate[0]), bf(f1_up[0]), bf(f1_down[0]))
    h = x.reshape(nb * seq, d)
    for i in range(depth):
        h = _ffn_ln(h, *ffn1, row(ln1_g[i]), row(ln1_b[i]), tail=heads if i == 0 else None)
        pending = [f2_gate, f2_up, f2_down] + ([f1_gate, f1_up, f1_down] if i + 1 < depth else [])
        layers = [i] * 3 + [i + 1] * 3
        h, cast = _mixer(h, nb, nx, i, w_proj, b_fgate[i], pool_w[i], pool_scale[i], conv_w[i],
                         conv_b[i], dt_bias[i], a_log[i], d_skip[i], ssd_norm_w[i],
                         (flat(w_out), i, depth),
                         row(ln2_g[i]), row(ln2_b[i]),
                         [(flat(w), l, depth) for w, l in zip(pending, layers)])
        ffn2, ffn1 = cast[:3], (cast[3:] if i + 1 < depth else None)
        h = _ffn_ln(h, *ffn2, row(ln3_g[i]), row(ln3_b[i]),
                    rows=nb * seq if i == depth - 1 else None)
    return h.reshape(nb, seq, d)
```
